```python
import jax, jax.numpy as jnp
from jax import lax
import numpy as np

D_MODEL = 2048
BATCH = 2
SEQ = 4096
DEPTH = 1
DEC_BATCH = 8
DEC_SEQ = 8
PAST_LEN = 16384
PAGE_SIZE = 128

POOL_WIDTH = D_MODEL // 2
POOL_GROUPS = 4
POOL_GW = POOL_WIDTH // POOL_GROUPS
POOL_WINDOWS = (2, 4, 8, 16)
POOL_STATE = max(POOL_WINDOWS) - 1
HEAD_DIM = 128
N_HEADS = (D_MODEL - POOL_WIDTH) // HEAD_DIM
ATTN_WIDTH = N_HEADS * HEAD_DIM
IDX_HEADS = 16
IDX_DIM = 128
TOPK_MAX = 256
Q_BLOCK = 128
ROPE_THETA = 10000.0
D_FF = ((8 * D_MODEL // 3 + 255) // 256) * 256
EPS = 1e-6
NEG = -1e30
MIX_SPLITS = (POOL_WIDTH, ATTN_WIDTH, ATTN_WIDTH, ATTN_WIDTH, IDX_HEADS * IDX_DIM, IDX_DIM, IDX_HEADS)
D_IN = sum(MIX_SPLITS)

kernel_name = "hymba_pool_dsa_decoder_step"


def rmsnorm(x, g):
    xf = x.astype(jnp.float32)
    y = xf * lax.rsqrt(jnp.mean(xf * xf, axis=-1, keepdims=True) + EPS)
    return (y * g.astype(jnp.float32)).astype(x.dtype)


def rope(x, pos):
    half = x.shape[-1] // 2
    inv = ROPE_THETA ** (-jnp.arange(half, dtype=jnp.float32) / half)
    ang = pos.astype(jnp.float32)[:, None] * inv[None, :]
    cos = jnp.cos(ang)[None, :, None, :]
    sin = jnp.sin(ang)[None, :, None, :]
    xf = x.astype(jnp.float32)
    x1, x2 = xf[..., :half], xf[..., half:]
    return jnp.concatenate([x1 * cos - x2 * sin, x2 * cos + x1 * sin], axis=-1).astype(x.dtype)


def mix_inputs(h, w_in, pos):
    B, T = h.shape[0], h.shape[1]
    u = h @ w_in
    offs = [int(o) for o in np.cumsum(MIX_SPLITS)[:-1]]
    up, q, k, v, qi, ki, wi = jnp.split(u, offs, axis=-1)
    q = rope(q.reshape(B, T, N_HEADS, HEAD_DIM), pos)
    k = rope(k.reshape(B, T, N_HEADS, HEAD_DIM), pos)
    v = v.reshape(B, T, N_HEADS, HEAD_DIM)
    qi = rope(qi.reshape(B, T, IDX_HEADS, IDX_DIM), pos)
    ki = rope(ki[:, :, None, :], pos)[:, :, 0]
    wi = wi * (IDX_HEADS ** -0.5)
    return up, q, k, v, qi, ki, wi


def pool_mixer(up, prefix, pos, w_pool, s_pool):
    B, T = up.shape[0], up.shape[1]
    P = POOL_STATE
    ext = jnp.concatenate([prefix, up], axis=1)
    c = jnp.pad(jnp.cumsum(ext.astype(jnp.float32), axis=1), ((0, 0), (1, 0), (0, 0)))
    means = []
    for g, w in enumerate(POOL_WINDOWS):
        sl = slice(g * POOL_GW, (g + 1) * POOL_GW)
        s = c[:, P + 1:P + 1 + T, sl] - c[:, P + 1 - w:P + 1 - w + T, sl]
        cnt = jnp.minimum(w, pos + 1).astype(jnp.float32)[None, :, None]
        means.append(s / cnt)
    mean = jnp.stack(means, axis=2)
    d = (mean - up.astype(jnp.float32).reshape(B, T, POOL_GROUPS, POOL_GW)).astype(up.dtype)
    y = jnp.einsum('btgc,gcd->btgd', d, w_pool).reshape(B, T, POOL_WIDTH) * s_pool
    return y, ext[:, -P:]


def index_scores(qi, ki, wi):
    s = jnp.einsum('bqhe,bse->bqhs', qi, ki, preferred_element_type=jnp.float32) * (IDX_DIM ** -0.5)
    return jnp.einsum('bqhs,bqh->bqs', jax.nn.relu(s), wi.astype(jnp.float32))


def attend(q, kg, vg, valid):
    s = jnp.einsum('bqhd,bqkhd->bqhk', q, kg, preferred_element_type=jnp.float32) * (HEAD_DIM ** -0.5)
    s = jnp.where(valid[:, :, None, :], s, NEG)
    p = jax.nn.softmax(s, axis=-1).astype(vg.dtype)
    return jnp.einsum('bqhk,bqkhd->bqhd', p, vg)


def sparse_attn_prompt(q, k, v, qi, ki, wi):
    B, S = q.shape[0], q.shape[1]
    topk = min(TOPK_MAX, S // 4)
    nb = S // Q_BLOCK
    bi = jnp.arange(B)[:, None, None]
    key_pos = jnp.arange(S)

    def blk(args):
        qb, qib, wib, start = args
        qpos = start + jnp.arange(Q_BLOCK)
        sc = index_scores(qib, ki, wib)
        sc = jnp.where((key_pos[None, :] <= qpos[:, None])[None], sc, -jnp.inf)
        _, idx = lax.top_k(sc, topk)
        valid = idx <= qpos[None, :, None]
        return attend(qb, k[bi, idx], v[bi, idx], valid)

    def to_blocks(a):
        return jnp.moveaxis(a.reshape((B, nb, Q_BLOCK) + a.shape[2:]), 1, 0)

    starts = jnp.arange(nb) * Q_BLOCK
    out = lax.map(blk, (to_blocks(q), to_blocks(qi), to_blocks(wi), starts))
    return jnp.moveaxis(out, 0, 1).reshape(B, S, ATTN_WIDTH)


def sparse_attn_sample(q, k, v, qi, ki, wi, cache_k, cache_v, cache_idx_k, page_table, layer):
    B, T = q.shape[0], q.shape[1]
    n_pages = page_table.shape[1]
    past = n_pages * PAGE_SIZE
    L = past + T
    topk = min(TOPK_MAX, L // 4)
    bi = jnp.arange(B)[:, None, None]
    ik_past = cache_idx_k[layer, page_table].reshape(B, past, IDX_DIM)
    ik_all = jnp.concatenate([ik_past, ki], axis=1)
    qpos = past + jnp.arange(T)
    sc = index_scores(qi, ik_all, wi)
    sc = jnp.where((jnp.arange(L)[None, :] <= qpos[:, None])[None], sc, -jnp.inf)
    _, idx = lax.top_k(sc, topk)
    valid = idx <= qpos[None, :, None]
    in_past = (idx < past)[..., None, None]
    pidx = jnp.minimum(idx, past - 1)
    phys = page_table[bi, pidx // PAGE_SIZE]
    off = pidx % PAGE_SIZE
    nidx = jnp.clip(idx - past, 0, T - 1)
    kg = jnp.where(in_past, cache_k[layer, phys, off], k[bi, nidx])
    vg = jnp.where(in_past, cache_v[layer, phys, off], v[bi, nidx])
    return attend(q, kg, vg, valid).reshape(B, T, ATTN_WIDTH)


def block(x, pos, pool_prefix, attn_fn, g_mix, w_in, w_pool, s_pool, w_out, g_ffn, w_gate, w_up, w_down):
    h = rmsnorm(x, g_mix)
    up, q, k, v, qi, ki, wi = mix_inputs(h, w_in, pos)
    y_pool, pool_state = pool_mixer(up, pool_prefix, pos, w_pool, s_pool)
    y_attn = attn_fn(q, k, v, qi, ki, wi)
    x = x + jnp.concatenate([y_pool, y_attn], axis=-1) @ w_out
    h2 = rmsnorm(x, g_ffn)
    x = x + (jax.nn.silu(h2 @ w_gate) * (h2 @ w_up)) @ w_down
    return x, k, v, ki, pool_state


def setup_inputs(seed: int = 0) -> dict:
    key = jax.random.key(seed)
    ks = jax.random.split(key, 20)
    f32 = jnp.float32
    n_pages = PAST_LEN // PAGE_SIZE
    n_used = DEC_BATCH * n_pages
    n_phys = n_used + max(1, n_used // 4)
    nrm = lambda k, shape, scale=1.0: jax.random.normal(k, shape, f32) * scale
    page_table = jax.random.permutation(ks[6], n_phys)[:n_used].reshape(DEC_BATCH, n_pages).astype(jnp.int32)
    return {
        'x_prompt': nrm(ks[0], (BATCH, SEQ, D_MODEL)),
        'x_sample': nrm(ks[1], (DEC_BATCH, DEC_SEQ, D_MODEL)),
        'cache_k': nrm(ks[2], (DEPTH, n_phys, PAGE_SIZE, N_HEADS, HEAD_DIM)),
        'cache_v': nrm(ks[3], (DEPTH, n_phys, PAGE_SIZE, N_HEADS, HEAD_DIM)),
        'cache_idx_k': nrm(ks[4], (DEPTH, n_phys, PAGE_SIZE, IDX_DIM)),
        'state_pool': nrm(ks[5], (DEPTH, DEC_BATCH, POOL_STATE, POOL_WIDTH)),
        'page_table': page_table,
        'g_mix': 1.0 + nrm(ks[7], (DEPTH, D_MODEL), 0.02),
        'w_in': nrm(ks[8], (DEPTH, D_MODEL, D_IN), D_MODEL ** -0.5),
        'w_pool': nrm(ks[9], (DEPTH, POOL_GROUPS, POOL_GW, POOL_GW), POOL_GW ** -0.5),
        's_pool': 1.0 + nrm(ks[10], (DEPTH, POOL_WIDTH), 0.1),
        'w_out': nrm(ks[11], (DEPTH, D_MODEL, D_MODEL), D_MODEL ** -0.5),
        'g_ffn': 1.0 + nrm(ks[12], (DEPTH, D_MODEL), 0.02),
        'w_gate': nrm(ks[13], (DEPTH, D_MODEL, D_FF), D_MODEL ** -0.5),
        'w_up': nrm(ks[14], (DEPTH, D_MODEL, D_FF), D_MODEL ** -0.5),
        'w_down': nrm(ks[15], (DEPTH, D_FF, D_MODEL), D_FF ** -0.5),
        'g_final': 1.0 + nrm(ks[16], (D_MODEL,), 0.02),
    }


def reference(x_prompt, x_sample, cache_k, cache_v, cache_idx_k, state_pool, page_table,
              g_mix, w_in, w_pool, s_pool, w_out, g_ffn, w_gate, w_up, w_down, g_final):
    B, S = x_prompt.shape[0], x_prompt.shape[1]
    DB, T = x_sample.shape[0], x_sample.shape[1]
    past = page_table.shape[1] * PAGE_SIZE
    pos_p = jnp.arange(S)
    pos_s = past + jnp.arange(T)
    zero_prefix = jnp.zeros((B, POOL_STATE, POOL_WIDTH), x_prompt.dtype)

    xp, xs = x_prompt, x_sample
    kp_l, vp_l, ikp_l, pp_l = [], [], [], []
    ks_l, vs_l, iks_l, ps_l = [], [], [], []
    for l in range(DEPTH):
        lw = (g_mix[l], w_in[l], w_pool[l], s_pool[l], w_out[l], g_ffn[l], w_gate[l], w_up[l], w_down[l])
        xp, kp, vp, ikp, pp = block(xp, pos_p, zero_prefix, sparse_attn_prompt, *lw)

        def attn_s(q, k, v, qi, ki, wi, l=l):
            return sparse_attn_sample(q, k, v, qi, ki, wi, cache_k, cache_v, cache_idx_k, page_table, l)

        xs, ksm, vsm, iks, ps = block(xs, pos_s, state_pool[l], attn_s, *lw)
        kp_l.append(kp); vp_l.append(vp); ikp_l.append(ikp); pp_l.append(pp)
        ks_l.append(ksm); vs_l.append(vsm); iks_l.append(iks); ps_l.append(ps)

    y_prompt = rmsnorm(xp, g_final)
    y_sample = rmsnorm(xs, g_final)
    k_prompt = jnp.stack(kp_l); v_prompt = jnp.stack(vp_l)
    idx_k_prompt = jnp.stack(ikp_l); pool_prompt = jnp.stack(pp_l)
    k_sample = jnp.stack(ks_l); v_sample = jnp.stack(vs_l)
    idx_k_sample = jnp.stack(iks_l); pool_sample = jnp.stack(ps_l)
    return (y_prompt, y_sample, k_prompt, v_prompt, idx_k_prompt, pool_prompt, k_sample, v_sample, idx_k_sample, pool_sample)
```

```python
import functools

import jax
import jax.numpy as jnp
from jax import lax
from jax.experimental import pallas as pl
from jax.experimental.pallas import tpu as pltpu

F32 = jnp.float32
BF16 = jnp.bfloat16

LANES = 128
HEAD_DIM = 128
IDX_DIM = 128
IDX_HEADS = 16
POOL_GROUPS = 4
POOL_WINDOWS = (2, 4, 8, 16)
POOL_STATE = max(POOL_WINDOWS) - 1
HALO = 16
TOPK_MAX = 256
Q_BLOCK = 128
ROPE_THETA = 10000.0
EPS = 1e-6
NEG = -1e30
MAX_BISECT = 200
VMEM_LIMIT = 56 * 1024 * 1024

NT_DIMS = (((1,), (1,)), ((), ()))


def _cparams(n_grid):
    return pltpu.CompilerParams(dimension_semantics=("arbitrary",) * n_grid, vmem_limit_bytes=VMEM_LIMIT)


def _rms(x, g):
    return x * lax.rsqrt(jnp.mean(x * x, axis=-1, keepdims=True) + EPS) * g


def _rmsnorm_kernel(x_ref, g_ref, o_ref):
    o_ref[...] = _rms(x_ref[...], g_ref[...]).astype(o_ref.dtype)


def rmsnorm_bf16(x, g, tm):
    m, d = x.shape
    return pl.pallas_call(
        _rmsnorm_kernel,
        grid=(m // tm,),
        in_specs=[pl.BlockSpec((tm, d), lambda i: (i, 0)), pl.BlockSpec((1, d), lambda i: (0, 0))],
        out_specs=pl.BlockSpec((tm, d), lambda i: (i, 0)),
        out_shape=jax.ShapeDtypeStruct((m, d), BF16),
        compiler_params=_cparams(1),
        name="rmsnorm",
    )(x, g.reshape(1, d))


def _proj_kernel(*refs, rope, scale, n_out):
    if rope:
        h_ref, w_ref, cos_ref, sin_ref = refs[:4]
        out_refs = refs[4:]
    else:
        h_ref, w_ref = refs[:2]
        out_refs = refs[2:]
    assert len(out_refs) == n_out
    y = jnp.dot(h_ref[...], w_ref[...], preferred_element_type=F32)
    if scale is not None:
        y = y * scale
    tn = y.shape[1]
    if rope:
        cos = cos_ref[...]
        sin = sin_ref[...]
        for g in range(tn // HEAD_DIM):
            sl = slice(g * HEAD_DIM, (g + 1) * HEAD_DIM)
            yg = y[:, sl]
            r = yg * cos + pltpu.roll(yg, HEAD_DIM // 2, axis=1) * sin
            for o_ref in out_refs:
                o_ref[:, sl] = r.astype(o_ref.dtype)
    else:
        for o_ref in out_refs:
            o_ref[...] = y.astype(o_ref.dtype)


def proj(h, w, out_dtypes, *, tm, tn, cos=None, sin=None, scale=None, name="proj"):
    m, k = h.shape
    n = w.shape[1]
    rope = cos is not None
    in_specs = [pl.BlockSpec((tm, k), lambda j, i: (i, 0)), pl.BlockSpec((k, tn), lambda j, i: (0, j))]
    args = [h, w]
    if rope:
        in_specs += [pl.BlockSpec((tm, HEAD_DIM), lambda j, i: (i, 0))] * 2
        args += [cos, sin]
    outs = pl.pallas_call(
        functools.partial(_proj_kernel, rope=rope, scale=scale, n_out=len(out_dtypes)),
        grid=(n // tn, m // tm),
        in_specs=in_specs,
        out_specs=[pl.BlockSpec((tm, tn), lambda j, i: (i, j)) for _ in out_dtypes],
        out_shape=[jax.ShapeDtypeStruct((m, n), dt) for dt in out_dtypes],
        compiler_params=_cparams(2),
        name=name,
    )(*args)
    return outs


def _pool_kernel(*refs, tm, pos0, has_prev):
    if has_prev:
        up_ref, prev_ref, prefix_ref, w_ref, s_ref, o_ref, ext_ref = refs
    else:
        up_ref, prefix_ref, w_ref, s_ref, o_ref, ext_ref = refs
    i = pl.program_id(1)
    if has_prev:
        halo = jnp.where(i == 0, prefix_ref[...], prev_ref[...])
    else:
        halo = prefix_ref[...]
    ext_ref[0:HALO, :] = halo
    ext_ref[HALO:HALO + tm, :] = up_ref[...]
    pos = pos0 + i * tm + lax.broadcasted_iota(jnp.int32, (tm, 1), 0)
    gw = up_ref.shape[1] // POOL_GROUPS
    for g, w in enumerate(POOL_WINDOWS):
        sl = slice(g * gw, (g + 1) * gw)
        cur = ext_ref[HALO:HALO + tm, sl]
        s = cur
        for j in range(1, w):
            s = s + ext_ref[HALO - j:HALO - j + tm, sl]
        cnt = jnp.minimum(w, pos + 1).astype(F32)
        d = s / cnt - cur
        y = jnp.dot(d.astype(BF16), w_ref[g], preferred_element_type=F32) * s_ref[:, sl]
        o_ref[:, sl] = y.astype(o_ref.dtype)


def pool_mixer(up, prefix16, w_pool, s_pool, *, tm, pos0):
    b, t, wd = up.shape
    has_prev = t > tm
    gw = wd // POOL_GROUPS
    in_specs = [pl.BlockSpec((None, tm, wd), lambda bi, i: (bi, i, 0))]
    args = [up]
    if has_prev:
        r = tm // HALO
        in_specs.append(pl.BlockSpec((None, HALO, wd), lambda bi, i: (bi, jnp.maximum(i * r - 1, 0), 0)))
        args.append(up)
    in_specs += [
        pl.BlockSpec((None, HALO, wd), lambda bi, i: (bi, 0, 0)),
        pl.BlockSpec((POOL_GROUPS, gw, gw), lambda bi, i: (0, 0, 0)),
        pl.BlockSpec((1, wd), lambda bi, i: (0, 0)),
    ]
    args += [prefix16, w_pool, s_pool.reshape(1, wd)]
    return pl.pallas_call(
        functools.partial(_pool_kernel, tm=tm, pos0=pos0, has_prev=has_prev),
        grid=(b, t // tm),
        in_specs=in_specs,
        out_specs=pl.BlockSpec((None, tm, wd), lambda bi, i: (bi, i, 0)),
        out_shape=jax.ShapeDtypeStruct((b, t, wd), BF16),
        scratch_shapes=[pltpu.VMEM((HALO + tm, wd), F32)],
        compiler_params=_cparams(2),
        name="pool_mixer",
    )(*args)


def _bisect_threshold(count_ge, rmin, rmax, n_valid, kq):
    c_hi = count_ge(rmax)
    top_tied = c_hi >= kq
    lo0 = jnp.where(top_tied, rmax, rmin)
    cl0 = jnp.where(top_tied, c_hi, n_valid)

    def not_done(cl, stuck):
        return jnp.max(jnp.where((cl == kq) | stuck, 0.0, 1.0))

    def cond(st):
        it, flag = st[0], st[1]
        return jnp.logical_and(it < MAX_BISECT, flag > 0.0)

    def body(st):
        it, _, lo, hi, cl = st
        mid = lo + 0.5 * (hi - lo)
        stuck = (mid <= lo) | (mid >= hi)
        c = count_ge(mid)
        ge = c >= kq
        lo2 = jnp.where(ge, mid, lo)
        cl2 = jnp.where(ge, c, cl)
        hi2 = jnp.where(ge, hi, mid)
        return it + 1, not_done(cl2, stuck), lo2, hi2, cl2

    st = (jnp.int32(0), not_done(cl0, lo0 >= rmax), lo0, rmax, cl0)
    st = lax.while_loop(cond, body, st)
    return st[2]


def _attn_prompt_kernel(q_ref, qi_ref, wi_ref, k_ref, v_ref, ki_ref, o_ref,
                        sc_ref, wb_ref, m_ref, l_ref, acc_ref, *, tk, topk, n_heads):
    i = pl.program_id(1)
    tq = q_ref.shape[0]
    sub = tk // LANES
    nc = (i * tq) // tk + 1
    qpos = i * tq + lax.broadcasted_iota(jnp.int32, (tq, 1), 0)
    lane = lax.broadcasted_iota(jnp.int32, (tq, LANES), 1)
    idx_scale = IDX_DIM ** -0.5
    att_scale = HEAD_DIM ** -0.5

    wi = wi_ref[...] * idx_scale
    for h in range(IDX_HEADS):
        wb_ref[h] = jnp.broadcast_to(wi[:, h:h + 1], (tq, LANES))

    def idx_chunk(c, carry):
        off = pl.multiple_of(c * tk, tk)
        kic = ki_ref[pl.ds(off, tk), :]
        accs = [jnp.zeros((tq, LANES), F32) for _ in range(sub)]
        for h in range(IDX_HEADS):
            s = lax.dot_general(qi_ref[:, h * IDX_DIM:(h + 1) * IDX_DIM], kic, NT_DIMS,
                                preferred_element_type=F32)
            wbh = wb_ref[h]
            for j in range(sub):
                accs[j] = accs[j] + jnp.maximum(s[:, j * LANES:(j + 1) * LANES], 0.0) * wbh
        for j in range(sub):
            kpos = off + j * LANES + lane
            sc_ref[c * sub + j] = jnp.where(kpos <= qpos, accs[j], -jnp.inf)
        return carry

    lax.fori_loop(0, nc, idx_chunk, 0)

    def reduce_rows(fn, init, red):
        def body(c, a):
            for j in range(sub):
                a = fn(a, sc_ref[c * sub + j])
            return a
        a = lax.fori_loop(0, nc, body, jnp.full((tq, LANES), init, F32))
        return red(a, axis=1, keepdims=True)

    def count_ge(t):
        tb = jnp.broadcast_to(t, (tq, LANES))
        return reduce_rows(lambda a, blk: a + jnp.where(blk >= tb, 1.0, 0.0), 0.0, jnp.sum)

    rmax = reduce_rows(jnp.maximum, -jnp.inf, jnp.max)
    rmin = reduce_rows(lambda a, blk: jnp.minimum(a, jnp.where(blk == -jnp.inf, jnp.inf, blk)), jnp.inf, jnp.min)
    n_valid = (qpos + 1).astype(F32)
    kq = jnp.minimum(float(topk), n_valid)
    thr = _bisect_threshold(count_ge, rmin, rmax, n_valid, kq)
    thr_b = jnp.broadcast_to(thr, (tq, LANES))

    m_ref[...] = jnp.full(m_ref.shape, -jnp.inf, F32)
    l_ref[...] = jnp.zeros(l_ref.shape, F32)
    acc_ref[...] = jnp.zeros(acc_ref.shape, F32)

    def att_chunk(c, carry):
        off = pl.multiple_of(c * tk, tk)
        masks = [sc_ref[c * sub + j] >= thr_b for j in range(sub)]
        for h in range(n_heads):
            hs = slice(h * HEAD_DIM, (h + 1) * HEAD_DIM)
            s = lax.dot_general(q_ref[:, hs], k_ref[pl.ds(off, tk), hs], NT_DIMS,
                                preferred_element_type=F32) * att_scale
            ss = [jnp.where(masks[j], s[:, j * LANES:(j + 1) * LANES], NEG) for j in range(sub)]
            m_prev = m_ref[h]
            m_cur = ss[0]
            for j in range(1, sub):
                m_cur = jnp.maximum(m_cur, ss[j])
            m_new = jnp.maximum(m_prev, jnp.max(m_cur, axis=1, keepdims=True))
            alpha = jnp.exp(m_prev - m_new)
            ps = [jnp.exp(ss[j] - m_new) for j in range(sub)]
            p_sum = ps[0]
            for j in range(1, sub):
                p_sum = p_sum + ps[j]
            l_ref[h] = alpha * l_ref[h] + jnp.sum(p_sum, axis=1, keepdims=True)
            m_ref[h] = m_new
            p = jnp.concatenate(ps, axis=1).astype(BF16)
            pv = jnp.dot(p, v_ref[pl.ds(off, tk), hs], preferred_element_type=F32)
            acc_ref[:, hs] = alpha * acc_ref[:, hs] + pv
        return carry

    lax.fori_loop(0, nc, att_chunk, 0)

    for h in range(n_heads):
        hs = slice(h * HEAD_DIM, (h + 1) * HEAD_DIM)
        o_ref[:, hs] = (acc_ref[:, hs] / l_ref[h]).astype(o_ref.dtype)


def attn_prompt(q, qi, wi, k, v, ki, *, topk, tk=256):
    b, s, aw = q.shape
    n_heads = aw // HEAD_DIM
    tq = Q_BLOCK
    kern = functools.partial(_attn_prompt_kernel, tk=tk, topk=topk, n_heads=n_heads)
    return pl.pallas_call(
        kern,
        grid=(b, s // tq),
        in_specs=[
            pl.BlockSpec((None, tq, aw), lambda bi, i: (bi, i, 0)),
            pl.BlockSpec((None, tq, qi.shape[2]), lambda bi, i: (bi, i, 0)),
            pl.BlockSpec((None, tq, LANES), lambda bi, i: (bi, i, 0)),
            pl.BlockSpec((None, s, aw), lambda bi, i: (bi, 0, 0)),
            pl.BlockSpec((None, s, aw), lambda bi, i: (bi, 0, 0)),
            pl.BlockSpec((None, s, IDX_DIM), lambda bi, i: (bi, 0, 0)),
        ],
        out_specs=pl.BlockSpec((None, tq, aw), lambda bi, i: (bi, i, 0)),
        out_shape=jax.ShapeDtypeStruct((b, s, aw), BF16),
        scratch_shapes=[
            pltpu.VMEM((s // LANES, tq, LANES), F32),
            pltpu.VMEM((IDX_HEADS, tq, LANES), F32),
            pltpu.VMEM((n_heads, tq, LANES), F32),
            pltpu.VMEM((n_heads, tq, LANES), F32),
            pltpu.VMEM((tq, aw), F32),
        ],
        compiler_params=_cparams(2),
        name="attn_prompt",
    )(q, qi, wi, k, v, ki)


def _idx_sample_kernel(pt_ref, qi_ref, w_ref, cik_ref, kin_ref, sc_ref, thr_ref, *, n_pages, t_new, topk):
    p = pl.program_id(1)
    rows = qi_ref.shape[0]

    def scores(keys_bf16):
        s = lax.dot_general(qi_ref[...], keys_bf16, NT_DIMS, preferred_element_type=F32)
        r = jnp.maximum(s, 0.0) * w_ref[...]
        return jnp.sum(r.reshape(rows // t_new, t_new, LANES), axis=0)

    @pl.when(p < n_pages)
    def _():
        sc_ref[p] = scores(cik_ref[...].astype(BF16))

    @pl.when(p == n_pages)
    def _():
        s = scores(kin_ref[...])
        tok = lax.broadcasted_iota(jnp.int32, (t_new, LANES), 0)
        lane = lax.broadcasted_iota(jnp.int32, (t_new, LANES), 1)
        sc_ref[n_pages] = jnp.where((lane <= tok) & (lane < t_new), s, -jnp.inf)

        def reduce_rows(fn, init, red):
            a = lax.fori_loop(0, n_pages + 1, lambda c, a: fn(a, sc_ref[c]), jnp.full((t_new, LANES), init, F32))
            return red(a, axis=1, keepdims=True)

        def count_ge(t):
            tb = jnp.broadcast_to(t, (t_new, LANES))
            return reduce_rows(lambda a, blk: a + jnp.where(blk >= tb, 1.0, 0.0), 0.0, jnp.sum)

        rmax = reduce_rows(jnp.maximum, -jnp.inf, jnp.max)
        rmin = reduce_rows(lambda a, blk: jnp.minimum(a, jnp.where(blk == -jnp.inf, jnp.inf, blk)),
                           jnp.inf, jnp.min)
        tokc = lax.broadcasted_iota(jnp.int32, (t_new, 1), 0)
        n_valid = (n_pages * LANES + tokc + 1).astype(F32)
        kq = jnp.minimum(float(topk), n_valid)
        thr = _bisect_threshold(count_ge, rmin, rmax, n_valid, kq)
        thr_ref[...] = jnp.broadcast_to(thr, (t_new, LANES))


def idx_sample(page_table, qi_ht, w_ht, cache_idx_k, ki_new_pad, *, t_new, topk):
    db, n_pages = page_table.shape
    rows = qi_ht.shape[1]
    page = cache_idx_k.shape[2]
    kern = functools.partial(_idx_sample_kernel, n_pages=n_pages, t_new=t_new, topk=topk)
    grid_spec = pltpu.PrefetchScalarGridSpec(
        num_scalar_prefetch=1,
        grid=(db, n_pages + 1),
        in_specs=[
            pl.BlockSpec((None, rows, IDX_DIM), lambda b, p, pt: (b, 0, 0)),
            pl.BlockSpec((None, rows, LANES), lambda b, p, pt: (b, 0, 0)),
            pl.BlockSpec((None, None, page, IDX_DIM),
                         lambda b, p, pt: (0, pt[b, jnp.minimum(p, n_pages - 1)], 0, 0)),
            pl.BlockSpec((None, page, IDX_DIM), lambda b, p, pt: (b, 0, 0)),
        ],
        out_specs=[
            pl.BlockSpec((None, n_pages + 1, t_new, LANES), lambda b, p, pt: (b, 0, 0, 0)),
            pl.BlockSpec((None, t_new, LANES), lambda b, p, pt: (b, 0, 0)),
        ],
    )
    return pl.pallas_call(
        kern,
        grid_spec=grid_spec,
        out_shape=[jax.ShapeDtypeStruct((db, n_pages + 1, t_new, LANES), F32),
                   jax.ShapeDtypeStruct((db, t_new, LANES), F32)],
        compiler_params=_cparams(2),
        name="idx_sample",
    )(page_table, qi_ht, w_ht, cache_idx_k, ki_new_pad)


def _attn_sample_kernel(pt_ref, q_ref, ck_ref, cv_ref, kn_ref, vn_ref, sc_ref, thr_ref, o_ref,
                        m_ref, l_ref, acc_ref, *, n_pages, t_new):
    p = pl.program_id(1)
    rows, aw = q_ref.shape
    att_scale = HEAD_DIM ** -0.5

    @pl.when(p == 0)
    def _():
        m_ref[...] = jnp.full(m_ref.shape, -jnp.inf, F32)
        l_ref[...] = jnp.zeros(l_ref.shape, F32)
        acc_ref[...] = jnp.zeros(acc_ref.shape, F32)

    def rep_rows(a):
        return jnp.concatenate([a] * (rows // t_new), axis=0)

    def rep_lanes(a):
        return jnp.concatenate([a] * (aw // LANES), axis=1)

    def step(kp, vp):
        s = lax.dot_general(q_ref[...], kp, NT_DIMS, preferred_element_type=F32) * att_scale
        s = jnp.where(rep_rows(sc_ref[...]) >= rep_rows(thr_ref[...]), s, NEG)
        m_prev = m_ref[...]
        m_new = jnp.maximum(m_prev, jnp.max(s, axis=1, keepdims=True))
        alpha = jnp.exp(m_prev - m_new)
        pr = jnp.exp(s - m_new)
        l_ref[...] = alpha * l_ref[...] + jnp.sum(pr, axis=1, keepdims=True)
        m_ref[...] = m_new
        pv = jnp.dot(pr.astype(BF16), vp, preferred_element_type=F32)
        acc_ref[...] = rep_lanes(alpha) * acc_ref[...] + pv

    @pl.when(p < n_pages)
    def _():
        step(ck_ref[...].astype(BF16), cv_ref[...].astype(BF16))

    @pl.when(p == n_pages)
    def _():
        step(kn_ref[...], vn_ref[...])
        o_ref[...] = acc_ref[...] / rep_lanes(l_ref[...])


def attn_sample(page_table, q_bd, cache_k, cache_v, k_new_pad, v_new_pad, sc, thr, *, t_new):
    db, n_pages = page_table.shape
    rows, aw = q_bd.shape[1], q_bd.shape[2]
    page = cache_k.shape[2]
    kern = functools.partial(_attn_sample_kernel, n_pages=n_pages, t_new=t_new)
    cache_spec = pl.BlockSpec((None, None, page, aw), lambda b, p, pt: (0, pt[b, jnp.minimum(p, n_pages - 1)], 0, 0))
    new_spec = pl.BlockSpec((None, page, aw), lambda b, p, pt: (b, 0, 0))
    grid_spec = pltpu.PrefetchScalarGridSpec(
        num_scalar_prefetch=1,
        grid=(db, n_pages + 1),
        in_specs=[
            pl.BlockSpec((None, rows, aw), lambda b, p, pt: (b, 0, 0)),
            cache_spec, cache_spec, new_spec, new_spec,
            pl.BlockSpec((None, None, t_new, LANES), lambda b, p, pt: (b, p, 0, 0)),
            pl.BlockSpec((None, t_new, LANES), lambda b, p, pt: (b, 0, 0)),
        ],
        out_specs=pl.BlockSpec((None, rows, aw), lambda b, p, pt: (b, 0, 0)),
        scratch_shapes=[
            pltpu.VMEM((rows, LANES), F32),
            pltpu.VMEM((rows, LANES), F32),
            pltpu.VMEM((rows, aw), F32),
        ],
    )
    return pl.pallas_call(
        kern,
        grid_spec=grid_spec,
        out_shape=jax.ShapeDtypeStruct((db, rows, aw), F32),
        compiler_params=_cparams(2),
        name="attn_sample",
    )(page_table, q_bd, cache_k, cache_v, k_new_pad, v_new_pad, sc, thr)


def _outproj_kernel(x_ref, yp_ref, ya_ref, wt_ref, wb_ref, g_ref, x2_ref, h2_ref):
    y = jnp.dot(yp_ref[...], wt_ref[...], preferred_element_type=F32)
    y = y + jnp.dot(ya_ref[...], wb_ref[...], preferred_element_type=F32)
    x2 = x_ref[...] + y
    x2_ref[...] = x2
    h2_ref[...] = _rms(x2, g_ref[...]).astype(h2_ref.dtype)


def outproj(x, yp, ya, w_top, w_bot, g, *, tm):
    m, d = x.shape
    kp, ka = yp.shape[1], ya.shape[1]
    return pl.pallas_call(
        _outproj_kernel,
        grid=(m // tm,),
        in_specs=[
            pl.BlockSpec((tm, d), lambda i: (i, 0)),
            pl.BlockSpec((tm, kp), lambda i: (i, 0)),
            pl.BlockSpec((tm, ka), lambda i: (i, 0)),
            pl.BlockSpec((kp, d), lambda i: (0, 0)),
            pl.BlockSpec((ka, d), lambda i: (0, 0)),
            pl.BlockSpec((1, d), lambda i: (0, 0)),
        ],
        out_specs=[pl.BlockSpec((tm, d), lambda i: (i, 0)), pl.BlockSpec((tm, d), lambda i: (i, 0))],
        out_shape=[jax.ShapeDtypeStruct((m, d), F32), jax.ShapeDtypeStruct((m, d), BF16)],
        compiler_params=_cparams(1),
        name="outproj",
    )(x, yp, ya, w_top, w_bot, g.reshape(1, d))


def _ffn_kernel(h2_ref, wg_ref, wu_ref, wd_ref, x2_ref, g_ref, o_ref, acc_ref):
    f = pl.program_id(1)

    @pl.when(f == 0)
    def _():
        acc_ref[...] = jnp.zeros(acc_ref.shape, F32)

    h2 = h2_ref[...]
    gate = jnp.dot(h2, wg_ref[...], preferred_element_type=F32)
    up = jnp.dot(h2, wu_ref[...], preferred_element_type=F32)
    a = (gate * jax.nn.sigmoid(gate) * up).astype(BF16)
    acc_ref[...] += jnp.dot(a, wd_ref[...], preferred_element_type=F32)

    @pl.when(f == pl.num_programs(1) - 1)
    def _():
        o_ref[...] = _rms(x2_ref[...] + acc_ref[...], g_ref[...])


def ffn(h2, wg, wu, wd, x2, g, *, tm, tf):
    m, d = h2.shape
    dff = wg.shape[1]
    return pl.pallas_call(
        _ffn_kernel,
        grid=(m // tm, dff // tf),
        in_specs=[
            pl.BlockSpec((tm, d), lambda i, f: (i, 0)),
            pl.BlockSpec((d, tf), lambda i, f: (0, f)),
            pl.BlockSpec((d, tf), lambda i, f: (0, f)),
            pl.BlockSpec((tf, d), lambda i, f: (f, 0)),
            pl.BlockSpec((tm, d), lambda i, f: (i, 0)),
            pl.BlockSpec((1, d), lambda i, f: (0, 0)),
        ],
        out_specs=pl.BlockSpec((tm, d), lambda i, f: (i, 0)),
        out_shape=jax.ShapeDtypeStruct((m, d), F32),
        scratch_shapes=[pltpu.VMEM((tm, d), F32)],
        compiler_params=_cparams(2),
        name="ffn",
    )(h2, wg, wu, wd, x2, g.reshape(1, d))


def _rope_tables(pos):
    half = HEAD_DIM // 2
    inv = ROPE_THETA ** (-jnp.arange(half, dtype=F32) / half)
    ang = pos.astype(F32)[:, None] * inv[None, :]
    cos, sin = jnp.cos(ang), jnp.sin(ang)
    return jnp.concatenate([cos, cos], axis=-1), jnp.concatenate([-sin, sin], axis=-1)


def _mix_inputs(x2d, pos_rows, g_mix, w_segs, *, tm):
    cos, sin = _rope_tables(pos_rows)
    h = rmsnorm_bf16(x2d, g_mix, tm)
    w_up, w_q, w_k, w_v, w_qi, w_ki, w_wi = w_segs
    tn = 1024
    (up,) = proj(h, w_up, (F32,), tm=tm, tn=tn, name="proj_up")
    (q,) = proj(h, w_q, (BF16,), tm=tm, tn=tn, cos=cos, sin=sin, name="proj_q")
    k32, k16 = proj(h, w_k, (F32, BF16), tm=tm, tn=tn, cos=cos, sin=sin, name="proj_k")
    v32, v16 = proj(h, w_v, (F32, BF16), tm=tm, tn=tn, name="proj_v")
    (qi,) = proj(h, w_qi, (BF16,), tm=tm, tn=tn, cos=cos, sin=sin, name="proj_qi")
    ki32, ki16 = proj(h, w_ki, (F32, BF16), tm=tm, tn=IDX_DIM, cos=cos, sin=sin, name="proj_ki")
    (wi,) = proj(h, w_wi, (F32,), tm=tm, tn=LANES, scale=IDX_HEADS ** -0.5, name="proj_wi")
    return up, q, k32, k16, v32, v16, qi, ki32, ki16, wi


def kernel(x_prompt, x_sample, cache_k, cache_v, cache_idx_k, state_pool, page_table, g_mix, w_in, w_pool,
           s_pool, w_out, g_ffn, w_gate, w_up, w_down, g_final):
    B, S, D = x_prompt.shape
    DB, T, _ = x_sample.shape
    depth = w_in.shape[0]
    assert depth == 1
    n_pages = page_table.shape[1]
    page = cache_k.shape[2]
    n_heads = cache_k.shape[3]
    aw = n_heads * HEAD_DIM
    pw = w_pool.shape[1] * w_pool.shape[2]
    past = n_pages * page
    l = 0

    splits = (pw, aw, aw, aw, IDX_HEADS * IDX_DIM, IDX_DIM, IDX_HEADS)
    offs = [0]
    for s_ in splits:
        offs.append(offs[-1] + s_)
    w_in16 = w_in[l].astype(BF16)
    w_segs = [w_in16[:, offs[j]:offs[j + 1]] for j in range(len(splits))]
    w_segs[6] = jnp.pad(w_segs[6], ((0, 0), (0, LANES - IDX_HEADS)))
    w_pool16 = w_pool[l].astype(BF16)
    w_out16 = w_out[l].astype(BF16)
    w_top, w_bot = w_out16[:pw], w_out16[pw:]
    wg16, wu16, wd16 = w_gate[l].astype(BF16), w_up[l].astype(BF16), w_down[l].astype(BF16)

    xp2 = x_prompt.reshape(B * S, D)
    pos_p = jnp.tile(jnp.arange(S), B)
    up, q, k32, k16, v32, v16, qi, ki32, ki16, wi = _mix_inputs(xp2, pos_p, g_mix[l], w_segs, tm=1024)
    up3 = up.reshape(B, S, pw)
    yp = pool_mixer(up3, jnp.zeros((B, HALO, pw), F32), w_pool16, s_pool[l], tm=512, pos0=0)
    ya = attn_prompt(q.reshape(B, S, aw), qi.reshape(B, S, -1), wi.reshape(B, S, LANES),
                     k16.reshape(B, S, aw), v16.reshape(B, S, aw), ki16.reshape(B, S, IDX_DIM),
                     topk=min(TOPK_MAX, S // 4))
    x2, h2 = outproj(xp2, yp.reshape(B * S, pw), ya.reshape(B * S, aw), w_top, w_bot, g_ffn[l], tm=512)
    y_prompt = ffn(h2, wg16, wu16, wd16, x2, g_final, tm=512, tf=512).reshape(B, S, D)

    k_prompt = k32.reshape(1, B, S, n_heads, HEAD_DIM)
    v_prompt = v32.reshape(1, B, S, n_heads, HEAD_DIM)
    idx_k_prompt = ki32.reshape(1, B, S, IDX_DIM)
    pool_prompt = up3[:, S - POOL_STATE:, :][None]

    M = DB * T
    xs2 = x_sample.reshape(M, D)
    pos_s = jnp.tile(past + jnp.arange(T), DB)
    ups, qs, ks32, ks16, vs32, vs16, qis, kis32, kis16, wis = _mix_inputs(xs2, pos_s, g_mix[l], w_segs, tm=M)
    ups3 = ups.reshape(DB, T, pw)
    prefix16 = jnp.pad(state_pool[l], ((0, 0), (HALO - POOL_STATE, 0), (0, 0)))
    yps = pool_mixer(ups3, prefix16, w_pool16, s_pool[l], tm=T, pos0=past)

    qi_ht = qis.reshape(DB, T, IDX_HEADS, IDX_DIM).transpose(0, 2, 1, 3).reshape(DB, IDX_HEADS * T, IDX_DIM)
    w_ht = wis.reshape(DB, T, LANES)[:, :, :IDX_HEADS].transpose(0, 2, 1).reshape(DB, IDX_HEADS * T, 1)
    w_ht = jnp.broadcast_to(w_ht * (IDX_DIM ** -0.5), (DB, IDX_HEADS * T, LANES))
    pad_rows = lambda a: jnp.pad(a.reshape(DB, T, -1), ((0, 0), (0, page - T), (0, 0)))
    sc, thr = idx_sample(page_table, qi_ht, w_ht, cache_idx_k[l][None], pad_rows(kis16), t_new=T,
                         topk=min(TOPK_MAX, (past + T) // 4))

    q4 = qs.reshape(DB, T, n_heads, HEAD_DIM).transpose(0, 2, 1, 3)
    eye = jnp.eye(n_heads, dtype=BF16)
    q_bd = (q4[:, :, :, None, :] * eye[None, :, None, :, None]).reshape(DB, n_heads * T, aw)
    o_bd = attn_sample(page_table, q_bd, cache_k[l].reshape(1, -1, page, aw), cache_v[l].reshape(1, -1, page, aw),
                       pad_rows(ks16), pad_rows(vs16), sc, thr, t_new=T)
    o5 = o_bd.reshape(DB, n_heads, T, n_heads, HEAD_DIM)
    yas = jnp.stack([o5[:, h, :, h, :] for h in range(n_heads)], axis=2).reshape(M, aw).astype(BF16)

    x2s, h2s = outproj(xs2, yps.reshape(M, pw), yas, w_top, w_bot, g_ffn[l], tm=M)
    y_sample = ffn(h2s, wg16, wu16, wd16, x2s, g_final, tm=M, tf=512).reshape(DB, T, D)

    k_sample = ks32.reshape(1, DB, T, n_heads, HEAD_DIM)
    v_sample = vs32.reshape(1, DB, T, n_heads, HEAD_DIM)
    idx_k_sample = kis32.reshape(1, DB, T, IDX_DIM)
    pool_sample = jnp.concatenate([state_pool[l][:, T:, :], ups3], axis=1)[None]

    return (y_prompt, y_sample, k_prompt, v_prompt, idx_k_prompt, pool_prompt,
            k_sample, v_sample, idx_k_sample, pool_sample)
```

```python
import functools

import jax
import jax.numpy as jnp
from jax import lax
from jax.experimental import pallas as pl
from jax.experimental.pallas import tpu as pltpu

F32 = jnp.float32
BF16 = jnp.bfloat16

LANES = 128
HEAD_DIM = 128
IDX_DIM = 128
IDX_HEADS = 16
POOL_GROUPS = 4
POOL_WINDOWS = (2, 4, 8, 16)
POOL_STATE = max(POOL_WINDOWS) - 1
HALO = 16
TOPK_MAX = 256
Q_BLOCK = 128
ROPE_THETA = 10000.0
EPS = 1e-6
NEG = -1e30
MAX_BISECT = 200
VMEM_LIMIT = 56 * 1024 * 1024

NT_DIMS = (((1,), (1,)), ((), ()))


def _cparams(n_grid):
    return pltpu.CompilerParams(dimension_semantics=("arbitrary",) * n_grid, vmem_limit_bytes=VMEM_LIMIT)


def _rms(x, g):
    return x * lax.rsqrt(jnp.mean(x * x, axis=-1, keepdims=True) + EPS) * g


def _rmsnorm_kernel(x_ref, g_ref, o_ref):
    o_ref[...] = _rms(x_ref[...], g_ref[...]).astype(o_ref.dtype)


def rmsnorm_bf16(x, g, tm):
    m, d = x.shape
    return pl.pallas_call(
        _rmsnorm_kernel,
        grid=(m // tm,),
        in_specs=[pl.BlockSpec((tm, d), lambda i: (i, 0)), pl.BlockSpec((1, d), lambda i: (0, 0))],
        out_specs=pl.BlockSpec((tm, d), lambda i: (i, 0)),
        out_shape=jax.ShapeDtypeStruct((m, d), BF16),
        compiler_params=_cparams(1),
        name="rmsnorm",
    )(x, g.reshape(1, d))


def _proj_kernel(*refs, rope, scale, n_out):
    if rope:
        h_ref, w_ref, cos_ref, sin_ref = refs[:4]
        out_refs = refs[4:]
    else:
        h_ref, w_ref = refs[:2]
        out_refs = refs[2:]
    assert len(out_refs) == n_out
    y = jnp.dot(h_ref[...], w_ref[...], preferred_element_type=F32)
    if scale is not None:
        y = y * scale
    tn = y.shape[1]
    if rope:
        cos = cos_ref[...]
        sin = sin_ref[...]
        for g in range(tn // HEAD_DIM):
            sl = slice(g * HEAD_DIM, (g + 1) * HEAD_DIM)
            yg = y[:, sl]
            r = yg * cos + pltpu.roll(yg, HEAD_DIM // 2, axis=1) * sin
            for o_ref in out_refs:
                o_ref[:, sl] = r.astype(o_ref.dtype)
    else:
        for o_ref in out_refs:
            o_ref[...] = y.astype(o_ref.dtype)


def proj(h, w, out_dtypes, *, tm, tn, cos=None, sin=None, scale=None, name="proj"):
    m, k = h.shape
    n = w.shape[1]
    rope = cos is not None
    in_specs = [pl.BlockSpec((tm, k), lambda j, i: (i, 0)), pl.BlockSpec((k, tn), lambda j, i: (0, j))]
    args = [h, w]
    if rope:
        in_specs += [pl.BlockSpec((tm, HEAD_DIM), lambda j, i: (i, 0))] * 2
        args += [cos, sin]
    outs = pl.pallas_call(
        functools.partial(_proj_kernel, rope=rope, scale=scale, n_out=len(out_dtypes)),
        grid=(n // tn, m // tm),
        in_specs=in_specs,
        out_specs=[pl.BlockSpec((tm, tn), lambda j, i: (i, j)) for _ in out_dtypes],
        out_shape=[jax.ShapeDtypeStruct((m, n), dt) for dt in out_dtypes],
        compiler_params=_cparams(2),
        name=name,
    )(*args)
    return outs


def _pool_kernel(*refs, tm, pos0, has_prev):
    if has_prev:
        up_ref, prev_ref, prefix_ref, w_ref, s_ref, o_ref, ext_ref = refs
    else:
        up_ref, prefix_ref, w_ref, s_ref, o_ref, ext_ref = refs
    i = pl.program_id(1)
    if has_prev:
        halo = jnp.where(i == 0, prefix_ref[...], prev_ref[...])
    else:
        halo = prefix_ref[...]
    ext_ref[0:HALO, :] = halo
    ext_ref[HALO:HALO + tm, :] = up_ref[...]
    pos = pos0 + i * tm + lax.broadcasted_iota(jnp.int32, (tm, 1), 0)
    gw = up_ref.shape[1] // POOL_GROUPS
    for g, w in enumerate(POOL_WINDOWS):
        sl = slice(g * gw, (g + 1) * gw)
        cur = ext_ref[HALO:HALO + tm, sl]
        s = cur
        for j in range(1, w):
            s = s + ext_ref[HALO - j:HALO - j + tm, sl]
        cnt = jnp.minimum(w, pos + 1).astype(F32)
        d = s / cnt - cur
        y = jnp.dot(d.astype(BF16), w_ref[g], preferred_element_type=F32) * s_ref[:, sl]
        o_ref[:, sl] = y.astype(o_ref.dtype)


def pool_mixer(up, prefix16, w_pool, s_pool, *, tm, pos0):
    b, t, wd = up.shape
    has_prev = t > tm
    gw = wd // POOL_GROUPS
    in_specs = [pl.BlockSpec((None, tm, wd), lambda bi, i: (bi, i, 0))]
    args = [up]
    if has_prev:
        r = tm // HALO
        in_specs.append(pl.BlockSpec((None, HALO, wd), lambda bi, i: (bi, jnp.maximum(i * r - 1, 0), 0)))
        args.append(up)
    in_specs += [
        pl.BlockSpec((None, HALO, wd), lambda bi, i: (bi, 0, 0)),
        pl.BlockSpec((POOL_GROUPS, gw, gw), lambda bi, i: (0, 0, 0)),
        pl.BlockSpec((1, wd), lambda bi, i: (0, 0)),
    ]
    args += [prefix16, w_pool, s_pool.reshape(1, wd)]
    return pl.pallas_call(
        functools.partial(_pool_kernel, tm=tm, pos0=pos0, has_prev=has_prev),
        grid=(b, t // tm),
        in_specs=in_specs,
        out_specs=pl.BlockSpec((None, tm, wd), lambda bi, i: (bi, i, 0)),
        out_shape=jax.ShapeDtypeStruct((b, t, wd), BF16),
        scratch_shapes=[pltpu.VMEM((HALO + tm, wd), F32)],
        compiler_params=_cparams(2),
        name="pool_mixer",
    )(*args)


def _bisect_threshold(count_ge, rmin, rmax, n_valid, kq):
    c_hi = count_ge(rmax)
    top_tied = c_hi >= kq
    lo0 = jnp.where(top_tied, rmax, rmin)
    cl0 = jnp.where(top_tied, c_hi, n_valid)

    def not_done(cl, stuck):
        return jnp.max(jnp.where((cl == kq) | stuck, 0.0, 1.0))

    def cond(st):
        it, flag = st[0], st[1]
        return jnp.logical_and(it < MAX_BISECT, flag > 0.0)

    def body(st):
        it, _, lo, hi, cl = st
        mid = lo + 0.5 * (hi - lo)
        stuck = (mid <= lo) | (mid >= hi)
        c = count_ge(mid)
        ge = c >= kq
        lo2 = jnp.where(ge, mid, lo)
        cl2 = jnp.where(ge, c, cl)
        hi2 = jnp.where(ge, hi, mid)
        return it + 1, not_done(cl2, stuck), lo2, hi2, cl2

    st = (jnp.int32(0), not_done(cl0, lo0 >= rmax), lo0, rmax, cl0)
    st = lax.while_loop(cond, body, st)
    return st[2]


def _attn_prompt_kernel(q_ref, qi_ref, wi_ref, k_ref, v_ref, ki_ref, o_ref,
                        sc_ref, wb_ref, m_ref, l_ref, acc_ref, *, tk, topk, n_heads):
    i = pl.program_id(1)
    tq = q_ref.shape[0]
    sub = tk // LANES
    nc = (i * tq) // tk + 1
    qpos = i * tq + lax.broadcasted_iota(jnp.int32, (tq, 1), 0)
    lane = lax.broadcasted_iota(jnp.int32, (tq, LANES), 1)
    idx_scale = IDX_DIM ** -0.5
    att_scale = HEAD_DIM ** -0.5

    wi = wi_ref[...] * idx_scale
    for h in range(IDX_HEADS):
        wb_ref[h] = jnp.broadcast_to(wi[:, h:h + 1], (tq, LANES))

    def idx_chunk(c, carry):
        off = pl.multiple_of(c * tk, tk)
        kic = ki_ref[pl.ds(off, tk), :]
        accs = [jnp.zeros((tq, LANES), F32) for _ in range(sub)]
        for h in range(IDX_HEADS):
            s = lax.dot_general(qi_ref[:, h * IDX_DIM:(h + 1) * IDX_DIM], kic, NT_DIMS,
                                preferred_element_type=F32)
            wbh = wb_ref[h]
            for j in range(sub):
                accs[j] = accs[j] + jnp.maximum(s[:, j * LANES:(j + 1) * LANES], 0.0) * wbh
        for j in range(sub):
            kpos = off + j * LANES + lane
            sc_ref[c * sub + j] = jnp.where(kpos <= qpos, accs[j], -jnp.inf)
        return carry

    lax.fori_loop(0, nc, idx_chunk, 0)

    def reduce_rows(fn, init, red):
        def body(c, a):
            for j in range(sub):
                a = fn(a, sc_ref[c * sub + j])
            return a
        a = lax.fori_loop(0, nc, body, jnp.full((tq, LANES), init, F32))
        return red(a, axis=1, keepdims=True)

    def count_ge(t):
        tb = jnp.broadcast_to(t, (tq, LANES))
        return reduce_rows(lambda a, blk: a + jnp.where(blk >= tb, 1.0, 0.0), 0.0, jnp.sum)

    rmax = reduce_rows(jnp.maximum, -jnp.inf, jnp.max)
    rmin = reduce_rows(lambda a, blk: jnp.minimum(a, jnp.where(blk == -jnp.inf, jnp.inf, blk)), jnp.inf, jnp.min)
    n_valid = (qpos + 1).astype(F32)
    kq = jnp.minimum(float(topk), n_valid)
    thr = _bisect_threshold(count_ge, rmin, rmax, n_valid, kq)
    thr_b = jnp.broadcast_to(thr, (tq, LANES))

    m_ref[...] = jnp.full(m_ref.shape, -jnp.inf, F32)
    l_ref[...] = jnp.zeros(l_ref.shape, F32)
    acc_ref[...] = jnp.zeros(acc_ref.shape, F32)

    def att_chunk(c, carry):
        off = pl.multiple_of(c * tk, tk)
        masks = [sc_ref[c * sub + j] >= thr_b for j in range(sub)]
        for h in range(n_heads):
            hs = slice(h * HEAD_DIM, (h + 1) * HEAD_DIM)
            s = lax.dot_general(q_ref[:, hs], k_ref[pl.ds(off, tk), hs], NT_DIMS,
                                preferred_element_type=F32) * att_scale
            ss = [jnp.where(masks[j], s[:, j * LANES:(j + 1) * LANES], NEG) for j in range(sub)]
            m_prev = m_ref[h]
            m_cur = ss[0]
            for j in range(1, sub):
                m_cur = jnp.maximum(m_cur, ss[j])
            m_new = jnp.maximum(m_prev, jnp.max(m_cur, axis=1, keepdims=True))
            alpha = jnp.exp(m_prev - m_new)
            ps = [jnp.exp(ss[j] - m_new) for j in range(sub)]
            p_sum = ps[0]
            for j in range(1, sub):
                p_sum = p_sum + ps[j]
            l_ref[h] = alpha * l_ref[h] + jnp.sum(p_sum, axis=1, keepdims=True)
            m_ref[h] = m_new
            p = jnp.concatenate(ps, axis=1).astype(BF16)
            pv = jnp.dot(p, v_ref[pl.ds(off, tk), hs], preferred_element_type=F32)
            acc_ref[:, hs] = alpha * acc_ref[:, hs] + pv
        return carry

    lax.fori_loop(0, nc, att_chunk, 0)

    for h in range(n_heads):
        hs = slice(h * HEAD_DIM, (h + 1) * HEAD_DIM)
        o_ref[:, hs] = (acc_ref[:, hs] / l_ref[h]).astype(o_ref.dtype)


def attn_prompt(q, qi, wi, k, v, ki, *, topk, tk=256):
    b, s, aw = q.shape
    n_heads = aw // HEAD_DIM
    tq = Q_BLOCK
    kern = functools.partial(_attn_prompt_kernel, tk=tk, topk=topk, n_heads=n_heads)
    return pl.pallas_call(
        kern,
        grid=(b, s // tq),
        in_specs=[
            pl.BlockSpec((None, tq, aw), lambda bi, i: (bi, i, 0)),
            pl.BlockSpec((None, tq, qi.shape[2]), lambda bi, i: (bi, i, 0)),
            pl.BlockSpec((None, tq, LANES), lambda bi, i: (bi, i, 0)),
            pl.BlockSpec((None, s, aw), lambda bi, i: (bi, 0, 0)),
            pl.BlockSpec((None, s, aw), lambda bi, i: (bi, 0, 0)),
            pl.BlockSpec((None, s, IDX_DIM), lambda bi, i: (bi, 0, 0)),
        ],
        out_specs=pl.BlockSpec((None, tq, aw), lambda bi, i: (bi, i, 0)),
        out_shape=jax.ShapeDtypeStruct((b, s, aw), BF16),
        scratch_shapes=[
            pltpu.VMEM((s // LANES, tq, LANES), F32),
            pltpu.VMEM((IDX_HEADS, tq, LANES), F32),
            pltpu.VMEM((n_heads, tq, LANES), F32),
            pltpu.VMEM((n_heads, tq, LANES), F32),
            pltpu.VMEM((tq, aw), F32),
        ],
        compiler_params=_cparams(2),
        name="attn_prompt",
    )(q, qi, wi, k, v, ki)


def _idx_sample_kernel(pt_ref, qi_ref, w_ref, *rest, n_pages, group, t_new, topk):
    cik_refs = rest[:group]
    kin_ref, sc_ref, thr_ref = rest[group:]
    p = pl.program_id(1)
    rows = qi_ref.shape[0]

    def scores(keys_bf16):
        s = lax.dot_general(qi_ref[...], keys_bf16, NT_DIMS, preferred_element_type=F32)
        r = jnp.maximum(s, 0.0) * w_ref[...]
        return jnp.sum(r.reshape(rows // t_new, t_new, LANES), axis=0)

    for g in range(group):
        sc_ref[p * group + g] = scores(cik_refs[g][...].astype(BF16))

    @pl.when(p == pl.num_programs(1) - 1)
    def _():
        s = scores(kin_ref[...])
        tok = lax.broadcasted_iota(jnp.int32, (t_new, LANES), 0)
        lane = lax.broadcasted_iota(jnp.int32, (t_new, LANES), 1)
        sc_ref[n_pages] = jnp.where((lane <= tok) & (lane < t_new), s, -jnp.inf)

        def reduce_rows(fn, init, red):
            a = lax.fori_loop(0, n_pages + 1, lambda c, a: fn(a, sc_ref[c]), jnp.full((t_new, LANES), init, F32))
            return red(a, axis=1, keepdims=True)

        def count_ge(t):
            tb = jnp.broadcast_to(t, (t_new, LANES))
            return reduce_rows(lambda a, blk: a + jnp.where(blk >= tb, 1.0, 0.0), 0.0, jnp.sum)

        rmax = reduce_rows(jnp.maximum, -jnp.inf, jnp.max)
        rmin = reduce_rows(lambda a, blk: jnp.minimum(a, jnp.where(blk == -jnp.inf, jnp.inf, blk)),
                           jnp.inf, jnp.min)
        tokc = lax.broadcasted_iota(jnp.int32, (t_new, 1), 0)
        n_valid = (n_pages * LANES + tokc + 1).astype(F32)
        kq = jnp.minimum(float(topk), n_valid)
        thr = _bisect_threshold(count_ge, rmin, rmax, n_valid, kq)
        thr_ref[...] = jnp.broadcast_to(thr, (t_new, LANES))


def _page_map(b, p, pt, *, g, group):
    return (0, pt[b, p * group + g], 0, 0)


def idx_sample(page_table, qi_ht, w_ht, cache_idx_k, ki_new_pad, *, t_new, topk, group=16):
    db, n_pages = page_table.shape
    rows = qi_ht.shape[1]
    page = cache_idx_k.shape[2]
    assert n_pages % group == 0
    kern = functools.partial(_idx_sample_kernel, n_pages=n_pages, group=group, t_new=t_new, topk=topk)
    page_specs = [pl.BlockSpec((None, None, page, IDX_DIM), functools.partial(_page_map, g=g, group=group))
                  for g in range(group)]
    grid_spec = pltpu.PrefetchScalarGridSpec(
        num_scalar_prefetch=1,
        grid=(db, n_pages // group),
        in_specs=[
            pl.BlockSpec((None, rows, IDX_DIM), lambda b, p, pt: (b, 0, 0)),
            pl.BlockSpec((None, rows, LANES), lambda b, p, pt: (b, 0, 0)),
            *page_specs,
            pl.BlockSpec((None, page, IDX_DIM), lambda b, p, pt: (b, 0, 0)),
        ],
        out_specs=[
            pl.BlockSpec((None, n_pages + 1, t_new, LANES), lambda b, p, pt: (b, 0, 0, 0)),
            pl.BlockSpec((None, t_new, LANES), lambda b, p, pt: (b, 0, 0)),
        ],
    )
    return pl.pallas_call(
        kern,
        grid_spec=grid_spec,
        out_shape=[jax.ShapeDtypeStruct((db, n_pages + 1, t_new, LANES), F32),
                   jax.ShapeDtypeStruct((db, t_new, LANES), F32)],
        compiler_params=_cparams(2),
        name="idx_sample",
    )(page_table, qi_ht, w_ht, *([cache_idx_k] * group), ki_new_pad)


def _attn_sample_kernel(pt_ref, q_ref, *rest, group, page, n_heads):
    ck_refs = rest[:group]
    cv_refs = rest[group:2 * group]
    kn_ref, vn_ref, sc_ref, scn_ref, thr_ref, o_ref, m_ref, l_ref, acc_ref = rest[2 * group:]
    p = pl.program_id(1)
    att_scale = HEAD_DIM ** -0.5

    @pl.when(p == 0)
    def _():
        m_ref[...] = jnp.full(m_ref.shape, -jnp.inf, F32)
        l_ref[...] = jnp.zeros(l_ref.shape, F32)
        acc_ref[...] = jnp.zeros(acc_ref.shape, F32)

    thr = thr_ref[...]

    def head_rows(ref, h):
        return ref[pl.ds(h, page, stride=n_heads), :].astype(BF16)

    def attend(h, k_refs, v_refs, masks):
        qh = q_ref[h]
        kh = jnp.concatenate([head_rows(r, h) for r in k_refs], axis=0)
        s = lax.dot_general(qh, kh, NT_DIMS, preferred_element_type=F32) * att_scale
        s = jnp.where(masks, s, NEG)
        m_prev = m_ref[h]
        m_new = jnp.maximum(m_prev, jnp.max(s, axis=1, keepdims=True))
        alpha = jnp.exp(m_prev - m_new)
        pr = jnp.exp(s - jnp.concatenate([m_new] * len(k_refs), axis=1))
        l_ref[h] = alpha * l_ref[h] + jnp.sum(pr, axis=1, keepdims=True)
        m_ref[h] = m_new
        vh = jnp.concatenate([head_rows(r, h) for r in v_refs], axis=0)
        acc_ref[h] = alpha * acc_ref[h] + jnp.dot(pr.astype(BF16), vh, preferred_element_type=F32)

    page_masks = (jnp.concatenate([sc_ref[g] for g in range(group)], axis=1)
                  >= jnp.concatenate([thr] * group, axis=1))
    for h in range(n_heads):
        attend(h, ck_refs, cv_refs, page_masks)

    @pl.when(p == pl.num_programs(1) - 1)
    def _():
        new_mask = scn_ref[...] >= thr
        for h in range(n_heads):
            attend(h, [kn_ref], [vn_ref], new_mask)
            o_ref[h] = acc_ref[h] / l_ref[h]


def attn_sample(page_table, q_ht, cache_k, cache_v, k_new_pad, v_new_pad, sc, thr, *, group=8):
    db, n_pages = page_table.shape
    n_heads, t_new = q_ht.shape[1], q_ht.shape[2]
    prow = cache_k.shape[2]
    page = prow // n_heads
    assert n_pages % group == 0
    kern = functools.partial(_attn_sample_kernel, group=group, page=page, n_heads=n_heads)
    cache_specs = [pl.BlockSpec((None, None, prow, HEAD_DIM), functools.partial(_page_map, g=g, group=group))
                   for g in range(group)]
    new_spec = pl.BlockSpec((None, prow, HEAD_DIM), lambda b, p, pt: (b, 0, 0))
    head_spec = pl.BlockSpec((None, n_heads, t_new, HEAD_DIM), lambda b, p, pt: (b, 0, 0, 0))
    grid_spec = pltpu.PrefetchScalarGridSpec(
        num_scalar_prefetch=1,
        grid=(db, n_pages // group),
        in_specs=[
            head_spec,
            *cache_specs, *cache_specs, new_spec, new_spec,
            pl.BlockSpec((None, group, t_new, LANES), lambda b, p, pt: (b, p, 0, 0)),
            pl.BlockSpec((None, None, t_new, LANES), lambda b, p, pt: (b, n_pages, 0, 0)),
            pl.BlockSpec((None, t_new, LANES), lambda b, p, pt: (b, 0, 0)),
        ],
        out_specs=head_spec,
        scratch_shapes=[
            pltpu.VMEM((n_heads, t_new, LANES), F32),
            pltpu.VMEM((n_heads, t_new, LANES), F32),
            pltpu.VMEM((n_heads, t_new, HEAD_DIM), F32),
        ],
    )
    return pl.pallas_call(
        kern,
        grid_spec=grid_spec,
        out_shape=jax.ShapeDtypeStruct((db, n_heads, t_new, HEAD_DIM), F32),
        compiler_params=_cparams(2),
        name="attn_sample",
    )(page_table, q_ht, *([cache_k] * group), *([cache_v] * group), k_new_pad, v_new_pad, sc, sc, thr)


def _outproj_kernel(x_ref, yp_ref, ya_ref, wt_ref, wb_ref, g_ref, x2_ref, h2_ref):
    y = jnp.dot(yp_ref[...], wt_ref[...], preferred_element_type=F32)
    y = y + jnp.dot(ya_ref[...], wb_ref[...], preferred_element_type=F32)
    x2 = x_ref[...] + y
    x2_ref[...] = x2
    h2_ref[...] = _rms(x2, g_ref[...]).astype(h2_ref.dtype)


def outproj(x, yp, ya, w_top, w_bot, g, *, tm):
    m, d = x.shape
    kp, ka = yp.shape[1], ya.shape[1]
    return pl.pallas_call(
        _outproj_kernel,
        grid=(m // tm,),
        in_specs=[
            pl.BlockSpec((tm, d), lambda i: (i, 0)),
            pl.BlockSpec((tm, kp), lambda i: (i, 0)),
            pl.BlockSpec((tm, ka), lambda i: (i, 0)),
            pl.BlockSpec((kp, d), lambda i: (0, 0)),
            pl.BlockSpec((ka, d), lambda i: (0, 0)),
            pl.BlockSpec((1, d), lambda i: (0, 0)),
        ],
        out_specs=[pl.BlockSpec((tm, d), lambda i: (i, 0)), pl.BlockSpec((tm, d), lambda i: (i, 0))],
        out_shape=[jax.ShapeDtypeStruct((m, d), F32), jax.ShapeDtypeStruct((m, d), BF16)],
        compiler_params=_cparams(1),
        name="outproj",
    )(x, yp, ya, w_top, w_bot, g.reshape(1, d))


def _ffn_kernel(h2_ref, wg_ref, wu_ref, wd_ref, x2_ref, g_ref, o_ref, acc_ref):
    f = pl.program_id(1)

    @pl.when(f == 0)
    def _():
        acc_ref[...] = jnp.zeros(acc_ref.shape, F32)

    h2 = h2_ref[...]
    gate = jnp.dot(h2, wg_ref[...], preferred_element_type=F32)
    up = jnp.dot(h2, wu_ref[...], preferred_element_type=F32)
    a = (gate * jax.nn.sigmoid(gate) * up).astype(BF16)
    acc_ref[...] += jnp.dot(a, wd_ref[...], preferred_element_type=F32)

    @pl.when(f == pl.num_programs(1) - 1)
    def _():
        o_ref[...] = _rms(x2_ref[...] + acc_ref[...], g_ref[...])


def ffn(h2, wg, wu, wd, x2, g, *, tm, tf):
    m, d = h2.shape
    dff = wg.shape[1]
    return pl.pallas_call(
        _ffn_kernel,
        grid=(m // tm, dff // tf),
        in_specs=[
            pl.BlockSpec((tm, d), lambda i, f: (i, 0)),
            pl.BlockSpec((d, tf), lambda i, f: (0, f)),
            pl.BlockSpec((d, tf), lambda i, f: (0, f)),
            pl.BlockSpec((tf, d), lambda i, f: (f, 0)),
            pl.BlockSpec((tm, d), lambda i, f: (i, 0)),
            pl.BlockSpec((1, d), lambda i, f: (0, 0)),
        ],
        out_specs=pl.BlockSpec((tm, d), lambda i, f: (i, 0)),
        out_shape=jax.ShapeDtypeStruct((m, d), F32),
        scratch_shapes=[pltpu.VMEM((tm, d), F32)],
        compiler_params=_cparams(2),
        name="ffn",
    )(h2, wg, wu, wd, x2, g.reshape(1, d))


def _rope_tables(pos):
    half = HEAD_DIM // 2
    inv = ROPE_THETA ** (-jnp.arange(half, dtype=F32) / half)
    ang = pos.astype(F32)[:, None] * inv[None, :]
    cos, sin = jnp.cos(ang), jnp.sin(ang)
    return jnp.concatenate([cos, cos], axis=-1), jnp.concatenate([-sin, sin], axis=-1)


def _mix_inputs(x2d, pos_rows, g_mix, w_segs, *, tm):
    cos, sin = _rope_tables(pos_rows)
    h = rmsnorm_bf16(x2d, g_mix, tm)
    w_up, w_q, w_k, w_v, w_qi, w_ki, w_wi = w_segs
    tn = 1024
    (up,) = proj(h, w_up, (F32,), tm=tm, tn=tn, name="proj_up")
    (q,) = proj(h, w_q, (BF16,), tm=tm, tn=tn, cos=cos, sin=sin, name="proj_q")
    k32, k16 = proj(h, w_k, (F32, BF16), tm=tm, tn=tn, cos=cos, sin=sin, name="proj_k")
    v32, v16 = proj(h, w_v, (F32, BF16), tm=tm, tn=tn, name="proj_v")
    (qi,) = proj(h, w_qi, (BF16,), tm=tm, tn=tn, cos=cos, sin=sin, name="proj_qi")
    ki32, ki16 = proj(h, w_ki, (F32, BF16), tm=tm, tn=IDX_DIM, cos=cos, sin=sin, name="proj_ki")
    (wi,) = proj(h, w_wi, (F32,), tm=tm, tn=LANES, scale=IDX_HEADS ** -0.5, name="proj_wi")
    return up, q, k32, k16, v32, v16, qi, ki32, ki16, wi


def kernel(x_prompt, x_sample, cache_k, cache_v, cache_idx_k, state_pool, page_table, g_mix, w_in, w_pool,
           s_pool, w_out, g_ffn, w_gate, w_up, w_down, g_final):
    B, S, D = x_prompt.shape
    DB, T, _ = x_sample.shape
    depth = w_in.shape[0]
    assert depth == 1
    n_pages = page_table.shape[1]
    page = cache_k.shape[2]
    n_heads = cache_k.shape[3]
    aw = n_heads * HEAD_DIM
    pw = w_pool.shape[1] * w_pool.shape[2]
    past = n_pages * page
    l = 0

    splits = (pw, aw, aw, aw, IDX_HEADS * IDX_DIM, IDX_DIM, IDX_HEADS)
    offs = [0]
    for s_ in splits:
        offs.append(offs[-1] + s_)
    w_in16 = w_in[l].astype(BF16)
    w_segs = [w_in16[:, offs[j]:offs[j + 1]] for j in range(len(splits))]
    w_segs[6] = jnp.pad(w_segs[6], ((0, 0), (0, LANES - IDX_HEADS)))
    w_pool16 = w_pool[l].astype(BF16)
    w_out16 = w_out[l].astype(BF16)
    w_top, w_bot = w_out16[:pw], w_out16[pw:]
    wg16, wu16, wd16 = w_gate[l].astype(BF16), w_up[l].astype(BF16), w_down[l].astype(BF16)

    xp2 = x_prompt.reshape(B * S, D)
    pos_p = jnp.tile(jnp.arange(S), B)
    up, q, k32, k16, v32, v16, qi, ki32, ki16, wi = _mix_inputs(xp2, pos_p, g_mix[l], w_segs, tm=1024)
    up3 = up.reshape(B, S, pw)
    yp = pool_mixer(up3, jnp.zeros((B, HALO, pw), F32), w_pool16, s_pool[l], tm=512, pos0=0)
    ya = attn_prompt(q.reshape(B, S, aw), qi.reshape(B, S, -1), wi.reshape(B, S, LANES),
                     k16.reshape(B, S, aw), v16.reshape(B, S, aw), ki16.reshape(B, S, IDX_DIM),
                     topk=min(TOPK_MAX, S // 4))
    x2, h2 = outproj(xp2, yp.reshape(B * S, pw), ya.reshape(B * S, aw), w_top, w_bot, g_ffn[l], tm=512)
    y_prompt = ffn(h2, wg16, wu16, wd16, x2, g_final, tm=512, tf=512).reshape(B, S, D)

    k_prompt = k32.reshape(1, B, S, n_heads, HEAD_DIM)
    v_prompt = v32.reshape(1, B, S, n_heads, HEAD_DIM)
    idx_k_prompt = ki32.reshape(1, B, S, IDX_DIM)
    pool_prompt = up3[:, S - POOL_STATE:, :][None]

    M = DB * T
    xs2 = x_sample.reshape(M, D)
    pos_s = jnp.tile(past + jnp.arange(T), DB)
    ups, qs, ks32, ks16, vs32, vs16, qis, kis32, kis16, wis = _mix_inputs(xs2, pos_s, g_mix[l], w_segs, tm=M)
    ups3 = ups.reshape(DB, T, pw)
    prefix16 = jnp.pad(state_pool[l], ((0, 0), (HALO - POOL_STATE, 0), (0, 0)))
    yps = pool_mixer(ups3, prefix16, w_pool16, s_pool[l], tm=T, pos0=past)

    qi_ht = qis.reshape(DB, T, IDX_HEADS, IDX_DIM).transpose(0, 2, 1, 3).reshape(DB, IDX_HEADS * T, IDX_DIM)
    w_ht = wis.reshape(DB, T, LANES)[:, :, :IDX_HEADS].transpose(0, 2, 1).reshape(DB, IDX_HEADS * T, 1)
    w_ht = jnp.broadcast_to(w_ht * (IDX_DIM ** -0.5), (DB, IDX_HEADS * T, LANES))
    pad_rows = lambda a: jnp.pad(a.reshape(DB, T, -1), ((0, 0), (0, page - T), (0, 0)))
    sc, thr = idx_sample(page_table, qi_ht, w_ht, cache_idx_k, pad_rows(kis16), t_new=T,
                         topk=min(TOPK_MAX, (past + T) // 4))

    n_phys = cache_k.shape[1]
    key_head_rows = lambda a: pad_rows(a).reshape(DB, page * n_heads, HEAD_DIM)
    q_ht = qs.reshape(DB, T, n_heads, HEAD_DIM).transpose(0, 2, 1, 3)
    o_ht = attn_sample(page_table, q_ht, cache_k.reshape(depth, n_phys, page * n_heads, HEAD_DIM),
                       cache_v.reshape(depth, n_phys, page * n_heads, HEAD_DIM),
                       key_head_rows(ks32), key_head_rows(vs32), sc, thr)
    yas = o_ht.transpose(0, 2, 1, 3).reshape(M, aw).astype(BF16)

    x2s, h2s = outproj(xs2, yps.reshape(M, pw), yas, w_top, w_bot, g_ffn[l], tm=M)
    y_sample = ffn(h2s, wg16, wu16, wd16, x2s, g_final, tm=M, tf=512).reshape(DB, T, D)

    k_sample = ks32.reshape(1, DB, T, n_heads, HEAD_DIM)
    v_sample = vs32.reshape(1, DB, T, n_heads, HEAD_DIM)
    idx_k_sample = kis32.reshape(1, DB, T, IDX_DIM)
    pool_sample = jnp.concatenate([state_pool[l][:, T:, :], ups3], axis=1)[None]

    return (y_prompt, y_sample, k_prompt, v_prompt, idx_k_prompt, pool_prompt,
            k_sample, v_sample, idx_k_sample, pool_sample)
```

```python
import functools

import jax
import jax.numpy as jnp
from jax import lax
from jax.experimental import pallas as pl
from jax.experimental.pallas import tpu as pltpu

F32 = jnp.float32
BF16 = jnp.bfloat16

LANES = 128
HEAD_DIM = 128
IDX_DIM = 128
IDX_HEADS = 16
POOL_GROUPS = 4
POOL_WINDOWS = (2, 4, 8, 16)
POOL_STATE = max(POOL_WINDOWS) - 1
HALO = 16
TOPK_MAX = 256
ATTN_TQ = 256
ATTN_TK = 128
ROPE_THETA = 10000.0
EPS = 1e-6
NEG = -1e30
MAX_SEARCH_STEPS = 400
LOG2E = 1.4426950408889634
VMEM_LIMIT = 56 * 1024 * 1024

NT_DIMS = (((1,), (1,)), ((), ()))


def _cparams(n_grid):
    return pltpu.CompilerParams(dimension_semantics=("arbitrary",) * n_grid, vmem_limit_bytes=VMEM_LIMIT)


def _rms(x, g):
    return x * lax.rsqrt(jnp.mean(x * x, axis=-1, keepdims=True) + EPS) * g


def _rmsnorm_kernel(x_ref, g_ref, o_ref):
    o_ref[...] = _rms(x_ref[...], g_ref[...]).astype(o_ref.dtype)


def rmsnorm_bf16(x, g, tm):
    m, d = x.shape
    return pl.pallas_call(
        _rmsnorm_kernel,
        grid=(m // tm,),
        in_specs=[pl.BlockSpec((tm, d), lambda i: (i, 0)), pl.BlockSpec((1, d), lambda i: (0, 0))],
        out_specs=pl.BlockSpec((tm, d), lambda i: (i, 0)),
        out_shape=jax.ShapeDtypeStruct((m, d), BF16),
        compiler_params=_cparams(1),
        name="rmsnorm",
    )(x, g.reshape(1, d))


def _proj_kernel(*refs, rope, scale, n_out):
    if rope:
        h_ref, w_ref, cos_ref, sin_ref = refs[:4]
        out_refs = refs[4:]
    else:
        h_ref, w_ref = refs[:2]
        out_refs = refs[2:]
    assert len(out_refs) == n_out
    y = jnp.dot(h_ref[...], w_ref[...], preferred_element_type=F32)
    if scale is not None:
        y = y * scale
    tn = y.shape[1]
    if rope:
        cos = cos_ref[...]
        sin = sin_ref[...]
        for g in range(tn // HEAD_DIM):
            sl = slice(g * HEAD_DIM, (g + 1) * HEAD_DIM)
            yg = y[:, sl]
            r = yg * cos + pltpu.roll(yg, HEAD_DIM // 2, axis=1) * sin
            for o_ref in out_refs:
                o_ref[:, sl] = r.astype(o_ref.dtype)
    else:
        for o_ref in out_refs:
            o_ref[...] = y.astype(o_ref.dtype)


def proj(h, w, out_dtypes, *, tm, tn, cos=None, sin=None, scale=None, name="proj"):
    m, k = h.shape
    n = w.shape[1]
    rope = cos is not None
    in_specs = [pl.BlockSpec((tm, k), lambda j, i: (i, 0)), pl.BlockSpec((k, tn), lambda j, i: (0, j))]
    args = [h, w]
    if rope:
        in_specs += [pl.BlockSpec((tm, HEAD_DIM), lambda j, i: (i, 0))] * 2
        args += [cos, sin]
    outs = pl.pallas_call(
        functools.partial(_proj_kernel, rope=rope, scale=scale, n_out=len(out_dtypes)),
        grid=(n // tn, m // tm),
        in_specs=in_specs,
        out_specs=[pl.BlockSpec((tm, tn), lambda j, i: (i, j)) for _ in out_dtypes],
        out_shape=[jax.ShapeDtypeStruct((m, n), dt) for dt in out_dtypes],
        compiler_params=_cparams(2),
        name=name,
    )(*args)
    return outs


def _pool_kernel(*refs, tm, pos0, has_prev):
    if has_prev:
        up_ref, prev_ref, prefix_ref, w_ref, s_ref, o_ref, ext_ref = refs
    else:
        up_ref, prefix_ref, w_ref, s_ref, o_ref, ext_ref = refs
    i = pl.program_id(1)
    if has_prev:
        halo = jnp.where(i == 0, prefix_ref[...], prev_ref[...])
    else:
        halo = prefix_ref[...]
    ext_ref[0:HALO, :] = halo
    ext_ref[HALO:HALO + tm, :] = up_ref[...]
    pos = pos0 + i * tm + lax.broadcasted_iota(jnp.int32, (tm, 1), 0)
    gw = up_ref.shape[1] // POOL_GROUPS
    for g, w in enumerate(POOL_WINDOWS):
        sl = slice(g * gw, (g + 1) * gw)
        cur = ext_ref[HALO:HALO + tm, sl]
        s = cur
        for j in range(1, w):
            s = s + ext_ref[HALO - j:HALO - j + tm, sl]
        cnt = jnp.minimum(w, pos + 1).astype(F32)
        d = s / cnt - cur
        y = jnp.dot(d.astype(BF16), w_ref[g], preferred_element_type=F32) * s_ref[:, sl]
        o_ref[:, sl] = y.astype(o_ref.dtype)


def pool_mixer(up, prefix16, w_pool, s_pool, *, tm, pos0):
    b, t, wd = up.shape
    has_prev = t > tm
    gw = wd // POOL_GROUPS
    in_specs = [pl.BlockSpec((None, tm, wd), lambda bi, i: (bi, i, 0))]
    args = [up]
    if has_prev:
        r = tm // HALO
        in_specs.append(pl.BlockSpec((None, HALO, wd), lambda bi, i: (bi, jnp.maximum(i * r - 1, 0), 0)))
        args.append(up)
    in_specs += [
        pl.BlockSpec((None, HALO, wd), lambda bi, i: (bi, 0, 0)),
        pl.BlockSpec((POOL_GROUPS, gw, gw), lambda bi, i: (0, 0, 0)),
        pl.BlockSpec((1, wd), lambda bi, i: (0, 0)),
    ]
    args += [prefix16, w_pool, s_pool.reshape(1, wd)]
    return pl.pallas_call(
        functools.partial(_pool_kernel, tm=tm, pos0=pos0, has_prev=has_prev),
        grid=(b, t // tm),
        in_specs=in_specs,
        out_specs=pl.BlockSpec((None, tm, wd), lambda bi, i: (bi, i, 0)),
        out_shape=jax.ShapeDtypeStruct((b, t, wd), BF16),
        scratch_shapes=[pltpu.VMEM((HALO + tm, wd), F32)],
        compiler_params=_cparams(2),
        name="pool_mixer",
    )(*args)


def _select_threshold(count_ge, rmin, rmax, n_valid, kq):
    c_hi = count_ge(rmax)
    top_tied = c_hi >= kq
    lo0 = jnp.where(top_tied, rmax, rmin)
    cl0 = jnp.where(top_tied, c_hi, n_valid)

    def not_done(cl, stuck):
        return jnp.max(jnp.where((cl == kq) | (stuck > 0.0), 0.0, 1.0))

    def cond(st):
        it, flag = st[0], st[1]
        return jnp.logical_and(it < MAX_SEARCH_STEPS, flag > 0.0)

    def body(st):
        it, _, lo, hi, cl, ch, stuck = st
        halve = (it % 2) == 1
        frac = jnp.where(halve, 0.5, (cl - kq - 0.5) / jnp.maximum(cl - ch, 1.0))
        mid = jnp.clip(lo + frac * (hi - lo), lo, hi)
        stuck = jnp.where(halve, jnp.where((mid <= lo) | (mid >= hi), 1.0, 0.0), stuck)
        c = count_ge(mid)
        ge = c >= kq
        lo2 = jnp.where(ge, mid, lo)
        cl2 = jnp.where(ge, c, cl)
        hi2 = jnp.where(ge, hi, mid)
        ch2 = jnp.where(ge, ch, c)
        return it + 1, not_done(cl2, stuck), lo2, hi2, cl2, ch2, stuck

    stuck0 = jnp.where(lo0 >= rmax, 1.0, 0.0)
    st = (jnp.int32(0), not_done(cl0, stuck0), lo0, rmax, cl0, c_hi, stuck0)
    st = lax.while_loop(cond, body, st)
    return st[2]


def _attn_prompt_kernel(q_ref, qi_ref, wit_ref, k_ref, vt_ref, ki_ref, o_ref,
                        sc_ref, m_ref, l_ref, acc_ref, *, tk, topk, n_heads):
    i = pl.program_id(1)
    tq = q_ref.shape[0]
    nc = ((i + 1) * tq + tk - 1) // tk
    qpos = i * tq + lax.broadcasted_iota(jnp.int32, (1, tq), 1)
    key_row = lax.broadcasted_iota(jnp.int32, (tk, tq), 0)
    w_rows = wit_ref[...] * (IDX_DIM ** -0.5)

    def idx_chunk(c, carry):
        off = pl.multiple_of(c * tk, tk)
        kic = ki_ref[pl.ds(off, tk), :]
        acc = jnp.zeros((tk, tq), F32)
        for h in range(IDX_HEADS):
            s = lax.dot_general(kic, qi_ref[:, h * IDX_DIM:(h + 1) * IDX_DIM], NT_DIMS,
                                preferred_element_type=F32)
            acc = acc + jnp.maximum(s, 0.0) * w_rows[h:h + 1, :]
        sc_ref[pl.ds(off, tk), :] = jnp.where(off + key_row <= qpos, acc, -jnp.inf)
        return carry

    lax.fori_loop(0, nc, idx_chunk, 0)

    lanes_par = 4

    def reduce_keys(fn, init, red):
        def body(c, a):
            blk = sc_ref[pl.ds(pl.multiple_of(c * tk, tk), tk), :]
            return fn(a, blk.reshape(lanes_par, tk // (8 * lanes_par), 8, tq))
        a = lax.fori_loop(0, nc, body, jnp.full((lanes_par, 8, tq), init, F32))
        return red(red(a, axis=0), axis=0, keepdims=True)

    def count_ge(t):
        return reduce_keys(lambda a, blk: a + jnp.sum(jnp.where(blk >= t, 1.0, 0.0), axis=1), 0.0, jnp.sum)

    rmax = reduce_keys(lambda a, blk: jnp.maximum(a, jnp.max(blk, axis=1)), -jnp.inf, jnp.max)
    rmin = reduce_keys(lambda a, blk: jnp.minimum(a, jnp.min(jnp.where(blk == -jnp.inf, jnp.inf, blk), axis=1)),
                       jnp.inf, jnp.min)
    n_valid = (qpos + 1).astype(F32)
    kq = jnp.minimum(float(topk), n_valid)
    thr = _select_threshold(count_ge, rmin, rmax, n_valid, kq)

    m_ref[...] = jnp.full(m_ref.shape, -jnp.inf, F32)
    l_ref[...] = jnp.zeros(l_ref.shape, F32)
    acc_ref[...] = jnp.zeros(acc_ref.shape, F32)

    def att_chunk(c, carry):
        off = pl.multiple_of(c * tk, tk)
        mask = sc_ref[pl.ds(off, tk), :] >= thr
        for h in range(n_heads):
            hs = slice(h * HEAD_DIM, (h + 1) * HEAD_DIM)
            s = lax.dot_general(k_ref[pl.ds(off, tk), hs], q_ref[:, hs], NT_DIMS,
                                preferred_element_type=F32)
            s = jnp.where(mask, s, NEG)
            m_prev = m_ref[h]
            m_new = jnp.maximum(m_prev, jnp.max(s, axis=0, keepdims=True))
            alpha = jnp.exp2(m_prev - m_new)
            p = jnp.exp2(s - m_new)
            l_ref[h] = alpha * l_ref[h] + jnp.sum(p, axis=0, keepdims=True)
            m_ref[h] = m_new
            pv = jnp.dot(vt_ref[c, hs, :], p.astype(BF16), preferred_element_type=F32)
            acc_ref[h] = alpha * acc_ref[h] + pv
        return carry

    lax.fori_loop(0, nc, att_chunk, 0)

    for h in range(n_heads):
        hs = slice(h * HEAD_DIM, (h + 1) * HEAD_DIM)
        o_ref[:, hs] = (acc_ref[h] / l_ref[h]).T.astype(o_ref.dtype)


def attn_prompt(q, qi, wit, k, vt, ki, *, topk):
    b, s, aw = q.shape
    n_heads = aw // HEAD_DIM
    tq = ATTN_TQ
    tk = vt.shape[3]
    kern = functools.partial(_attn_prompt_kernel, tk=tk, topk=topk, n_heads=n_heads)
    return pl.pallas_call(
        kern,
        grid=(b, s // tq),
        in_specs=[
            pl.BlockSpec((None, tq, aw), lambda bi, i: (bi, i, 0)),
            pl.BlockSpec((None, tq, qi.shape[2]), lambda bi, i: (bi, i, 0)),
            pl.BlockSpec((None, IDX_HEADS, tq), lambda bi, i: (bi, 0, i)),
            pl.BlockSpec((None, s, aw), lambda bi, i: (bi, 0, 0)),
            pl.BlockSpec((None, s // tk, aw, tk), lambda bi, i: (bi, 0, 0, 0)),
            pl.BlockSpec((None, s, IDX_DIM), lambda bi, i: (bi, 0, 0)),
        ],
        out_specs=pl.BlockSpec((None, tq, aw), lambda bi, i: (bi, i, 0)),
        out_shape=jax.ShapeDtypeStruct((b, s, aw), BF16),
        scratch_shapes=[
            pltpu.VMEM((s, tq), F32),
            pltpu.VMEM((n_heads, 1, tq), F32),
            pltpu.VMEM((n_heads, 1, tq), F32),
            pltpu.VMEM((n_heads, HEAD_DIM, tq), F32),
        ],
        compiler_params=_cparams(2),
        name="attn_prompt",
    )(q, qi, wit, k, vt, ki)


def _idx_sample_kernel(pt_ref, qi_ref, w_ref, *rest, n_pages, group, t_new, topk):
    cik_refs = rest[:group]
    kin_ref, sc_ref, thr_ref = rest[group:]
    p = pl.program_id(1)
    rows = qi_ref.shape[0]

    def scores(keys_bf16):
        s = lax.dot_general(qi_ref[...], keys_bf16, NT_DIMS, preferred_element_type=F32)
        r = jnp.maximum(s, 0.0) * w_ref[...]
        return jnp.sum(r.reshape(rows // t_new, t_new, LANES), axis=0)

    for g in range(group):
        sc_ref[p * group + g] = scores(cik_refs[g][...].astype(BF16))

    @pl.when(p == pl.num_programs(1) - 1)
    def _():
        s = scores(kin_ref[...])
        tok = lax.broadcasted_iota(jnp.int32, (t_new, LANES), 0)
        lane = lax.broadcasted_iota(jnp.int32, (t_new, LANES), 1)
        sc_ref[n_pages] = jnp.where((lane <= tok) & (lane < t_new), s, -jnp.inf)

        def reduce_rows(fn, init, red):
            a = lax.fori_loop(0, n_pages + 1, lambda c, a: fn(a, sc_ref[c]), jnp.full((t_new, LANES), init, F32))
            return red(a, axis=1, keepdims=True)

        def count_ge(t):
            tb = jnp.broadcast_to(t, (t_new, LANES))
            return reduce_rows(lambda a, blk: a + jnp.where(blk >= tb, 1.0, 0.0), 0.0, jnp.sum)

        rmax = reduce_rows(jnp.maximum, -jnp.inf, jnp.max)
        rmin = reduce_rows(lambda a, blk: jnp.minimum(a, jnp.where(blk == -jnp.inf, jnp.inf, blk)),
                           jnp.inf, jnp.min)
        tokc = lax.broadcasted_iota(jnp.int32, (t_new, 1), 0)
        n_valid = (n_pages * LANES + tokc + 1).astype(F32)
        kq = jnp.minimum(float(topk), n_valid)
        thr = _select_threshold(count_ge, rmin, rmax, n_valid, kq)
        thr_ref[...] = jnp.broadcast_to(thr, (t_new, LANES))


def _page_map(b, p, pt, *, g, group):
    return (0, pt[b, p * group + g], 0, 0)


def idx_sample(page_table, qi_ht, w_ht, cache_idx_k, ki_new_pad, *, t_new, topk, group=16):
    db, n_pages = page_table.shape
    rows = qi_ht.shape[1]
    page = cache_idx_k.shape[2]
    assert n_pages % group == 0
    kern = functools.partial(_idx_sample_kernel, n_pages=n_pages, group=group, t_new=t_new, topk=topk)
    page_specs = [pl.BlockSpec((None, None, page, IDX_DIM), functools.partial(_page_map, g=g, group=group))
                  for g in range(group)]
    grid_spec = pltpu.PrefetchScalarGridSpec(
        num_scalar_prefetch=1,
        grid=(db, n_pages // group),
        in_specs=[
            pl.BlockSpec((None, rows, IDX_DIM), lambda b, p, pt: (b, 0, 0)),
            pl.BlockSpec((None, rows, LANES), lambda b, p, pt: (b, 0, 0)),
            *page_specs,
            pl.BlockSpec((None, page, IDX_DIM), lambda b, p, pt: (b, 0, 0)),
        ],
        out_specs=[
            pl.BlockSpec((None, n_pages + 1, t_new, LANES), lambda b, p, pt: (b, 0, 0, 0)),
            pl.BlockSpec((None, t_new, LANES), lambda b, p, pt: (b, 0, 0)),
        ],
    )
    return pl.pallas_call(
        kern,
        grid_spec=grid_spec,
        out_shape=[jax.ShapeDtypeStruct((db, n_pages + 1, t_new, LANES), F32),
                   jax.ShapeDtypeStruct((db, t_new, LANES), F32)],
        compiler_params=_cparams(2),
        name="idx_sample",
    )(page_table, qi_ht, w_ht, *([cache_idx_k] * group), ki_new_pad)


def _attn_sample_kernel(pt_ref, q_ref, *rest, group, page, n_heads):
    ck_refs = rest[:group]
    cv_refs = rest[group:2 * group]
    kn_ref, vn_ref, sc_ref, scn_ref, thr_ref, o_ref, m_ref, l_ref, acc_ref = rest[2 * group:]
    p = pl.program_id(1)

    @pl.when(p == 0)
    def _():
        m_ref[...] = jnp.full(m_ref.shape, -jnp.inf, F32)
        l_ref[...] = jnp.zeros(l_ref.shape, F32)
        acc_ref[...] = jnp.zeros(acc_ref.shape, F32)

    thr = thr_ref[...]

    def head_rows(ref, h):
        return ref[pl.ds(h, page, stride=n_heads), :].astype(BF16)

    def attend(h, k_refs, v_refs, masks):
        qh = q_ref[h]
        kh = jnp.concatenate([head_rows(r, h) for r in k_refs], axis=0)
        s = lax.dot_general(qh, kh, NT_DIMS, preferred_element_type=F32)
        s = jnp.where(masks, s, NEG)
        m_prev = m_ref[h]
        m_new = jnp.maximum(m_prev, jnp.max(s, axis=1, keepdims=True))
        alpha = jnp.exp2(m_prev - m_new)
        pr = jnp.exp2(s - jnp.concatenate([m_new] * len(k_refs), axis=1))
        l_ref[h] = alpha * l_ref[h] + jnp.sum(pr, axis=1, keepdims=True)
        m_ref[h] = m_new
        vh = jnp.concatenate([head_rows(r, h) for r in v_refs], axis=0)
        acc_ref[h] = alpha * acc_ref[h] + jnp.dot(pr.astype(BF16), vh, preferred_element_type=F32)

    page_masks = (jnp.concatenate([sc_ref[g] for g in range(group)], axis=1)
                  >= jnp.concatenate([thr] * group, axis=1))
    for h in range(n_heads):
        attend(h, ck_refs, cv_refs, page_masks)

    @pl.when(p == pl.num_programs(1) - 1)
    def _():
        new_mask = scn_ref[...] >= thr
        for h in range(n_heads):
            attend(h, [kn_ref], [vn_ref], new_mask)
            o_ref[h] = acc_ref[h] / l_ref[h]


def attn_sample(page_table, q_ht, cache_k, cache_v, k_new_pad, v_new_pad, sc, thr, *, group=8):
    db, n_pages = page_table.shape
    n_heads, t_new = q_ht.shape[1], q_ht.shape[2]
    prow = cache_k.shape[2]
    page = prow // n_heads
    assert n_pages % group == 0
    kern = functools.partial(_attn_sample_kernel, group=group, page=page, n_heads=n_heads)
    cache_specs = [pl.BlockSpec((None, None, prow, HEAD_DIM), functools.partial(_page_map, g=g, group=group))
                   for g in range(group)]
    new_spec = pl.BlockSpec((None, prow, HEAD_DIM), lambda b, p, pt: (b, 0, 0))
    head_spec = pl.BlockSpec((None, n_heads, t_new, HEAD_DIM), lambda b, p, pt: (b, 0, 0, 0))
    grid_spec = pltpu.PrefetchScalarGridSpec(
        num_scalar_prefetch=1,
        grid=(db, n_pages // group),
        in_specs=[
            head_spec,
            *cache_specs, *cache_specs, new_spec, new_spec,
            pl.BlockSpec((None, group, t_new, LANES), lambda b, p, pt: (b, p, 0, 0)),
            pl.BlockSpec((None, None, t_new, LANES), lambda b, p, pt: (b, n_pages, 0, 0)),
            pl.BlockSpec((None, t_new, LANES), lambda b, p, pt: (b, 0, 0)),
        ],
        out_specs=head_spec,
        scratch_shapes=[
            pltpu.VMEM((n_heads, t_new, LANES), F32),
            pltpu.VMEM((n_heads, t_new, LANES), F32),
            pltpu.VMEM((n_heads, t_new, HEAD_DIM), F32),
        ],
    )
    return pl.pallas_call(
        kern,
        grid_spec=grid_spec,
        out_shape=jax.ShapeDtypeStruct((db, n_heads, t_new, HEAD_DIM), F32),
        compiler_params=_cparams(2),
        name="attn_sample",
    )(page_table, q_ht, *([cache_k] * group), *([cache_v] * group), k_new_pad, v_new_pad, sc, sc, thr)


def _outproj_kernel(x_ref, yp_ref, ya_ref, wt_ref, wb_ref, g_ref, x2_ref, h2_ref):
    y = jnp.dot(yp_ref[...], wt_ref[...], preferred_element_type=F32)
    y = y + jnp.dot(ya_ref[...], wb_ref[...], preferred_element_type=F32)
    x2 = x_ref[...] + y
    x2_ref[...] = x2
    h2_ref[...] = _rms(x2, g_ref[...]).astype(h2_ref.dtype)


def outproj(x, yp, ya, w_top, w_bot, g, *, tm):
    m, d = x.shape
    kp, ka = yp.shape[1], ya.shape[1]
    return pl.pallas_call(
        _outproj_kernel,
        grid=(m // tm,),
        in_specs=[
            pl.BlockSpec((tm, d), lambda i: (i, 0)),
            pl.BlockSpec((tm, kp), lambda i: (i, 0)),
            pl.BlockSpec((tm, ka), lambda i: (i, 0)),
            pl.BlockSpec((kp, d), lambda i: (0, 0)),
            pl.BlockSpec((ka, d), lambda i: (0, 0)),
            pl.BlockSpec((1, d), lambda i: (0, 0)),
        ],
        out_specs=[pl.BlockSpec((tm, d), lambda i: (i, 0)), pl.BlockSpec((tm, d), lambda i: (i, 0))],
        out_shape=[jax.ShapeDtypeStruct((m, d), F32), jax.ShapeDtypeStruct((m, d), BF16)],
        compiler_params=_cparams(1),
        name="outproj",
    )(x, yp, ya, w_top, w_bot, g.reshape(1, d))


def _ffn_kernel(h2_ref, wg_ref, wu_ref, wd_ref, x2_ref, g_ref, o_ref, acc_ref):
    f = pl.program_id(1)

    @pl.when(f == 0)
    def _():
        acc_ref[...] = jnp.zeros(acc_ref.shape, F32)

    h2 = h2_ref[...]
    gate = jnp.dot(h2, wg_ref[...], preferred_element_type=F32)
    up = jnp.dot(h2, wu_ref[...], preferred_element_type=F32)
    a = (gate * jax.nn.sigmoid(gate) * up).astype(BF16)
    acc_ref[...] += jnp.dot(a, wd_ref[...], preferred_element_type=F32)

    @pl.when(f == pl.num_programs(1) - 1)
    def _():
        o_ref[...] = _rms(x2_ref[...] + acc_ref[...], g_ref[...])


def ffn(h2, wg, wu, wd, x2, g, *, tm, tf):
    m, d = h2.shape
    dff = wg.shape[1]
    return pl.pallas_call(
        _ffn_kernel,
        grid=(m // tm, dff // tf),
        in_specs=[
            pl.BlockSpec((tm, d), lambda i, f: (i, 0)),
            pl.BlockSpec((d, tf), lambda i, f: (0, f)),
            pl.BlockSpec((d, tf), lambda i, f: (0, f)),
            pl.BlockSpec((tf, d), lambda i, f: (f, 0)),
            pl.BlockSpec((tm, d), lambda i, f: (i, 0)),
            pl.BlockSpec((1, d), lambda i, f: (0, 0)),
        ],
        out_specs=pl.BlockSpec((tm, d), lambda i, f: (i, 0)),
        out_shape=jax.ShapeDtypeStruct((m, d), F32),
        scratch_shapes=[pltpu.VMEM((tm, d), F32)],
        compiler_params=_cparams(2),
        name="ffn",
    )(h2, wg, wu, wd, x2, g.reshape(1, d))


def _rope_tables(pos):
    half = HEAD_DIM // 2
    inv = ROPE_THETA ** (-jnp.arange(half, dtype=F32) / half)
    ang = pos.astype(F32)[:, None] * inv[None, :]
    cos, sin = jnp.cos(ang), jnp.sin(ang)
    return jnp.concatenate([cos, cos], axis=-1), jnp.concatenate([-sin, sin], axis=-1)


def _mix_inputs(x2d, pos_rows, g_mix, w_segs, *, tm):
    cos, sin = _rope_tables(pos_rows)
    h = rmsnorm_bf16(x2d, g_mix, tm)
    w_up, w_q, w_k, w_v, w_qi, w_ki, w_wi = w_segs
    tn = 1024
    (up,) = proj(h, w_up, (F32,), tm=tm, tn=tn, name="proj_up")
    (q,) = proj(h, w_q, (BF16,), tm=tm, tn=tn, cos=cos, sin=sin, scale=HEAD_DIM ** -0.5 * LOG2E, name="proj_q")
    k32, k16 = proj(h, w_k, (F32, BF16), tm=tm, tn=tn, cos=cos, sin=sin, name="proj_k")
    v32, v16 = proj(h, w_v, (F32, BF16), tm=tm, tn=tn, name="proj_v")
    (qi,) = proj(h, w_qi, (BF16,), tm=tm, tn=tn, cos=cos, sin=sin, name="proj_qi")
    ki32, ki16 = proj(h, w_ki, (F32, BF16), tm=tm, tn=IDX_DIM, cos=cos, sin=sin, name="proj_ki")
    (wi,) = proj(h, w_wi, (F32,), tm=tm, tn=LANES, scale=IDX_HEADS ** -0.5, name="proj_wi")
    return up, q, k32, k16, v32, v16, qi, ki32, ki16, wi


def kernel(x_prompt, x_sample, cache_k, cache_v, cache_idx_k, state_pool, page_table, g_mix, w_in, w_pool,
           s_pool, w_out, g_ffn, w_gate, w_up, w_down, g_final):
    B, S, D = x_prompt.shape
    DB, T, _ = x_sample.shape
    depth = w_in.shape[0]
    assert depth == 1
    n_pages = page_table.shape[1]
    page = cache_k.shape[2]
    n_heads = cache_k.shape[3]
    aw = n_heads * HEAD_DIM
    pw = w_pool.shape[1] * w_pool.shape[2]
    past = n_pages * page
    l = 0

    splits = (pw, aw, aw, aw, IDX_HEADS * IDX_DIM, IDX_DIM, IDX_HEADS)
    offs = [0]
    for s_ in splits:
        offs.append(offs[-1] + s_)
    w_in16 = w_in[l].astype(BF16)
    w_segs = [w_in16[:, offs[j]:offs[j + 1]] for j in range(len(splits))]
    w_segs[6] = jnp.pad(w_segs[6], ((0, 0), (0, LANES - IDX_HEADS)))
    w_pool16 = w_pool[l].astype(BF16)
    w_out16 = w_out[l].astype(BF16)
    w_top, w_bot = w_out16[:pw], w_out16[pw:]
    wg16, wu16, wd16 = w_gate[l].astype(BF16), w_up[l].astype(BF16), w_down[l].astype(BF16)

    xp2 = x_prompt.reshape(B * S, D)
    pos_p = jnp.tile(jnp.arange(S), B)
    up, q, k32, k16, v32, v16, qi, ki32, ki16, wi = _mix_inputs(xp2, pos_p, g_mix[l], w_segs, tm=1024)
    up3 = up.reshape(B, S, pw)
    yp = pool_mixer(up3, jnp.zeros((B, HALO, pw), F32), w_pool16, s_pool[l], tm=512, pos0=0)
    tk = ATTN_TK
    wit = wi.reshape(B, S, LANES)[:, :, :IDX_HEADS].transpose(0, 2, 1)
    vt = v16.reshape(B, S // tk, tk, aw).transpose(0, 1, 3, 2)
    ya = attn_prompt(q.reshape(B, S, aw), qi.reshape(B, S, -1), wit, k16.reshape(B, S, aw), vt,
                     ki16.reshape(B, S, IDX_DIM), topk=min(TOPK_MAX, S // 4))
    x2, h2 = outproj(xp2, yp.reshape(B * S, pw), ya.reshape(B * S, aw), w_top, w_bot, g_ffn[l], tm=512)
    y_prompt = ffn(h2, wg16, wu16, wd16, x2, g_final, tm=512, tf=512).reshape(B, S, D)

    k_prompt = k32.reshape(1, B, S, n_heads, HEAD_DIM)
    v_prompt = v32.reshape(1, B, S, n_heads, HEAD_DIM)
    idx_k_prompt = ki32.reshape(1, B, S, IDX_DIM)
    pool_prompt = up3[:, S - POOL_STATE:, :][None]

    M = DB * T
    xs2 = x_sample.reshape(M, D)
    pos_s = jnp.tile(past + jnp.arange(T), DB)
    ups, qs, ks32, ks16, vs32, vs16, qis, kis32, kis16, wis = _mix_inputs(xs2, pos_s, g_mix[l], w_segs, tm=M)
    ups3 = ups.reshape(DB, T, pw)
    prefix16 = jnp.pad(state_pool[l], ((0, 0), (HALO - POOL_STATE, 0), (0, 0)))
    yps = pool_mixer(ups3, prefix16, w_pool16, s_pool[l], tm=T, pos0=past)

    qi_ht = qis.reshape(DB, T, IDX_HEADS, IDX_DIM).transpose(0, 2, 1, 3).reshape(DB, IDX_HEADS * T, IDX_DIM)
    w_ht = wis.reshape(DB, T, LANES)[:, :, :IDX_HEADS].transpose(0, 2, 1).reshape(DB, IDX_HEADS * T, 1)
    w_ht = jnp.broadcast_to(w_ht * (IDX_DIM ** -0.5), (DB, IDX_HEADS * T, LANES))
    pad_rows = lambda a: jnp.pad(a.reshape(DB, T, -1), ((0, 0), (0, page - T), (0, 0)))
    sc, thr = idx_sample(page_table, qi_ht, w_ht, cache_idx_k, pad_rows(kis16), t_new=T,
                         topk=min(TOPK_MAX, (past + T) // 4))

    n_phys = cache_k.shape[1]
    key_head_rows = lambda a: pad_rows(a).reshape(DB, page * n_heads, HEAD_DIM)
    q_ht = qs.reshape(DB, T, n_heads, HEAD_DIM).transpose(0, 2, 1, 3)
    o_ht = attn_sample(page_table, q_ht, cache_k.reshape(depth, n_phys, page * n_heads, HEAD_DIM),
                       cache_v.reshape(depth, n_phys, page * n_heads, HEAD_DIM),
                       key_head_rows(ks32), key_head_rows(vs32), sc, thr)
    yas = o_ht.transpose(0, 2, 1, 3).reshape(M, aw).astype(BF16)

    x2s, h2s = outproj(xs2, yps.reshape(M, pw), yas, w_top, w_bot, g_ffn[l], tm=M)
    y_sample = ffn(h2s, wg16, wu16, wd16, x2s, g_final, tm=M, tf=512).reshape(DB, T, D)

    k_sample = ks32.reshape(1, DB, T, n_heads, HEAD_DIM)
    v_sample = vs32.reshape(1, DB, T, n_heads, HEAD_DIM)
    idx_k_sample = kis32.reshape(1, DB, T, IDX_DIM)
    pool_sample = jnp.concatenate([state_pool[l][:, T:, :], ups3], axis=1)[None]

    return (y_prompt, y_sample, k_prompt, v_prompt, idx_k_prompt, pool_prompt,
            k_sample, v_sample, idx_k_sample, pool_sample)
```

```python
import functools

import jax
import jax.numpy as jnp
from jax import lax
from jax.experimental import pallas as pl
from jax.experimental.pallas import tpu as pltpu

F32 = jnp.float32
BF16 = jnp.bfloat16

LANES = 128
HEAD_DIM = 128
IDX_DIM = 128
IDX_HEADS = 16
POOL_GROUPS = 4
POOL_WINDOWS = (2, 4, 8, 16)
POOL_STATE = max(POOL_WINDOWS) - 1
HALO = 16
TOPK_MAX = 256
ATTN_TQ = 256
ATTN_TK = 128
ROPE_THETA = 10000.0
EPS = 1e-6
NEG = -1e30
MAX_SEARCH_STEPS = 400
LOG2E = 1.4426950408889634
VMEM_LIMIT = 56 * 1024 * 1024

NT_DIMS = (((1,), (1,)), ((), ()))


def _cparams(n_grid):
    return pltpu.CompilerParams(dimension_semantics=("arbitrary",) * n_grid, vmem_limit_bytes=VMEM_LIMIT)


def _rms(x, g):
    return x * lax.rsqrt(jnp.mean(x * x, axis=-1, keepdims=True) + EPS) * g


PROJ_TN = 1024


def _rope(y, cos, sin):
    return [y[:, g * HEAD_DIM:(g + 1) * HEAD_DIM] * cos
            + pltpu.roll(y[:, g * HEAD_DIM:(g + 1) * HEAD_DIM], HEAD_DIM // 2, axis=1) * sin
            for g in range(y.shape[1] // HEAD_DIM)]


def _mixproj_kernel(x_ref, g_ref, w_ref, wt_ref, cos_ref, sin_ref,
                    up_ref, q_ref, k32_ref, k16_ref, v32_ref, v16_ref, qi_ref, ki32_ref, ki16_ref, wi_ref, h_ref):
    j = pl.program_id(1)

    def heads_to(pieces, refs, col0=0):
        for g, r in enumerate(pieces):
            for o_ref in refs:
                o_ref[:, col0 + g * HEAD_DIM:col0 + (g + 1) * HEAD_DIM] = r.astype(o_ref.dtype)

    @pl.when(j == 0)
    def _():
        h_ref[...] = _rms(x_ref[...], g_ref[...]).astype(h_ref.dtype)
        t = jnp.dot(h_ref[...], wt_ref[...], preferred_element_type=F32)
        heads_to(_rope(t[:, :IDX_DIM], cos_ref[...], sin_ref[...]), (ki32_ref, ki16_ref))
        wi_ref[...] = t[:, IDX_DIM:] * (IDX_HEADS ** -0.5)

    y = jnp.dot(h_ref[...], w_ref[...], preferred_element_type=F32)

    @pl.when(j == 0)
    def _():
        up_ref[...] = y

    @pl.when(j == 1)
    def _():
        heads_to(_rope(y * (HEAD_DIM ** -0.5 * LOG2E), cos_ref[...], sin_ref[...]), (q_ref,))

    @pl.when(j == 2)
    def _():
        heads_to(_rope(y, cos_ref[...], sin_ref[...]), (k32_ref, k16_ref))

    @pl.when(j == 3)
    def _():
        v32_ref[...] = y
        v16_ref[...] = y.astype(v16_ref.dtype)

    for t_qi in range(2):
        @pl.when(j == 4 + t_qi)
        def _():
            heads_to(_rope(y, cos_ref[...], sin_ref[...]), (qi_ref,), col0=t_qi * PROJ_TN)


def mixproj(x, g, w_in16, w_tail, cos, sin, *, tm):
    m, d = x.shape
    tn = PROJ_TN
    n_tiles = 6
    row = lambda width: pl.BlockSpec((tm, width), lambda i, j: (i, 0))
    widths = (tn, tn, tn, tn, tn, tn, 2 * tn, IDX_DIM, IDX_DIM, LANES)
    dtypes = (F32, BF16, F32, BF16, F32, BF16, BF16, F32, BF16, F32)
    return pl.pallas_call(
        _mixproj_kernel,
        grid=(m // tm, n_tiles),
        in_specs=[
            row(d),
            pl.BlockSpec((1, d), lambda i, j: (0, 0)),
            pl.BlockSpec((d, tn), lambda i, j: (0, j)),
            pl.BlockSpec(w_tail.shape, lambda i, j: (0, 0)),
            row(HEAD_DIM), row(HEAD_DIM),
        ],
        out_specs=[row(w) for w in widths],
        out_shape=[jax.ShapeDtypeStruct((m, w), dt) for w, dt in zip(widths, dtypes)],
        scratch_shapes=[pltpu.VMEM((tm, d), BF16)],
        compiler_params=_cparams(2),
        name="mixproj",
    )(x, g.reshape(1, d), w_in16, w_tail, cos, sin)


def _pool_kernel(*refs, tm, pos0, has_prev):
    if has_prev:
        up_ref, prev_ref, prefix_ref, w_ref, s_ref, o_ref, ext_ref = refs
    else:
        up_ref, prefix_ref, w_ref, s_ref, o_ref, ext_ref = refs
    i = pl.program_id(1)
    if has_prev:
        halo = jnp.where(i == 0, prefix_ref[...], prev_ref[...])
    else:
        halo = prefix_ref[...]
    ext_ref[0:HALO, :] = halo
    ext_ref[HALO:HALO + tm, :] = up_ref[...]
    pos = pos0 + i * tm + lax.broadcasted_iota(jnp.int32, (tm, 1), 0)
    gw = up_ref.shape[1] // POOL_GROUPS
    for g, w in enumerate(POOL_WINDOWS):
        sl = slice(g * gw, (g + 1) * gw)
        cur = ext_ref[HALO:HALO + tm, sl]
        s = cur
        for j in range(1, w):
            s = s + ext_ref[HALO - j:HALO - j + tm, sl]
        cnt = jnp.minimum(w, pos + 1).astype(F32)
        d = s / cnt - cur
        y = jnp.dot(d.astype(BF16), w_ref[g], preferred_element_type=F32) * s_ref[:, sl]
        o_ref[:, sl] = y.astype(o_ref.dtype)


def pool_mixer(up, prefix16, w_pool, s_pool, *, tm, pos0):
    b, t, wd = up.shape
    has_prev = t > tm
    gw = wd // POOL_GROUPS
    in_specs = [pl.BlockSpec((None, tm, wd), lambda bi, i: (bi, i, 0))]
    args = [up]
    if has_prev:
        r = tm // HALO
        in_specs.append(pl.BlockSpec((None, HALO, wd), lambda bi, i: (bi, jnp.maximum(i * r - 1, 0), 0)))
        args.append(up)
    in_specs += [
        pl.BlockSpec((None, HALO, wd), lambda bi, i: (bi, 0, 0)),
        pl.BlockSpec((POOL_GROUPS, gw, gw), lambda bi, i: (0, 0, 0)),
        pl.BlockSpec((1, wd), lambda bi, i: (0, 0)),
    ]
    args += [prefix16, w_pool, s_pool.reshape(1, wd)]
    return pl.pallas_call(
        functools.partial(_pool_kernel, tm=tm, pos0=pos0, has_prev=has_prev),
        grid=(b, t // tm),
        in_specs=in_specs,
        out_specs=pl.BlockSpec((None, tm, wd), lambda bi, i: (bi, i, 0)),
        out_shape=jax.ShapeDtypeStruct((b, t, wd), BF16),
        scratch_shapes=[pltpu.VMEM((HALO + tm, wd), F32)],
        compiler_params=_cparams(2),
        name="pool_mixer",
    )(*args)


def _select_threshold(count_ge, rmin, rmax, n_valid, kq):
    c_hi = count_ge(rmax)
    top_tied = c_hi >= kq
    lo0 = jnp.where(top_tied, rmax, rmin)
    cl0 = jnp.where(top_tied, c_hi, n_valid)

    def not_done(cl, stuck):
        return jnp.max(jnp.where((cl == kq) | (stuck > 0.0), 0.0, 1.0))

    def cond(st):
        it, flag = st[0], st[1]
        return jnp.logical_and(it < MAX_SEARCH_STEPS, flag > 0.0)

    def body(st):
        it, _, lo, hi, cl, ch, stuck = st
        halve = (it % 2) == 1
        frac = jnp.where(halve, 0.5, (cl - kq - 0.5) / jnp.maximum(cl - ch, 1.0))
        mid = jnp.clip(lo + frac * (hi - lo), lo, hi)
        stuck = jnp.where(halve, jnp.where((mid <= lo) | (mid >= hi), 1.0, 0.0), stuck)
        c = count_ge(mid)
        ge = c >= kq
        lo2 = jnp.where(ge, mid, lo)
        cl2 = jnp.where(ge, c, cl)
        hi2 = jnp.where(ge, hi, mid)
        ch2 = jnp.where(ge, ch, c)
        return it + 1, not_done(cl2, stuck), lo2, hi2, cl2, ch2, stuck

    stuck0 = jnp.where(lo0 >= rmax, 1.0, 0.0)
    st = (jnp.int32(0), not_done(cl0, stuck0), lo0, rmax, cl0, c_hi, stuck0)
    st = lax.while_loop(cond, body, st)
    return st[2]


def _attn_prompt_kernel(q_ref, qi_ref, wit_ref, k_ref, vt_ref, ki_ref, o_ref,
                        sc_ref, m_ref, l_ref, acc_ref, *, tk, topk, n_heads):
    i = pl.program_id(1)
    tq = q_ref.shape[0]
    nc = ((i + 1) * tq + tk - 1) // tk
    qpos = i * tq + lax.broadcasted_iota(jnp.int32, (1, tq), 1)
    key_row = lax.broadcasted_iota(jnp.int32, (tk, tq), 0)
    w_rows = wit_ref[...] * (IDX_DIM ** -0.5)

    def idx_chunk(c, carry):
        off = pl.multiple_of(c * tk, tk)
        kic = ki_ref[pl.ds(off, tk), :]
        acc = jnp.zeros((tk, tq), F32)
        for h in range(IDX_HEADS):
            s = lax.dot_general(kic, qi_ref[:, h * IDX_DIM:(h + 1) * IDX_DIM], NT_DIMS,
                                preferred_element_type=F32)
            acc = acc + jnp.maximum(s, 0.0) * w_rows[h:h + 1, :]
        sc_ref[pl.ds(off, tk), :] = jnp.where(off + key_row <= qpos, acc, -jnp.inf)
        return carry

    lax.fori_loop(0, nc, idx_chunk, 0)

    lanes_par = 4

    def reduce_keys(fn, init, red):
        def body(c, a):
            blk = sc_ref[pl.ds(pl.multiple_of(c * tk, tk), tk), :]
            return fn(a, blk.reshape(lanes_par, tk // (8 * lanes_par), 8, tq))
        a = lax.fori_loop(0, nc, body, jnp.full((lanes_par, 8, tq), init, F32))
        return red(red(a, axis=0), axis=0, keepdims=True)

    def count_ge(t):
        return reduce_keys(lambda a, blk: a + jnp.sum(jnp.where(blk >= t, 1.0, 0.0), axis=1), 0.0, jnp.sum)

    rmax = reduce_keys(lambda a, blk: jnp.maximum(a, jnp.max(blk, axis=1)), -jnp.inf, jnp.max)
    rmin = reduce_keys(lambda a, blk: jnp.minimum(a, jnp.min(jnp.where(blk == -jnp.inf, jnp.inf, blk), axis=1)),
                       jnp.inf, jnp.min)
    n_valid = (qpos + 1).astype(F32)
    kq = jnp.minimum(float(topk), n_valid)
    thr = _select_threshold(count_ge, rmin, rmax, n_valid, kq)

    m_ref[...] = jnp.full(m_ref.shape, -jnp.inf, F32)
    l_ref[...] = jnp.zeros(l_ref.shape, F32)
    acc_ref[...] = jnp.zeros(acc_ref.shape, F32)

    def att_chunk(c, carry):
        off = pl.multiple_of(c * tk, tk)
        mask = sc_ref[pl.ds(off, tk), :] >= thr
        for h in range(n_heads):
            hs = slice(h * HEAD_DIM, (h + 1) * HEAD_DIM)
            s = lax.dot_general(k_ref[pl.ds(off, tk), hs], q_ref[:, hs], NT_DIMS,
                                preferred_element_type=F32)
            s = jnp.where(mask, s, NEG)
            m_prev = m_ref[h]
            m_new = jnp.maximum(m_prev, jnp.max(s, axis=0, keepdims=True))
            alpha = jnp.exp2(m_prev - m_new)
            p = jnp.exp2(s - m_new)
            l_ref[h] = alpha * l_ref[h] + jnp.sum(p, axis=0, keepdims=True)
            m_ref[h] = m_new
            pv = jnp.dot(vt_ref[c, hs, :], p.astype(BF16), preferred_element_type=F32)
            acc_ref[h] = alpha * acc_ref[h] + pv
        return carry

    lax.fori_loop(0, nc, att_chunk, 0)

    for h in range(n_heads):
        hs = slice(h * HEAD_DIM, (h + 1) * HEAD_DIM)
        o_ref[:, hs] = (acc_ref[h] / l_ref[h]).T.astype(o_ref.dtype)


def attn_prompt(q, qi, wit, k, vt, ki, *, topk):
    b, s, aw = q.shape
    n_heads = aw // HEAD_DIM
    tq = ATTN_TQ
    tk = vt.shape[3]
    kern = functools.partial(_attn_prompt_kernel, tk=tk, topk=topk, n_heads=n_heads)
    return pl.pallas_call(
        kern,
        grid=(b, s // tq),
        in_specs=[
            pl.BlockSpec((None, tq, aw), lambda bi, i: (bi, i, 0)),
            pl.BlockSpec((None, tq, qi.shape[2]), lambda bi, i: (bi, i, 0)),
            pl.BlockSpec((None, IDX_HEADS, tq), lambda bi, i: (bi, 0, i)),
            pl.BlockSpec((None, s, aw), lambda bi, i: (bi, 0, 0)),
            pl.BlockSpec((None, s // tk, aw, tk), lambda bi, i: (bi, 0, 0, 0)),
            pl.BlockSpec((None, s, IDX_DIM), lambda bi, i: (bi, 0, 0)),
        ],
        out_specs=pl.BlockSpec((None, tq, aw), lambda bi, i: (bi, i, 0)),
        out_shape=jax.ShapeDtypeStruct((b, s, aw), BF16),
        scratch_shapes=[
            pltpu.VMEM((s, tq), F32),
            pltpu.VMEM((n_heads, 1, tq), F32),
            pltpu.VMEM((n_heads, 1, tq), F32),
            pltpu.VMEM((n_heads, HEAD_DIM, tq), F32),
        ],
        compiler_params=_cparams(2),
        name="attn_prompt",
    )(q, qi, wit, k, vt, ki)


def _idx_sample_kernel(pt_ref, qi_ref, w_ref, *rest, n_pages, group, t_new, topk):
    cik_refs = rest[:group]
    kin_ref, sc_ref, thr_ref = rest[group:]
    b, p = pl.program_id(0), pl.program_id(1)
    db = sc_ref.shape[0]
    rows = qi_ref.shape[0]

    def scores(keys_bf16):
        s = lax.dot_general(qi_ref[...], keys_bf16, NT_DIMS, preferred_element_type=F32)
        r = jnp.maximum(s, 0.0) * w_ref[...]
        return jnp.sum(r.reshape(rows // t_new, t_new, LANES), axis=0)

    for g in range(group):
        sc_ref[b, p * group + g] = scores(cik_refs[g][...].astype(BF16))

    @pl.when(p == 0)
    def _():
        s = scores(kin_ref[...])
        tok = lax.broadcasted_iota(jnp.int32, (t_new, LANES), 0)
        lane = lax.broadcasted_iota(jnp.int32, (t_new, LANES), 1)
        sc_ref[b, n_pages] = jnp.where((lane <= tok) & (lane < t_new), s, -jnp.inf)

    @pl.when((b == db - 1) & (p == pl.num_programs(1) - 1))
    def _():
        def reduce_keys(fn, red):
            return red(red(fn(sc_ref[...]), axis=1), axis=2, keepdims=True)

        def count_ge(t):
            return reduce_keys(lambda x: jnp.where(x >= t[:, None], 1.0, 0.0), jnp.sum)

        rmax = reduce_keys(lambda x: x, jnp.max)
        rmin = reduce_keys(lambda x: jnp.where(x == -jnp.inf, jnp.inf, x), jnp.min)
        tokc = lax.broadcasted_iota(jnp.int32, (db, t_new, 1), 1)
        n_valid = (n_pages * LANES + tokc + 1).astype(F32)
        kq = jnp.minimum(float(topk), n_valid)
        thr = _select_threshold(count_ge, rmin, rmax, n_valid, kq)
        thr_ref[...] = jnp.broadcast_to(thr, thr_ref.shape)


def _page_map(b, p, pt, *, g, group):
    return (0, pt[b, p * group + g], 0, 0)


def idx_sample(page_table, qi_ht, w_ht, cache_idx_k, ki_new_pad, *, t_new, topk, group=16):
    db, n_pages = page_table.shape
    rows = qi_ht.shape[1]
    page = cache_idx_k.shape[2]
    assert n_pages % group == 0
    kern = functools.partial(_idx_sample_kernel, n_pages=n_pages, group=group, t_new=t_new, topk=topk)
    page_specs = [pl.BlockSpec((None, None, page, IDX_DIM), functools.partial(_page_map, g=g, group=group))
                  for g in range(group)]
    grid_spec = pltpu.PrefetchScalarGridSpec(
        num_scalar_prefetch=1,
        grid=(db, n_pages // group),
        in_specs=[
            pl.BlockSpec((None, rows, IDX_DIM), lambda b, p, pt: (b, 0, 0)),
            pl.BlockSpec((None, rows, LANES), lambda b, p, pt: (b, 0, 0)),
            *page_specs,
            pl.BlockSpec((None, page, IDX_DIM), lambda b, p, pt: (b, 0, 0)),
        ],
        out_specs=[
            pl.BlockSpec((db, n_pages + 1, t_new, LANES), lambda b, p, pt: (0, 0, 0, 0)),
            pl.BlockSpec((db, t_new, LANES), lambda b, p, pt: (0, 0, 0)),
        ],
    )
    return pl.pallas_call(
        kern,
        grid_spec=grid_spec,
        out_shape=[jax.ShapeDtypeStruct((db, n_pages + 1, t_new, LANES), F32),
                   jax.ShapeDtypeStruct((db, t_new, LANES), F32)],
        compiler_params=_cparams(2),
        name="idx_sample",
    )(page_table, qi_ht, w_ht, *([cache_idx_k] * group), ki_new_pad)


def _attn_sample_kernel(pt_ref, q_ref, *rest, group, page, n_heads):
    ck_refs = rest[:group]
    cv_refs = rest[group:2 * group]
    kn_ref, vn_ref, sc_ref, scn_ref, thr_ref, o_ref, m_ref, l_ref, acc_ref = rest[2 * group:]
    p = pl.program_id(1)

    @pl.when(p == 0)
    def _():
        m_ref[...] = jnp.full(m_ref.shape, -jnp.inf, F32)
        l_ref[...] = jnp.zeros(l_ref.shape, F32)
        acc_ref[...] = jnp.zeros(acc_ref.shape, F32)

    thr = thr_ref[...]

    def head_rows(ref, h):
        return ref[pl.ds(h, page, stride=n_heads), :].astype(BF16)

    def attend(k_refs, v_refs, sc, sc_thr):
        n = len(k_refs)
        s = jnp.stack([
            lax.dot_general(q_ref[h], jnp.concatenate([head_rows(r, h) for r in k_refs], axis=0), NT_DIMS,
                            preferred_element_type=F32)
            for h in range(n_heads)])
        s = jnp.where(sc[None] >= sc_thr[None], s, NEG)
        m_prev = m_ref[...]
        m_new = jnp.maximum(m_prev, jnp.max(s, axis=2, keepdims=True))
        alpha = jnp.exp2(m_prev - m_new)
        pr = jnp.exp2(s - jnp.concatenate([m_new] * n, axis=2))
        l_ref[...] = alpha * l_ref[...] + jnp.sum(pr, axis=2, keepdims=True)
        m_ref[...] = m_new
        pb = pr.astype(BF16)
        for h in range(n_heads):
            vh = jnp.concatenate([head_rows(r, h) for r in v_refs], axis=0)
            acc_ref[h] = alpha[h] * acc_ref[h] + jnp.dot(pb[h], vh, preferred_element_type=F32)

    attend(ck_refs, cv_refs, jnp.concatenate([sc_ref[g] for g in range(group)], axis=1),
           jnp.concatenate([thr] * group, axis=1))

    @pl.when(p == pl.num_programs(1) - 1)
    def _():
        attend([kn_ref], [vn_ref], scn_ref[...], thr)
        o_ref[...] = acc_ref[...] / l_ref[...]


def attn_sample(page_table, q_ht, cache_k, cache_v, k_new_pad, v_new_pad, sc, thr, *, group=8):
    db, n_pages = page_table.shape
    n_heads, t_new = q_ht.shape[1], q_ht.shape[2]
    prow = cache_k.shape[2]
    page = prow // n_heads
    assert n_pages % group == 0
    kern = functools.partial(_attn_sample_kernel, group=group, page=page, n_heads=n_heads)
    cache_specs = [pl.BlockSpec((None, None, prow, HEAD_DIM), functools.partial(_page_map, g=g, group=group))
                   for g in range(group)]
    new_spec = pl.BlockSpec((None, prow, HEAD_DIM), lambda b, p, pt: (b, 0, 0))
    head_spec = pl.BlockSpec((None, n_heads, t_new, HEAD_DIM), lambda b, p, pt: (b, 0, 0, 0))
    grid_spec = pltpu.PrefetchScalarGridSpec(
        num_scalar_prefetch=1,
        grid=(db, n_pages // group),
        in_specs=[
            head_spec,
            *cache_specs, *cache_specs, new_spec, new_spec,
            pl.BlockSpec((None, group, t_new, LANES), lambda b, p, pt: (b, p, 0, 0)),
            pl.BlockSpec((None, None, t_new, LANES), lambda b, p, pt: (b, n_pages, 0, 0)),
            pl.BlockSpec((None, t_new, LANES), lambda b, p, pt: (b, 0, 0)),
        ],
        out_specs=head_spec,
        scratch_shapes=[
            pltpu.VMEM((n_heads, t_new, LANES), F32),
            pltpu.VMEM((n_heads, t_new, LANES), F32),
            pltpu.VMEM((n_heads, t_new, HEAD_DIM), F32),
        ],
    )
    return pl.pallas_call(
        kern,
        grid_spec=grid_spec,
        out_shape=jax.ShapeDtypeStruct((db, n_heads, t_new, HEAD_DIM), F32),
        compiler_params=_cparams(2),
        name="attn_sample",
    )(page_table, q_ht, *([cache_k] * group), *([cache_v] * group), k_new_pad, v_new_pad, sc, sc, thr)


def _outproj_kernel(x_ref, yp_ref, ya_ref, wt_ref, wb_ref, g_ref, x2_ref, h2_ref):
    y = jnp.dot(yp_ref[...], wt_ref[...], preferred_element_type=F32)
    y = y + jnp.dot(ya_ref[...], wb_ref[...], preferred_element_type=F32)
    x2 = x_ref[...] + y
    x2_ref[...] = x2
    h2_ref[...] = _rms(x2, g_ref[...]).astype(h2_ref.dtype)


def outproj(x, yp, ya, w_top, w_bot, g, *, tm):
    m, d = x.shape
    kp, ka = yp.shape[1], ya.shape[1]
    return pl.pallas_call(
        _outproj_kernel,
        grid=(m // tm,),
        in_specs=[
            pl.BlockSpec((tm, d), lambda i: (i, 0)),
            pl.BlockSpec((tm, kp), lambda i: (i, 0)),
            pl.BlockSpec((tm, ka), lambda i: (i, 0)),
            pl.BlockSpec((kp, d), lambda i: (0, 0)),
            pl.BlockSpec((ka, d), lambda i: (0, 0)),
            pl.BlockSpec((1, d), lambda i: (0, 0)),
        ],
        out_specs=[pl.BlockSpec((tm, d), lambda i: (i, 0)), pl.BlockSpec((tm, d), lambda i: (i, 0))],
        out_shape=[jax.ShapeDtypeStruct((m, d), F32), jax.ShapeDtypeStruct((m, d), BF16)],
        compiler_params=_cparams(1),
        name="outproj",
    )(x, yp, ya, w_top, w_bot, g.reshape(1, d))


def _ffn_kernel(h2_ref, wg_ref, wu_ref, wd_ref, x2_ref, g_ref, o_ref, acc_ref):
    f = pl.program_id(1)

    @pl.when(f == 0)
    def _():
        acc_ref[...] = jnp.zeros(acc_ref.shape, F32)

    h2 = h2_ref[...]
    gate = jnp.dot(h2, wg_ref[...], preferred_element_type=F32)
    up = jnp.dot(h2, wu_ref[...], preferred_element_type=F32)
    a = (gate * jax.nn.sigmoid(gate) * up).astype(BF16)
    acc_ref[...] += jnp.dot(a, wd_ref[...], preferred_element_type=F32)

    @pl.when(f == pl.num_programs(1) - 1)
    def _():
        o_ref[...] = _rms(x2_ref[...] + acc_ref[...], g_ref[...])


def ffn(h2, wg, wu, wd, x2, g, *, tm, tf):
    m, d = h2.shape
    dff = wg.shape[1]
    return pl.pallas_call(
        _ffn_kernel,
        grid=(m // tm, dff // tf),
        in_specs=[
            pl.BlockSpec((tm, d), lambda i, f: (i, 0)),
            pl.BlockSpec((d, tf), lambda i, f: (0, f)),
            pl.BlockSpec((d, tf), lambda i, f: (0, f)),
            pl.BlockSpec((tf, d), lambda i, f: (f, 0)),
            pl.BlockSpec((tm, d), lambda i, f: (i, 0)),
            pl.BlockSpec((1, d), lambda i, f: (0, 0)),
        ],
        out_specs=pl.BlockSpec((tm, d), lambda i, f: (i, 0)),
        out_shape=jax.ShapeDtypeStruct((m, d), F32),
        scratch_shapes=[pltpu.VMEM((tm, d), F32)],
        compiler_params=_cparams(2),
        name="ffn",
    )(h2, wg, wu, wd, x2, g.reshape(1, d))


def _rope_tables(pos):
    half = HEAD_DIM // 2
    inv = ROPE_THETA ** (-jnp.arange(half, dtype=F32) / half)
    ang = pos.astype(F32)[:, None] * inv[None, :]
    cos, sin = jnp.cos(ang), jnp.sin(ang)
    return jnp.concatenate([cos, cos], axis=-1), jnp.concatenate([-sin, sin], axis=-1)


def _mix_inputs(x2d, pos_rows, g_mix, w_in16, w_tail, *, tm):
    cos, sin = _rope_tables(pos_rows)
    return mixproj(x2d, g_mix, w_in16, w_tail, cos, sin, tm=tm)


def kernel(x_prompt, x_sample, cache_k, cache_v, cache_idx_k, state_pool, page_table, g_mix, w_in, w_pool,
           s_pool, w_out, g_ffn, w_gate, w_up, w_down, g_final):
    B, S, D = x_prompt.shape
    DB, T, _ = x_sample.shape
    depth = w_in.shape[0]
    assert depth == 1
    n_pages = page_table.shape[1]
    page = cache_k.shape[2]
    n_heads = cache_k.shape[3]
    aw = n_heads * HEAD_DIM
    pw = w_pool.shape[1] * w_pool.shape[2]
    past = n_pages * page
    l = 0

    wide = pw + 3 * aw + IDX_HEADS * IDX_DIM
    assert (pw, aw, wide) == (PROJ_TN, PROJ_TN, 6 * PROJ_TN) and w_in.shape[2] == wide + IDX_DIM + IDX_HEADS
    w_in16 = w_in[l].astype(BF16)
    w_tail = jnp.pad(w_in16[:, wide:], ((0, 0), (0, LANES - IDX_HEADS)))
    w_pool16 = w_pool[l].astype(BF16)
    w_out16 = w_out[l].astype(BF16)
    w_top, w_bot = w_out16[:pw], w_out16[pw:]
    wg16, wu16, wd16 = w_gate[l].astype(BF16), w_up[l].astype(BF16), w_down[l].astype(BF16)

    xp2 = x_prompt.reshape(B * S, D)
    pos_p = jnp.tile(jnp.arange(S), B)
    up, q, k32, k16, v32, v16, qi, ki32, ki16, wi = _mix_inputs(xp2, pos_p, g_mix[l], w_in16, w_tail, tm=512)
    up3 = up.reshape(B, S, pw)
    yp = pool_mixer(up3, jnp.zeros((B, HALO, pw), F32), w_pool16, s_pool[l], tm=512, pos0=0)
    tk = ATTN_TK
    wit = wi.reshape(B, S, LANES)[:, :, :IDX_HEADS].transpose(0, 2, 1)
    vt = v16.reshape(B, S // tk, tk, aw).transpose(0, 1, 3, 2)
    ya = attn_prompt(q.reshape(B, S, aw), qi.reshape(B, S, -1), wit, k16.reshape(B, S, aw), vt,
                     ki16.reshape(B, S, IDX_DIM), topk=min(TOPK_MAX, S // 4))
    x2, h2 = outproj(xp2, yp.reshape(B * S, pw), ya.reshape(B * S, aw), w_top, w_bot, g_ffn[l], tm=512)
    y_prompt = ffn(h2, wg16, wu16, wd16, x2, g_final, tm=512, tf=512).reshape(B, S, D)

    k_prompt = k32.reshape(1, B, S, n_heads, HEAD_DIM)
    v_prompt = v32.reshape(1, B, S, n_heads, HEAD_DIM)
    idx_k_prompt = ki32.reshape(1, B, S, IDX_DIM)
    pool_prompt = up3[:, S - POOL_STATE:, :][None]

    M = DB * T
    xs2 = x_sample.reshape(M, D)
    pos_s = jnp.tile(past + jnp.arange(T), DB)
    ups, qs, ks32, ks16, vs32, vs16, qis, kis32, kis16, wis = _mix_inputs(xs2, pos_s, g_mix[l], w_in16, w_tail, tm=M)
    ups3 = ups.reshape(DB, T, pw)
    prefix16 = jnp.pad(state_pool[l], ((0, 0), (HALO - POOL_STATE, 0), (0, 0)))
    yps = pool_mixer(ups3, prefix16, w_pool16, s_pool[l], tm=T, pos0=past)

    qi_ht = qis.reshape(DB, T, IDX_HEADS, IDX_DIM).transpose(0, 2, 1, 3).reshape(DB, IDX_HEADS * T, IDX_DIM)
    w_ht = wis.reshape(DB, T, LANES)[:, :, :IDX_HEADS].transpose(0, 2, 1).reshape(DB, IDX_HEADS * T, 1)
    w_ht = jnp.broadcast_to(w_ht * (IDX_DIM ** -0.5), (DB, IDX_HEADS * T, LANES))
    pad_rows = lambda a: jnp.pad(a.reshape(DB, T, -1), ((0, 0), (0, page - T), (0, 0)))
    sc, thr = idx_sample(page_table, qi_ht, w_ht, cache_idx_k, pad_rows(kis16), t_new=T,
                         topk=min(TOPK_MAX, (past + T) // 4))

    n_phys = cache_k.shape[1]
    key_head_rows = lambda a: pad_rows(a).reshape(DB, page * n_heads, HEAD_DIM)
    q_ht = qs.reshape(DB, T, n_heads, HEAD_DIM).transpose(0, 2, 1, 3)
    o_ht = attn_sample(page_table, q_ht, cache_k.reshape(depth, n_phys, page * n_heads, HEAD_DIM),
                       cache_v.reshape(depth, n_phys, page * n_heads, HEAD_DIM),
                       key_head_rows(ks32), key_head_rows(vs32), sc, thr)
    yas = o_ht.transpose(0, 2, 1, 3).reshape(M, aw).astype(BF16)

    x2s, h2s = outproj(xs2, yps.reshape(M, pw), yas, w_top, w_bot, g_ffn[l], tm=M)
    y_sample = ffn(h2s, wg16, wu16, wd16, x2s, g_final, tm=M, tf=512).reshape(DB, T, D)

    k_sample = ks32.reshape(1, DB, T, n_heads, HEAD_DIM)
    v_sample = vs32.reshape(1, DB, T, n_heads, HEAD_DIM)
    idx_k_sample = kis32.reshape(1, DB, T, IDX_DIM)
    pool_sample = jnp.concatenate([state_pool[l][:, T:, :], ups3], axis=1)[None]

    return (y_prompt, y_sample, k_prompt, v_prompt, idx_k_prompt, pool_prompt,
            k_sample, v_sample, idx_k_sample, pool_sample)
```

```python
import functools

import jax
import jax.numpy as jnp
from jax import lax
from jax.experimental import pallas as pl
from jax.experimental.pallas import tpu as pltpu

F32 = jnp.float32
BF16 = jnp.bfloat16

LANES = 128
HEAD_DIM = 128
IDX_DIM = 128
IDX_HEADS = 16
POOL_GROUPS = 4
POOL_WINDOWS = (2, 4, 8, 16)
POOL_STATE = max(POOL_WINDOWS) - 1
HALO = 16
TOPK_MAX = 256
ATTN_TQ = 256
ATTN_TK = 128
CHUNK_TRIP_FACTOR = 2
ROPE_THETA = 10000.0
EPS = 1e-6
NEG = -1e30
MAX_SEARCH_STEPS = 400
LOG2E = 1.4426950408889634
VMEM_LIMIT = 56 * 1024 * 1024

NT_DIMS = (((1,), (1,)), ((), ()))


def _cparams(n_grid):
    return pltpu.CompilerParams(dimension_semantics=("arbitrary",) * n_grid, vmem_limit_bytes=VMEM_LIMIT)


def _rms(x, g):
    return x * lax.rsqrt(jnp.mean(x * x, axis=-1, keepdims=True) + EPS) * g


PROJ_TN = 1024


def _rope(y, cos, sin):
    return [y[:, g * HEAD_DIM:(g + 1) * HEAD_DIM] * cos
            + pltpu.roll(y[:, g * HEAD_DIM:(g + 1) * HEAD_DIM], HEAD_DIM // 2, axis=1) * sin
            for g in range(y.shape[1] // HEAD_DIM)]


def _mixproj_kernel(x_ref, g_ref, w_ref, wt_ref, cos_ref, sin_ref,
                    up_ref, q_ref, k32_ref, k16_ref, v32_ref, qi_ref, ki32_ref, ki16_ref, wi_ref, *rest):
    maybe_vt_ref, h_ref = rest if len(rest) == 2 else (None, rest[0])
    j = pl.program_id(1)

    def heads_to(pieces, refs, col0=0):
        for g, r in enumerate(pieces):
            for o_ref in refs:
                o_ref[:, col0 + g * HEAD_DIM:col0 + (g + 1) * HEAD_DIM] = r.astype(o_ref.dtype)

    @pl.when(j == 0)
    def _():
        h_ref[...] = _rms(x_ref[...], g_ref[...]).astype(h_ref.dtype)
        t = jnp.dot(h_ref[...], wt_ref[...], preferred_element_type=F32)
        heads_to(_rope(t[:, :IDX_DIM], cos_ref[...], sin_ref[...]), (ki32_ref, ki16_ref))
        wi_ref[...] = t[:, IDX_DIM:] * (IDX_HEADS ** -0.5)

    y = jnp.dot(h_ref[...], w_ref[...], preferred_element_type=F32)

    @pl.when(j == 0)
    def _():
        up_ref[...] = y

    @pl.when(j == 1)
    def _():
        heads_to(_rope(y * (HEAD_DIM ** -0.5 * LOG2E), cos_ref[...], sin_ref[...]), (q_ref,))

    @pl.when(j == 2)
    def _():
        heads_to(_rope(y, cos_ref[...], sin_ref[...]), (k32_ref, k16_ref))

    @pl.when(j == 3)
    def _():
        v32_ref[...] = y
        if maybe_vt_ref is not None:
            n_chunks, _, ck = maybe_vt_ref.shape
            for c in range(n_chunks):
                for g in range(y.shape[1] // HEAD_DIM):
                    hs = slice(g * HEAD_DIM, (g + 1) * HEAD_DIM)
                    maybe_vt_ref[c, hs, :] = y[c * ck:(c + 1) * ck, hs].T.astype(maybe_vt_ref.dtype)

    for t_qi in range(2):
        @pl.when(j == 4 + t_qi)
        def _():
            heads_to(_rope(y, cos_ref[...], sin_ref[...]), (qi_ref,), col0=t_qi * PROJ_TN)


def mixproj(x, g, w_in16, w_tail, cos, sin, *, tm, vt_chunk=None):
    m, d = x.shape
    tn = PROJ_TN
    n_tiles = 6
    row = lambda width: pl.BlockSpec((tm, width), lambda i, j: (i, 0))
    widths = (tn, tn, tn, tn, tn, 2 * tn, IDX_DIM, IDX_DIM, LANES)
    dtypes = (F32, BF16, F32, BF16, F32, BF16, F32, BF16, F32)
    out_specs = [row(w) for w in widths]
    out_shape = [jax.ShapeDtypeStruct((m, w), dt) for w, dt in zip(widths, dtypes)]
    if vt_chunk is not None:
        out_specs.append(pl.BlockSpec((tm // vt_chunk, tn, vt_chunk), lambda i, j: (i, 0, 0)))
        out_shape.append(jax.ShapeDtypeStruct((m // vt_chunk, tn, vt_chunk), BF16))
    return pl.pallas_call(
        _mixproj_kernel,
        grid=(m // tm, n_tiles),
        in_specs=[
            row(d),
            pl.BlockSpec((1, d), lambda i, j: (0, 0)),
            pl.BlockSpec((d, tn), lambda i, j: (0, j)),
            pl.BlockSpec(w_tail.shape, lambda i, j: (0, 0)),
            row(HEAD_DIM), row(HEAD_DIM),
        ],
        out_specs=out_specs,
        out_shape=out_shape,
        scratch_shapes=[pltpu.VMEM((tm, d), BF16)],
        compiler_params=_cparams(2),
        name="mixproj",
    )(x, g.reshape(1, d), w_in16, w_tail, cos, sin)


def _pool_kernel(*refs, tm, pos0, has_prev):
    if has_prev:
        up_ref, prev_ref, prefix_ref, w_ref, s_ref, o_ref, ext_ref = refs
    else:
        up_ref, prefix_ref, w_ref, s_ref, o_ref, ext_ref = refs
    i = pl.program_id(1)
    if has_prev:
        halo = jnp.where(i == 0, prefix_ref[...], prev_ref[...])
    else:
        halo = prefix_ref[...]
    ext_ref[0:HALO, :] = halo
    ext_ref[HALO:HALO + tm, :] = up_ref[...]
    pos = pos0 + i * tm + lax.broadcasted_iota(jnp.int32, (tm, 1), 0)
    gw = up_ref.shape[1] // POOL_GROUPS
    for g, w in enumerate(POOL_WINDOWS):
        sl = slice(g * gw, (g + 1) * gw)
        cur = ext_ref[HALO:HALO + tm, sl]
        s = cur
        for j in range(1, w):
            s = s + ext_ref[HALO - j:HALO - j + tm, sl]
        cnt = jnp.minimum(w, pos + 1).astype(F32)
        d = s / cnt - cur
        y = jnp.dot(d.astype(BF16), w_ref[g], preferred_element_type=F32) * s_ref[:, sl]
        o_ref[:, sl] = y.astype(o_ref.dtype)


def pool_mixer(up, prefix16, w_pool, s_pool, *, tm, pos0):
    b, t, wd = up.shape
    has_prev = t > tm
    gw = wd // POOL_GROUPS
    in_specs = [pl.BlockSpec((None, tm, wd), lambda bi, i: (bi, i, 0))]
    args = [up]
    if has_prev:
        r = tm // HALO
        in_specs.append(pl.BlockSpec((None, HALO, wd), lambda bi, i: (bi, jnp.maximum(i * r - 1, 0), 0)))
        args.append(up)
    in_specs += [
        pl.BlockSpec((None, HALO, wd), lambda bi, i: (bi, 0, 0)),
        pl.BlockSpec((POOL_GROUPS, gw, gw), lambda bi, i: (0, 0, 0)),
        pl.BlockSpec((1, wd), lambda bi, i: (0, 0)),
    ]
    args += [prefix16, w_pool, s_pool.reshape(1, wd)]
    return pl.pallas_call(
        functools.partial(_pool_kernel, tm=tm, pos0=pos0, has_prev=has_prev),
        grid=(b, t // tm),
        in_specs=in_specs,
        out_specs=pl.BlockSpec((None, tm, wd), lambda bi, i: (bi, i, 0)),
        out_shape=jax.ShapeDtypeStruct((b, t, wd), BF16),
        scratch_shapes=[pltpu.VMEM((HALO + tm, wd), F32)],
        compiler_params=_cparams(2),
        name="pool_mixer",
    )(*args)


def _select_threshold(count_ge, rmin, rmax, n_valid, kq):
    c_hi = count_ge(rmax)
    top_tied = c_hi >= kq
    lo0 = jnp.where(top_tied, rmax, rmin)
    cl0 = jnp.where(top_tied, c_hi, n_valid)

    def not_done(cl, stuck):
        return jnp.max(jnp.where((cl == kq) | (stuck > 0.0), 0.0, 1.0))

    def cond(st):
        it, flag = st[0], st[1]
        return jnp.logical_and(it < MAX_SEARCH_STEPS, flag > 0.0)

    def body(st):
        it, _, lo, hi, cl, ch, stuck = st
        halve = (it % 2) == 1
        frac = jnp.where(halve, 0.5, (cl - kq - 0.5) / jnp.maximum(cl - ch, 1.0))
        mid = jnp.clip(lo + frac * (hi - lo), lo, hi)
        stuck = jnp.where(halve, jnp.where((mid <= lo) | (mid >= hi), 1.0, 0.0), stuck)
        c = count_ge(mid)
        ge = c >= kq
        lo2 = jnp.where(ge, mid, lo)
        cl2 = jnp.where(ge, c, cl)
        hi2 = jnp.where(ge, hi, mid)
        ch2 = jnp.where(ge, ch, c)
        return it + 1, not_done(cl2, stuck), lo2, hi2, cl2, ch2, stuck

    stuck0 = jnp.where(lo0 >= rmax, 1.0, 0.0)
    st = (jnp.int32(0), not_done(cl0, stuck0), lo0, rmax, cl0, c_hi, stuck0)
    st = lax.while_loop(cond, body, st)
    return st[2]


def _attn_prompt_kernel(q_ref, qi_ref, wit_ref, k_ref, vt_ref, ki_ref, o_ref,
                        sc_ref, m_ref, l_ref, acc_ref, *, tk, topk, n_heads):
    i = pl.program_id(1)
    tq = q_ref.shape[0]
    nc = ((i + 1) * tq + tk - 1) // tk
    qpos = i * tq + lax.broadcasted_iota(jnp.int32, (1, tq), 1)
    key_row = lax.broadcasted_iota(jnp.int32, (tk, tq), 0)
    w_rows = wit_ref[...] * (IDX_DIM ** -0.5)

    def idx_chunk(c, carry):
        off = pl.multiple_of(c * tk, tk)
        kic = ki_ref[pl.ds(off, tk), :]
        acc = jnp.zeros((tk, tq), F32)
        for h in range(IDX_HEADS):
            s = lax.dot_general(kic, qi_ref[:, h * IDX_DIM:(h + 1) * IDX_DIM], NT_DIMS,
                                preferred_element_type=F32)
            acc = acc + jnp.maximum(s, 0.0) * w_rows[h:h + 1, :]
        sc_ref[pl.ds(off, tk), :] = jnp.where(off + key_row <= qpos, acc, -jnp.inf)
        return carry

    def chunk_loop(n_chunks, chunk_fn):
        base = tq // tk
        per_trip = CHUNK_TRIP_FACTOR * base

        def run(first, count):
            for u in range(count):
                chunk_fn(first + u, 0)

        def body(t, carry):
            run(t * per_trip, per_trip)
            return carry
        lax.fori_loop(0, n_chunks // per_trip, body, 0)
        for r in range(1, CHUNK_TRIP_FACTOR):
            @pl.when(n_chunks % per_trip == r * base)
            def _():
                run((n_chunks // per_trip) * per_trip, r * base)

    chunk_loop(nc, idx_chunk)

    lanes_par = 4

    def reduce_keys(fn, init, red):
        def body(c, a):
            blk = sc_ref[pl.ds(pl.multiple_of(c * tk, tk), tk), :]
            return fn(a, blk.reshape(lanes_par, tk // (8 * lanes_par), 8, tq))
        a = lax.fori_loop(0, nc, body, jnp.full((lanes_par, 8, tq), init, F32))
        return red(red(a, axis=0), axis=0, keepdims=True)

    def count_ge(t):
        return reduce_keys(lambda a, blk: a + jnp.sum(jnp.where(blk >= t, 1.0, 0.0), axis=1), 0.0, jnp.sum)

    rmax = reduce_keys(lambda a, blk: jnp.maximum(a, jnp.max(blk, axis=1)), -jnp.inf, jnp.max)
    rmin = reduce_keys(lambda a, blk: jnp.minimum(a, jnp.min(jnp.where(blk == -jnp.inf, jnp.inf, blk), axis=1)),
                       jnp.inf, jnp.min)
    n_valid = (qpos + 1).astype(F32)
    kq = jnp.minimum(float(topk), n_valid)
    thr = _select_threshold(count_ge, rmin, rmax, n_valid, kq)

    m_ref[...] = jnp.full(m_ref.shape, -jnp.inf, F32)
    l_ref[...] = jnp.zeros(l_ref.shape, F32)
    acc_ref[...] = jnp.zeros(acc_ref.shape, F32)

    def att_chunk(c, carry):
        off = pl.multiple_of(c * tk, tk)
        mask = sc_ref[pl.ds(off, tk), :] >= thr
        for h in range(n_heads):
            hs = slice(h * HEAD_DIM, (h + 1) * HEAD_DIM)
            s = lax.dot_general(k_ref[pl.ds(off, tk), hs], q_ref[:, hs], NT_DIMS,
                                preferred_element_type=F32)
            s = jnp.where(mask, s, NEG)
            m_prev = m_ref[h]
            m_new = jnp.maximum(m_prev, jnp.max(s, axis=0, keepdims=True))
            alpha = jnp.exp2(m_prev - m_new)
            p = jnp.exp2(s - m_new)
            l_ref[h] = alpha * l_ref[h] + jnp.sum(p, axis=0, keepdims=True)
            m_ref[h] = m_new
            pv = jnp.dot(vt_ref[c, hs, :], p.astype(BF16), preferred_element_type=F32)
            acc_ref[h] = alpha * acc_ref[h] + pv
        return carry

    chunk_loop(nc, att_chunk)

    for h in range(n_heads):
        hs = slice(h * HEAD_DIM, (h + 1) * HEAD_DIM)
        o_ref[:, hs] = (acc_ref[h] / l_ref[h]).T.astype(o_ref.dtype)


def attn_prompt(q, qi, wit, k, vt, ki, *, topk):
    b, s, aw = q.shape
    n_heads = aw // HEAD_DIM
    tq = ATTN_TQ
    tk = vt.shape[3]
    kern = functools.partial(_attn_prompt_kernel, tk=ATTN_TK, topk=topk, n_heads=n_heads)
    return pl.pallas_call(
        kern,
        grid=(b, s // tq),
        in_specs=[
            pl.BlockSpec((None, tq, aw), lambda bi, i: (bi, i, 0)),
            pl.BlockSpec((None, tq, qi.shape[2]), lambda bi, i: (bi, i, 0)),
            pl.BlockSpec((None, IDX_HEADS, tq), lambda bi, i: (bi, 0, i)),
            pl.BlockSpec((None, s, aw), lambda bi, i: (bi, 0, 0)),
            pl.BlockSpec((None, s // tk, aw, tk), lambda bi, i: (bi, 0, 0, 0)),
            pl.BlockSpec((None, s, IDX_DIM), lambda bi, i: (bi, 0, 0)),
        ],
        out_specs=pl.BlockSpec((None, tq, aw), lambda bi, i: (bi, i, 0)),
        out_shape=jax.ShapeDtypeStruct((b, s, aw), BF16),
        scratch_shapes=[
            pltpu.VMEM((s, tq), F32),
            pltpu.VMEM((n_heads, 1, tq), F32),
            pltpu.VMEM((n_heads, 1, tq), F32),
            pltpu.VMEM((n_heads, HEAD_DIM, tq), F32),
        ],
        compiler_params=_cparams(2),
        name="attn_prompt",
    )(q, qi, wit, k, vt, ki)


def _idx_sample_kernel(pt_ref, qi_ref, w_ref, *rest, n_pages, group, t_new, topk):
    cik_refs = rest[:group]
    kin_ref, sc_ref, thr_ref = rest[group:]
    b, p = pl.program_id(0), pl.program_id(1)
    db = sc_ref.shape[0]
    rows = qi_ref.shape[0]

    def scores(keys_bf16):
        s = lax.dot_general(qi_ref[...], keys_bf16, NT_DIMS, preferred_element_type=F32)
        r = jnp.maximum(s, 0.0) * w_ref[...]
        return jnp.sum(r.reshape(rows // t_new, t_new, LANES), axis=0)

    for g in range(group):
        sc_ref[b, p * group + g] = scores(cik_refs[g][...].astype(BF16))

    @pl.when(p == 0)
    def _():
        s = scores(kin_ref[...])
        tok = lax.broadcasted_iota(jnp.int32, (t_new, LANES), 0)
        lane = lax.broadcasted_iota(jnp.int32, (t_new, LANES), 1)
        sc_ref[b, n_pages] = jnp.where((lane <= tok) & (lane < t_new), s, -jnp.inf)

    @pl.when((b == db - 1) & (p == pl.num_programs(1) - 1))
    def _():
        def reduce_keys(fn, red):
            return red(red(fn(sc_ref[...]), axis=1), axis=2, keepdims=True)

        def count_ge(t):
            return reduce_keys(lambda x: jnp.where(x >= t[:, None], 1.0, 0.0), jnp.sum)

        rmax = reduce_keys(lambda x: x, jnp.max)
        rmin = reduce_keys(lambda x: jnp.where(x == -jnp.inf, jnp.inf, x), jnp.min)
        tokc = lax.broadcasted_iota(jnp.int32, (db, t_new, 1), 1)
        n_valid = (n_pages * LANES + tokc + 1).astype(F32)
        kq = jnp.minimum(float(topk), n_valid)
        thr = _select_threshold(count_ge, rmin, rmax, n_valid, kq)
        thr_ref[...] = jnp.broadcast_to(thr, thr_ref.shape)


def _page_map(b, p, pt, *, g, group):
    return (0, pt[b, p * group + g], 0, 0)


def idx_sample(page_table, qi_ht, w_ht, cache_idx_k, ki_new_pad, *, t_new, topk, group=16):
    db, n_pages = page_table.shape
    rows = qi_ht.shape[1]
    page = cache_idx_k.shape[2]
    assert n_pages % group == 0
    kern = functools.partial(_idx_sample_kernel, n_pages=n_pages, group=group, t_new=t_new, topk=topk)
    page_specs = [pl.BlockSpec((None, None, page, IDX_DIM), functools.partial(_page_map, g=g, group=group))
                  for g in range(group)]
    grid_spec = pltpu.PrefetchScalarGridSpec(
        num_scalar_prefetch=1,
        grid=(db, n_pages // group),
        in_specs=[
            pl.BlockSpec((None, rows, IDX_DIM), lambda b, p, pt: (b, 0, 0)),
            pl.BlockSpec((None, rows, LANES), lambda b, p, pt: (b, 0, 0)),
            *page_specs,
            pl.BlockSpec((None, page, IDX_DIM), lambda b, p, pt: (b, 0, 0)),
        ],
        out_specs=[
            pl.BlockSpec((db, n_pages + 1, t_new, LANES), lambda b, p, pt: (0, 0, 0, 0)),
            pl.BlockSpec((db, t_new, LANES), lambda b, p, pt: (0, 0, 0)),
        ],
    )
    return pl.pallas_call(
        kern,
        grid_spec=grid_spec,
        out_shape=[jax.ShapeDtypeStruct((db, n_pages + 1, t_new, LANES), F32),
                   jax.ShapeDtypeStruct((db, t_new, LANES), F32)],
        compiler_params=_cparams(2),
        name="idx_sample",
    )(page_table, qi_ht, w_ht, *([cache_idx_k] * group), ki_new_pad)


def _attn_sample_kernel(pt_ref, q_ref, *rest, group, page, n_heads):
    ck_refs = rest[:group]
    cv_refs = rest[group:2 * group]
    kn_ref, vn_ref, sc_ref, scn_ref, thr_ref, o_ref, m_ref, l_ref, acc_ref = rest[2 * group:]
    p = pl.program_id(1)

    @pl.when(p == 0)
    def _():
        m_ref[...] = jnp.full(m_ref.shape, -jnp.inf, F32)
        l_ref[...] = jnp.zeros(l_ref.shape, F32)
        acc_ref[...] = jnp.zeros(acc_ref.shape, F32)

    thr = thr_ref[...]

    def head_rows(ref, h):
        return ref[pl.ds(h, page, stride=n_heads), :].astype(BF16)

    def attend(k_refs, v_refs, sc, sc_thr):
        n = len(k_refs)
        s = jnp.stack([
            lax.dot_general(q_ref[h], jnp.concatenate([head_rows(r, h) for r in k_refs], axis=0), NT_DIMS,
                            preferred_element_type=F32)
            for h in range(n_heads)])
        s = jnp.where(sc[None] >= sc_thr[None], s, NEG)
        m_prev = m_ref[...]
        m_new = jnp.maximum(m_prev, jnp.max(s, axis=2, keepdims=True))
        alpha = jnp.exp2(m_prev - m_new)
        pr = jnp.exp2(s - jnp.concatenate([m_new] * n, axis=2))
        l_ref[...] = alpha * l_ref[...] + jnp.sum(pr, axis=2, keepdims=True)
        m_ref[...] = m_new
        pb = pr.astype(BF16)
        for h in range(n_heads):
            vh = jnp.concatenate([head_rows(r, h) for r in v_refs], axis=0)
            acc_ref[h] = alpha[h] * acc_ref[h] + jnp.dot(pb[h], vh, preferred_element_type=F32)

    attend(ck_refs, cv_refs, jnp.concatenate([sc_ref[g] for g in range(group)], axis=1),
           jnp.concatenate([thr] * group, axis=1))

    @pl.when(p == pl.num_programs(1) - 1)
    def _():
        attend([kn_ref], [vn_ref], scn_ref[...], thr)
        o_ref[...] = acc_ref[...] / l_ref[...]


def attn_sample(page_table, q_ht, cache_k, cache_v, k_new_pad, v_new_pad, sc, thr, *, group=8):
    db, n_pages = page_table.shape
    n_heads, t_new = q_ht.shape[1], q_ht.shape[2]
    prow = cache_k.shape[2]
    page = prow // n_heads
    assert n_pages % group == 0
    kern = functools.partial(_attn_sample_kernel, group=group, page=page, n_heads=n_heads)
    cache_specs = [pl.BlockSpec((None, None, prow, HEAD_DIM), functools.partial(_page_map, g=g, group=group))
                   for g in range(group)]
    new_spec = pl.BlockSpec((None, prow, HEAD_DIM), lambda b, p, pt: (b, 0, 0))
    head_spec = pl.BlockSpec((None, n_heads, t_new, HEAD_DIM), lambda b, p, pt: (b, 0, 0, 0))
    grid_spec = pltpu.PrefetchScalarGridSpec(
        num_scalar_prefetch=1,
        grid=(db, n_pages // group),
        in_specs=[
            head_spec,
            *cache_specs, *cache_specs, new_spec, new_spec,
            pl.BlockSpec((None, group, t_new, LANES), lambda b, p, pt: (b, p, 0, 0)),
            pl.BlockSpec((None, None, t_new, LANES), lambda b, p, pt: (b, n_pages, 0, 0)),
            pl.BlockSpec((None, t_new, LANES), lambda b, p, pt: (b, 0, 0)),
        ],
        out_specs=head_spec,
        scratch_shapes=[
            pltpu.VMEM((n_heads, t_new, LANES), F32),
            pltpu.VMEM((n_heads, t_new, LANES), F32),
            pltpu.VMEM((n_heads, t_new, HEAD_DIM), F32),
        ],
    )
    return pl.pallas_call(
        kern,
        grid_spec=grid_spec,
        out_shape=jax.ShapeDtypeStruct((db, n_heads, t_new, HEAD_DIM), F32),
        compiler_params=_cparams(2),
        name="attn_sample",
    )(page_table, q_ht, *([cache_k] * group), *([cache_v] * group), k_new_pad, v_new_pad, sc, sc, thr)


def _outproj_kernel(x_ref, yp_ref, ya_ref, wt_ref, wb_ref, g_ref, x2_ref, h2_ref):
    y = jnp.dot(yp_ref[...], wt_ref[...], preferred_element_type=F32)
    y = y + jnp.dot(ya_ref[...], wb_ref[...], preferred_element_type=F32)
    x2 = x_ref[...] + y
    x2_ref[...] = x2
    h2_ref[...] = _rms(x2, g_ref[...]).astype(h2_ref.dtype)


def outproj(x, yp, ya, w_top, w_bot, g, *, tm):
    m, d = x.shape
    kp, ka = yp.shape[1], ya.shape[1]
    return pl.pallas_call(
        _outproj_kernel,
        grid=(m // tm,),
        in_specs=[
            pl.BlockSpec((tm, d), lambda i: (i, 0)),
            pl.BlockSpec((tm, kp), lambda i: (i, 0)),
            pl.BlockSpec((tm, ka), lambda i: (i, 0)),
            pl.BlockSpec((kp, d), lambda i: (0, 0)),
            pl.BlockSpec((ka, d), lambda i: (0, 0)),
            pl.BlockSpec((1, d), lambda i: (0, 0)),
        ],
        out_specs=[pl.BlockSpec((tm, d), lambda i: (i, 0)), pl.BlockSpec((tm, d), lambda i: (i, 0))],
        out_shape=[jax.ShapeDtypeStruct((m, d), F32), jax.ShapeDtypeStruct((m, d), BF16)],
        compiler_params=_cparams(1),
        name="outproj",
    )(x, yp, ya, w_top, w_bot, g.reshape(1, d))


def _ffn_kernel(h2_ref, wg_ref, wu_ref, wd_ref, x2_ref, g_ref, o_ref, acc_ref):
    f = pl.program_id(1)

    @pl.when(f == 0)
    def _():
        acc_ref[...] = jnp.zeros(acc_ref.shape, F32)

    h2 = h2_ref[...]
    gate = jnp.dot(h2, wg_ref[...], preferred_element_type=F32)
    up = jnp.dot(h2, wu_ref[...], preferred_element_type=F32)
    a = (gate * jax.nn.sigmoid(gate) * up).astype(BF16)
    acc_ref[...] += jnp.dot(a, wd_ref[...], preferred_element_type=F32)

    @pl.when(f == pl.num_programs(1) - 1)
    def _():
        o_ref[...] = _rms(x2_ref[...] + acc_ref[...], g_ref[...])


def ffn(h2, wg, wu, wd, x2, g, *, tm, tf):
    m, d = h2.shape
    dff = wg.shape[1]
    return pl.pallas_call(
        _ffn_kernel,
        grid=(m // tm, dff // tf),
        in_specs=[
            pl.BlockSpec((tm, d), lambda i, f: (i, 0)),
            pl.BlockSpec((d, tf), lambda i, f: (0, f)),
            pl.BlockSpec((d, tf), lambda i, f: (0, f)),
            pl.BlockSpec((tf, d), lambda i, f: (f, 0)),
            pl.BlockSpec((tm, d), lambda i, f: (i, 0)),
            pl.BlockSpec((1, d), lambda i, f: (0, 0)),
        ],
        out_specs=pl.BlockSpec((tm, d), lambda i, f: (i, 0)),
        out_shape=jax.ShapeDtypeStruct((m, d), F32),
        scratch_shapes=[pltpu.VMEM((tm, d), F32)],
        compiler_params=_cparams(2),
        name="ffn",
    )(h2, wg, wu, wd, x2, g.reshape(1, d))


def _rope_tables(pos):
    half = HEAD_DIM // 2
    inv = ROPE_THETA ** (-jnp.arange(half, dtype=F32) / half)
    ang = pos.astype(F32)[:, None] * inv[None, :]
    cos, sin = jnp.cos(ang), jnp.sin(ang)
    return jnp.concatenate([cos, cos], axis=-1), jnp.concatenate([-sin, sin], axis=-1)


def _mix_inputs(x2d, pos_rows, g_mix, w_in16, w_tail, *, tm, vt_chunk=None):
    cos, sin = _rope_tables(pos_rows)
    return mixproj(x2d, g_mix, w_in16, w_tail, cos, sin, tm=tm, vt_chunk=vt_chunk)


def kernel(x_prompt, x_sample, cache_k, cache_v, cache_idx_k, state_pool, page_table, g_mix, w_in, w_pool,
           s_pool, w_out, g_ffn, w_gate, w_up, w_down, g_final):
    B, S, D = x_prompt.shape
    DB, T, _ = x_sample.shape
    depth = w_in.shape[0]
    assert depth == 1
    n_pages = page_table.shape[1]
    page = cache_k.shape[2]
    n_heads = cache_k.shape[3]
    aw = n_heads * HEAD_DIM
    pw = w_pool.shape[1] * w_pool.shape[2]
    past = n_pages * page
    l = 0

    wide = pw + 3 * aw + IDX_HEADS * IDX_DIM
    assert (pw, aw, wide) == (PROJ_TN, PROJ_TN, 6 * PROJ_TN) and w_in.shape[2] == wide + IDX_DIM + IDX_HEADS
    w_in16 = w_in[l].astype(BF16)
    w_tail = jnp.pad(w_in16[:, wide:], ((0, 0), (0, LANES - IDX_HEADS)))
    w_pool16 = w_pool[l].astype(BF16)
    w_out16 = w_out[l].astype(BF16)
    w_top, w_bot = w_out16[:pw], w_out16[pw:]
    wg16, wu16, wd16 = w_gate[l].astype(BF16), w_up[l].astype(BF16), w_down[l].astype(BF16)

    xp2 = x_prompt.reshape(B * S, D)
    pos_p = jnp.tile(jnp.arange(S), B)
    tk = ATTN_TK
    up, q, k32, k16, v32, qi, ki32, ki16, wi, vt = _mix_inputs(xp2, pos_p, g_mix[l], w_in16, w_tail, tm=512,
                                                               vt_chunk=tk)
    up3 = up.reshape(B, S, pw)
    yp = pool_mixer(up3, jnp.zeros((B, HALO, pw), F32), w_pool16, s_pool[l], tm=512, pos0=0)
    wit = wi.reshape(B, S, LANES)[:, :, :IDX_HEADS].transpose(0, 2, 1)
    ya = attn_prompt(q.reshape(B, S, aw), qi.reshape(B, S, -1), wit, k16.reshape(B, S, aw),
                     vt.reshape(B, S // tk, aw, tk), ki16.reshape(B, S, IDX_DIM), topk=min(TOPK_MAX, S // 4))
    x2, h2 = outproj(xp2, yp.reshape(B * S, pw), ya.reshape(B * S, aw), w_top, w_bot, g_ffn[l], tm=512)
    y_prompt = ffn(h2, wg16, wu16, wd16, x2, g_final, tm=512, tf=512).reshape(B, S, D)

    k_prompt = k32.reshape(1, B, S, n_heads, HEAD_DIM)
    v_prompt = v32.reshape(1, B, S, n_heads, HEAD_DIM)
    idx_k_prompt = ki32.reshape(1, B, S, IDX_DIM)
    pool_prompt = up3[:, S - POOL_STATE:, :][None]

    M = DB * T
    xs2 = x_sample.reshape(M, D)
    pos_s = jnp.tile(past + jnp.arange(T), DB)
    ups, qs, ks32, _, vs32, qis, kis32, kis16, wis = _mix_inputs(xs2, pos_s, g_mix[l], w_in16, w_tail, tm=M)
    ups3 = ups.reshape(DB, T, pw)
    prefix16 = jnp.pad(state_pool[l], ((0, 0), (HALO - POOL_STATE, 0), (0, 0)))
    yps = pool_mixer(ups3, prefix16, w_pool16, s_pool[l], tm=T, pos0=past)

    qi_ht = qis.reshape(DB, T, IDX_HEADS, IDX_DIM).transpose(0, 2, 1, 3).reshape(DB, IDX_HEADS * T, IDX_DIM)
    w_ht = wis.reshape(DB, T, LANES)[:, :, :IDX_HEADS].transpose(0, 2, 1).reshape(DB, IDX_HEADS * T, 1)
    w_ht = jnp.broadcast_to(w_ht * (IDX_DIM ** -0.5), (DB, IDX_HEADS * T, LANES))
    pad_rows = lambda a: jnp.pad(a.reshape(DB, T, -1), ((0, 0), (0, page - T), (0, 0)))
    sc, thr = idx_sample(page_table, qi_ht, w_ht, cache_idx_k, pad_rows(kis16), t_new=T,
                         topk=min(TOPK_MAX, (past + T) // 4))

    n_phys = cache_k.shape[1]
    key_head_rows = lambda a: pad_rows(a).reshape(DB, page * n_heads, HEAD_DIM)
    q_ht = qs.reshape(DB, T, n_heads, HEAD_DIM).transpose(0, 2, 1, 3)
    o_ht = attn_sample(page_table, q_ht, cache_k.reshape(depth, n_phys, page * n_heads, HEAD_DIM),
                       cache_v.reshape(depth, n_phys, page * n_heads, HEAD_DIM),
                       key_head_rows(ks32), key_head_rows(vs32), sc, thr)
    yas = o_ht.transpose(0, 2, 1, 3).reshape(M, aw).astype(BF16)

    x2s, h2s = outproj(xs2, yps.reshape(M, pw), yas, w_top, w_bot, g_ffn[l], tm=M)
    y_sample = ffn(h2s, wg16, wu16, wd16, x2s, g_final, tm=M, tf=512).reshape(DB, T, D)

    k_sample = ks32.reshape(1, DB, T, n_heads, HEAD_DIM)
    v_sample = vs32.reshape(1, DB, T, n_heads, HEAD_DIM)
    idx_k_sample = kis32.reshape(1, DB, T, IDX_DIM)
    pool_sample = jnp.concatenate([state_pool[l][:, T:, :], ups3], axis=1)[None]

    return (y_prompt, y_sample, k_prompt, v_prompt, idx_k_prompt, pool_prompt,
            k_sample, v_sample, idx_k_sample, pool_sample)
```

```python
import functools
import math

import jax
import jax.numpy as jnp
from jax import lax
from jax.experimental import pallas as pl
from jax.experimental.pallas import tpu as pltpu

F32 = jnp.float32
BF16 = jnp.bfloat16

LANES = 128
HEAD_DIM = 128
IDX_DIM = 128
IDX_HEADS = 16
POOL_GROUPS = 4
POOL_WINDOWS = (2, 4, 8, 16)
POOL_STATE = max(POOL_WINDOWS) - 1
HALO = 16
TOPK_MAX = 256
ATTN_TQ = 256
ATTN_TK = 128
CHUNK_TRIP_FACTOR = 4
IDX_PAGES_PER_STEP = 32
KV_PAGES_PER_STEP = 8
ROPE_THETA = 10000.0
EPS = 1e-6
NEG = -1e30
MAX_SEARCH_STEPS = 400
LOG2E = 1.4426950408889634
VMEM_LIMIT = 56 * 1024 * 1024

NT_DIMS = (((1,), (1,)), ((), ()))


def _cparams(n_grid):
    return pltpu.CompilerParams(dimension_semantics=("arbitrary",) * n_grid, vmem_limit_bytes=VMEM_LIMIT)


def _rms(x, g):
    return x * lax.rsqrt(jnp.mean(x * x, axis=-1, keepdims=True) + EPS) * g


PROJ_TN = 1024


def _rope(y, cos, sin):
    return [y[:, g * HEAD_DIM:(g + 1) * HEAD_DIM] * cos
            + pltpu.roll(y[:, g * HEAD_DIM:(g + 1) * HEAD_DIM], HEAD_DIM // 2, axis=1) * sin
            for g in range(y.shape[1] // HEAD_DIM)]


def _mixproj_kernel(x_ref, g_ref, w_ref, wt_ref, cos_ref, sin_ref,
                    up_ref, q_ref, k32_ref, k16_ref, v32_ref, qi_ref, ki32_ref, ki16_ref, wi_ref, *rest):
    maybe_vt_ref, h_ref = rest if len(rest) == 2 else (None, rest[0])
    j = pl.program_id(1)

    def heads_to(pieces, refs, col0=0):
        for g, r in enumerate(pieces):
            for o_ref in refs:
                o_ref[:, col0 + g * HEAD_DIM:col0 + (g + 1) * HEAD_DIM] = r.astype(o_ref.dtype)

    @pl.when(j == 0)
    def _():
        h_ref[...] = _rms(x_ref[...], g_ref[...]).astype(h_ref.dtype)
        t = jnp.dot(h_ref[...], wt_ref[...], preferred_element_type=F32)
        heads_to(_rope(t[:, :IDX_DIM], cos_ref[...], sin_ref[...]), (ki32_ref, ki16_ref))
        wi_ref[...] = t[:, IDX_DIM:] * (IDX_HEADS ** -0.5)

    y = jnp.dot(h_ref[...], w_ref[...], preferred_element_type=F32)

    @pl.when(j == 0)
    def _():
        up_ref[...] = y

    @pl.when(j == 1)
    def _():
        heads_to(_rope(y * (HEAD_DIM ** -0.5 * LOG2E), cos_ref[...], sin_ref[...]), (q_ref,))

    @pl.when(j == 2)
    def _():
        heads_to(_rope(y, cos_ref[...], sin_ref[...]), (k32_ref, k16_ref))

    @pl.when(j == 3)
    def _():
        v32_ref[...] = y
        if maybe_vt_ref is not None:
            n_chunks, _, ck = maybe_vt_ref.shape
            for c in range(n_chunks):
                for g in range(y.shape[1] // HEAD_DIM):
                    hs = slice(g * HEAD_DIM, (g + 1) * HEAD_DIM)
                    maybe_vt_ref[c, hs, :] = y[c * ck:(c + 1) * ck, hs].T.astype(maybe_vt_ref.dtype)

    for t_qi in range(2):
        @pl.when(j == 4 + t_qi)
        def _():
            heads_to(_rope(y, cos_ref[...], sin_ref[...]), (qi_ref,), col0=t_qi * PROJ_TN)


def mixproj(x, g, w_in16, w_tail, cos, sin, *, tm, vt_chunk=None):
    m, d = x.shape
    tn = PROJ_TN
    n_tiles = 6
    row = lambda width: pl.BlockSpec((tm, width), lambda i, j: (i, 0))
    rope_tiles = cos.shape[0] // tm
    rope_spec = pl.BlockSpec((tm, HEAD_DIM), lambda i, j: (i % rope_tiles, 0))
    widths = (tn, tn, tn, tn, tn, 2 * tn, IDX_DIM, IDX_DIM, LANES)
    dtypes = (F32, BF16, F32, BF16, F32, BF16, F32, BF16, F32)
    out_specs = [row(w) for w in widths]
    out_shape = [jax.ShapeDtypeStruct((m, w), dt) for w, dt in zip(widths, dtypes)]
    if vt_chunk is not None:
        out_specs.append(pl.BlockSpec((tm // vt_chunk, tn, vt_chunk), lambda i, j: (i, 0, 0)))
        out_shape.append(jax.ShapeDtypeStruct((m // vt_chunk, tn, vt_chunk), BF16))
    return pl.pallas_call(
        _mixproj_kernel,
        grid=(m // tm, n_tiles),
        in_specs=[
            row(d),
            pl.BlockSpec((1, d), lambda i, j: (0, 0)),
            pl.BlockSpec((d, tn), lambda i, j: (0, j)),
            pl.BlockSpec(w_tail.shape, lambda i, j: (0, 0)),
            rope_spec, rope_spec,
        ],
        out_specs=out_specs,
        out_shape=out_shape,
        scratch_shapes=[pltpu.VMEM((tm, d), BF16)],
        compiler_params=_cparams(2),
        name="mixproj",
    )(x, g.reshape(1, d), w_in16, w_tail, cos, sin)


def _pool_kernel(*refs, tm, pos0, has_prev):
    if has_prev:
        up_ref, prev_ref, prefix_ref, w_ref, s_ref, o_ref, ext_ref = refs
    else:
        up_ref, prefix_ref, w_ref, s_ref, o_ref, ext_ref = refs
    i = pl.program_id(1)
    if has_prev:
        halo = jnp.where(i == 0, prefix_ref[...], prev_ref[...])
    else:
        halo = prefix_ref[...]
    ext_ref[0:HALO, :] = halo
    ext_ref[HALO:HALO + tm, :] = up_ref[...]
    pos = pos0 + i * tm + lax.broadcasted_iota(jnp.int32, (tm, 1), 0)
    gw = up_ref.shape[1] // POOL_GROUPS
    for g, w in enumerate(POOL_WINDOWS):
        sl = slice(g * gw, (g + 1) * gw)
        cur = ext_ref[HALO:HALO + tm, sl]
        s = cur
        for j in range(1, w):
            s = s + ext_ref[HALO - j:HALO - j + tm, sl]
        cnt = jnp.minimum(w, pos + 1).astype(F32)
        d = s / cnt - cur
        y = jnp.dot(d.astype(BF16), w_ref[g], preferred_element_type=F32) * s_ref[:, sl]
        o_ref[:, sl] = y.astype(o_ref.dtype)


def pool_mixer(up, prefix16, w_pool, s_pool, *, tm, pos0):
    b, t, wd = up.shape
    has_prev = t > tm
    gw = wd // POOL_GROUPS
    in_specs = [pl.BlockSpec((None, tm, wd), lambda bi, i: (bi, i, 0))]
    args = [up]
    if has_prev:
        r = tm // HALO
        in_specs.append(pl.BlockSpec((None, HALO, wd), lambda bi, i: (bi, jnp.maximum(i * r - 1, 0), 0)))
        args.append(up)
    in_specs += [
        pl.BlockSpec((None, HALO, wd), lambda bi, i: (bi, 0, 0)),
        pl.BlockSpec((POOL_GROUPS, gw, gw), lambda bi, i: (0, 0, 0)),
        pl.BlockSpec((1, wd), lambda bi, i: (0, 0)),
    ]
    args += [prefix16, w_pool, s_pool.reshape(1, wd)]
    return pl.pallas_call(
        functools.partial(_pool_kernel, tm=tm, pos0=pos0, has_prev=has_prev),
        grid=(b, t // tm),
        in_specs=in_specs,
        out_specs=pl.BlockSpec((None, tm, wd), lambda bi, i: (bi, i, 0)),
        out_shape=jax.ShapeDtypeStruct((b, t, wd), BF16),
        scratch_shapes=[pltpu.VMEM((HALO + tm, wd), F32)],
        compiler_params=_cparams(2),
        name="pool_mixer",
    )(*args)


def _select_threshold(count_ge, rmin, rmax, n_valid, kq):
    c_hi = count_ge(rmax)
    top_tied = c_hi >= kq
    lo0 = jnp.where(top_tied, rmax, rmin)
    cl0 = jnp.where(top_tied, c_hi, n_valid)

    def not_done(cl, stuck):
        return jnp.max(jnp.where((cl == kq) | (stuck > 0.0), 0.0, 1.0))

    def cond(st):
        it, flag = st[0], st[1]
        return jnp.logical_and(it < MAX_SEARCH_STEPS, flag > 0.0)

    def body(st):
        it, _, lo, hi, cl, ch, stuck = st
        halve = (it % 2) == 1
        frac = jnp.where(halve, 0.5, (cl - kq - 0.5) / jnp.maximum(cl - ch, 1.0))
        mid = jnp.clip(lo + frac * (hi - lo), lo, hi)
        stuck = jnp.where(halve, jnp.where((mid <= lo) | (mid >= hi), 1.0, 0.0), stuck)
        c = count_ge(mid)
        ge = c >= kq
        lo2 = jnp.where(ge, mid, lo)
        cl2 = jnp.where(ge, c, cl)
        hi2 = jnp.where(ge, hi, mid)
        ch2 = jnp.where(ge, ch, c)
        return it + 1, not_done(cl2, stuck), lo2, hi2, cl2, ch2, stuck

    stuck0 = jnp.where(lo0 >= rmax, 1.0, 0.0)
    st = (jnp.int32(0), not_done(cl0, stuck0), lo0, rmax, cl0, c_hi, stuck0)
    st = lax.while_loop(cond, body, st)
    return st[2]


def _attn_prompt_kernel(q_ref, qi_ref, wit_ref, k_ref, vt_ref, ki_ref, o_ref,
                        sc_ref, m_ref, l_ref, acc_ref, *, tk, topk, n_heads):
    i = pl.program_id(1)
    tq = q_ref.shape[0]
    nc = ((i + 1) * tq + tk - 1) // tk
    qpos = i * tq + lax.broadcasted_iota(jnp.int32, (1, tq), 1)
    key_row = lax.broadcasted_iota(jnp.int32, (tk, tq), 0)
    w_rows = wit_ref[...] * (IDX_DIM ** -0.5)

    def idx_chunk(c, carry):
        off = pl.multiple_of(c * tk, tk)
        kic = ki_ref[pl.ds(off, tk), :]
        acc = jnp.zeros((tk, tq), F32)
        for h in range(IDX_HEADS):
            s = lax.dot_general(kic, qi_ref[:, h * IDX_DIM:(h + 1) * IDX_DIM], NT_DIMS,
                                preferred_element_type=F32)
            acc = acc + jnp.maximum(s, 0.0) * w_rows[h:h + 1, :]
        sc_ref[pl.ds(off, tk), :] = jnp.where(off + key_row <= qpos, acc, -jnp.inf)
        return carry

    def chunk_loop(n_chunks, chunk_fn):
        base = tq // tk
        per_trip = CHUNK_TRIP_FACTOR * base

        def run(first, count):
            for u in range(count):
                chunk_fn(first + u, 0)

        def body(t, carry):
            run(t * per_trip, per_trip)
            return carry
        lax.fori_loop(0, n_chunks // per_trip, body, 0)
        for r in range(1, CHUNK_TRIP_FACTOR):
            @pl.when(n_chunks % per_trip == r * base)
            def _():
                run((n_chunks // per_trip) * per_trip, r * base)

    chunk_loop(nc, idx_chunk)

    lanes_par = 4

    def reduce_keys(fn, init, red):
        rows = tq

        def body(c, a):
            blk = sc_ref[pl.ds(pl.multiple_of(c * rows, rows), rows), :]
            return fn(a, blk.reshape(lanes_par, rows // (8 * lanes_par), 8, tq))
        a = lax.fori_loop(0, (nc * tk) // rows, body, jnp.full((lanes_par, 8, tq), init, F32))
        return red(red(a, axis=0), axis=0, keepdims=True)

    def count_ge(t):
        return reduce_keys(lambda a, blk: a + jnp.sum(jnp.where(blk >= t, 1.0, 0.0), axis=1), 0.0, jnp.sum)

    rmax = reduce_keys(lambda a, blk: jnp.maximum(a, jnp.max(blk, axis=1)), -jnp.inf, jnp.max)
    rmin = reduce_keys(lambda a, blk: jnp.minimum(a, jnp.min(jnp.where(blk == -jnp.inf, jnp.inf, blk), axis=1)),
                       jnp.inf, jnp.min)
    n_valid = (qpos + 1).astype(F32)
    kq = jnp.minimum(float(topk), n_valid)
    thr = _select_threshold(count_ge, rmin, rmax, n_valid, kq)

    m_ref[...] = jnp.full(m_ref.shape, -jnp.inf, F32)
    l_ref[...] = jnp.zeros(l_ref.shape, F32)
    acc_ref[...] = jnp.zeros(acc_ref.shape, F32)

    def att_chunk(c, carry):
        off = pl.multiple_of(c * tk, tk)
        mask = sc_ref[pl.ds(off, tk), :] >= thr
        for h in range(n_heads):
            hs = slice(h * HEAD_DIM, (h + 1) * HEAD_DIM)
            s = lax.dot_general(k_ref[pl.ds(off, tk), hs], q_ref[:, hs], NT_DIMS,
                                preferred_element_type=F32)
            s = jnp.where(mask, s, NEG)
            m_prev = m_ref[h]
            m_new = jnp.maximum(m_prev, jnp.max(s, axis=0, keepdims=True))
            alpha = jnp.exp2(m_prev - m_new)
            p = jnp.exp2(s - m_new)
            l_ref[h] = alpha * l_ref[h] + jnp.sum(p, axis=0, keepdims=True)
            m_ref[h] = m_new
            pv = jnp.dot(vt_ref[c, hs, :], p.astype(BF16), preferred_element_type=F32)
            acc_ref[h] = alpha * acc_ref[h] + pv
        return carry

    chunk_loop(nc, att_chunk)

    for h in range(n_heads):
        hs = slice(h * HEAD_DIM, (h + 1) * HEAD_DIM)
        o_ref[:, hs] = (acc_ref[h] / l_ref[h]).T.astype(o_ref.dtype)


def attn_prompt(q, qi, wit, k, vt, ki, *, topk):
    b, s, aw = q.shape
    n_heads = aw // HEAD_DIM
    tq = ATTN_TQ
    tk = vt.shape[3]
    kern = functools.partial(_attn_prompt_kernel, tk=ATTN_TK, topk=topk, n_heads=n_heads)
    return pl.pallas_call(
        kern,
        grid=(b, s // tq),
        in_specs=[
            pl.BlockSpec((None, tq, aw), lambda bi, i: (bi, i, 0)),
            pl.BlockSpec((None, tq, qi.shape[2]), lambda bi, i: (bi, i, 0)),
            pl.BlockSpec((None, IDX_HEADS, tq), lambda bi, i: (bi, 0, i)),
            pl.BlockSpec((None, s, aw), lambda bi, i: (bi, 0, 0)),
            pl.BlockSpec((None, s // tk, aw, tk), lambda bi, i: (bi, 0, 0, 0)),
            pl.BlockSpec((None, s, IDX_DIM), lambda bi, i: (bi, 0, 0)),
        ],
        out_specs=pl.BlockSpec((None, tq, aw), lambda bi, i: (bi, i, 0)),
        out_shape=jax.ShapeDtypeStruct((b, s, aw), BF16),
        scratch_shapes=[
            pltpu.VMEM((s, tq), F32),
            pltpu.VMEM((n_heads, 1, tq), F32),
            pltpu.VMEM((n_heads, 1, tq), F32),
            pltpu.VMEM((n_heads, HEAD_DIM, tq), F32),
        ],
        compiler_params=_cparams(2),
        name="attn_prompt",
    )(q, qi, wit, k, vt, ki)


def _idx_sample_kernel(pt_ref, qi_ref, w_ref, *rest, n_pages, group, t_new, topk):
    cik_refs = rest[:group]
    kin_ref, sc_ref, thr_ref = rest[group:]
    b, p = pl.program_id(0), pl.program_id(1)
    db = sc_ref.shape[0]
    rows = qi_ref.shape[0]

    def scores(keys_bf16):
        s = lax.dot_general(qi_ref[...], keys_bf16, NT_DIMS, preferred_element_type=F32)
        r = jnp.maximum(s, 0.0) * w_ref[...]
        return jnp.sum(r.reshape(rows // t_new, t_new, LANES), axis=0)

    for g in range(group):
        sc_ref[b, p * group + g] = scores(cik_refs[g][...].astype(BF16))

    @pl.when(p == 0)
    def _():
        s = scores(kin_ref[...])
        tok = lax.broadcasted_iota(jnp.int32, (t_new, LANES), 0)
        lane = lax.broadcasted_iota(jnp.int32, (t_new, LANES), 1)
        sc_ref[b, n_pages] = jnp.where((lane <= tok) & (lane < t_new), s, -jnp.inf)

    @pl.when((b == db - 1) & (p == pl.num_programs(1) - 1))
    def _():
        def reduce_keys(fn, red):
            return red(red(fn(sc_ref[...]), axis=1), axis=2, keepdims=True)

        def count_ge(t):
            return reduce_keys(lambda x: jnp.where(x >= t[:, None], 1.0, 0.0), jnp.sum)

        rmax = reduce_keys(lambda x: x, jnp.max)
        rmin = reduce_keys(lambda x: jnp.where(x == -jnp.inf, jnp.inf, x), jnp.min)
        tokc = lax.broadcasted_iota(jnp.int32, (db, t_new, 1), 1)
        n_valid = (n_pages * LANES + tokc + 1).astype(F32)
        kq = jnp.minimum(float(topk), n_valid)
        thr = _select_threshold(count_ge, rmin, rmax, n_valid, kq)
        thr_ref[...] = jnp.broadcast_to(thr, thr_ref.shape)


def _page_map(b, p, pt, *, g, group):
    return (0, pt[b, p * group + g], 0, 0)


def idx_sample(page_table, qi_ht, w_ht, cache_idx_k, ki_new_pad, *, t_new, topk):
    db, n_pages = page_table.shape
    group = math.gcd(n_pages, IDX_PAGES_PER_STEP)
    rows = qi_ht.shape[1]
    page = cache_idx_k.shape[2]
    assert n_pages % group == 0
    kern = functools.partial(_idx_sample_kernel, n_pages=n_pages, group=group, t_new=t_new, topk=topk)
    page_specs = [pl.BlockSpec((None, None, page, IDX_DIM), functools.partial(_page_map, g=g, group=group))
                  for g in range(group)]
    grid_spec = pltpu.PrefetchScalarGridSpec(
        num_scalar_prefetch=1,
        grid=(db, n_pages // group),
        in_specs=[
            pl.BlockSpec((None, rows, IDX_DIM), lambda b, p, pt: (b, 0, 0)),
            pl.BlockSpec((None, rows, LANES), lambda b, p, pt: (b, 0, 0)),
            *page_specs,
            pl.BlockSpec((None, page, IDX_DIM), lambda b, p, pt: (b, 0, 0)),
        ],
        out_specs=[
            pl.BlockSpec((db, n_pages + 1, t_new, LANES), lambda b, p, pt: (0, 0, 0, 0)),
            pl.BlockSpec((db, t_new, LANES), lambda b, p, pt: (0, 0, 0)),
        ],
    )
    return pl.pallas_call(
        kern,
        grid_spec=grid_spec,
        out_shape=[jax.ShapeDtypeStruct((db, n_pages + 1, t_new, LANES), F32),
                   jax.ShapeDtypeStruct((db, t_new, LANES), F32)],
        compiler_params=_cparams(2),
        name="idx_sample",
    )(page_table, qi_ht, w_ht, *([cache_idx_k] * group), ki_new_pad)


def _attn_sample_kernel(pt_ref, q_ref, *rest, group, page, n_heads):
    ck_refs = rest[:group]
    cv_refs = rest[group:2 * group]
    kn_ref, vn_ref, sc_ref, scn_ref, thr_ref, o_ref, m_ref, l_ref, acc_ref = rest[2 * group:]
    p = pl.program_id(1)

    @pl.when(p == 0)
    def _():
        m_ref[...] = jnp.full(m_ref.shape, -jnp.inf, F32)
        l_ref[...] = jnp.zeros(l_ref.shape, F32)
        acc_ref[...] = jnp.zeros(acc_ref.shape, F32)

    thr = thr_ref[...]

    def head_rows(ref, h):
        return ref[pl.ds(h, page, stride=n_heads), :].astype(BF16)

    def attend(k_refs, v_refs, sc, sc_thr):
        n = len(k_refs)
        s = jnp.stack([
            lax.dot_general(q_ref[h], jnp.concatenate([head_rows(r, h) for r in k_refs], axis=0), NT_DIMS,
                            preferred_element_type=F32)
            for h in range(n_heads)])
        s = jnp.where(sc[None] >= sc_thr[None], s, NEG)
        m_prev = m_ref[...]
        m_new = jnp.maximum(m_prev, jnp.max(s, axis=2, keepdims=True))
        alpha = jnp.exp2(m_prev - m_new)
        pr = jnp.exp2(s - jnp.concatenate([m_new] * n, axis=2))
        l_ref[...] = alpha * l_ref[...] + jnp.sum(pr, axis=2, keepdims=True)
        m_ref[...] = m_new
        pb = pr.astype(BF16)
        for h in range(n_heads):
            vh = jnp.concatenate([head_rows(r, h) for r in v_refs], axis=0)
            acc_ref[h] = alpha[h] * acc_ref[h] + jnp.dot(pb[h], vh, preferred_element_type=F32)

    attend(ck_refs, cv_refs, jnp.concatenate([sc_ref[g] for g in range(group)], axis=1),
           jnp.concatenate([thr] * group, axis=1))

    @pl.when(p == pl.num_programs(1) - 1)
    def _():
        attend([kn_ref], [vn_ref], scn_ref[...], thr)
        o_ref[...] = acc_ref[...] / l_ref[...]


def attn_sample(page_table, q_ht, cache_k, cache_v, k_new_pad, v_new_pad, sc, thr):
    db, n_pages = page_table.shape
    n_heads, t_new = q_ht.shape[1], q_ht.shape[2]
    prow = cache_k.shape[2]
    page = prow // n_heads
    group = math.gcd(n_pages, KV_PAGES_PER_STEP)
    kern = functools.partial(_attn_sample_kernel, group=group, page=page, n_heads=n_heads)
    cache_specs = [pl.BlockSpec((None, None, prow, HEAD_DIM), functools.partial(_page_map, g=g, group=group))
                   for g in range(group)]
    new_spec = pl.BlockSpec((None, prow, HEAD_DIM), lambda b, p, pt: (b, 0, 0))
    head_spec = pl.BlockSpec((None, n_heads, t_new, HEAD_DIM), lambda b, p, pt: (b, 0, 0, 0))
    grid_spec = pltpu.PrefetchScalarGridSpec(
        num_scalar_prefetch=1,
        grid=(db, n_pages // group),
        in_specs=[
            head_spec,
            *cache_specs, *cache_specs, new_spec, new_spec,
            pl.BlockSpec((None, group, t_new, LANES), lambda b, p, pt: (b, p, 0, 0)),
            pl.BlockSpec((None, None, t_new, LANES), lambda b, p, pt: (b, n_pages, 0, 0)),
            pl.BlockSpec((None, t_new, LANES), lambda b, p, pt: (b, 0, 0)),
        ],
        out_specs=head_spec,
        scratch_shapes=[
            pltpu.VMEM((n_heads, t_new, LANES), F32),
            pltpu.VMEM((n_heads, t_new, LANES), F32),
            pltpu.VMEM((n_heads, t_new, HEAD_DIM), F32),
        ],
    )
    return pl.pallas_call(
        kern,
        grid_spec=grid_spec,
        out_shape=jax.ShapeDtypeStruct((db, n_heads, t_new, HEAD_DIM), F32),
        compiler_params=_cparams(2),
        name="attn_sample",
    )(page_table, q_ht, *([cache_k] * group), *([cache_v] * group), k_new_pad, v_new_pad, sc, sc, thr)


def _outproj_kernel(x_ref, yp_ref, ya_ref, wt_ref, wb_ref, g_ref, x2_ref, h2_ref):
    y = jnp.dot(yp_ref[...], wt_ref[...], preferred_element_type=F32)
    y = y + jnp.dot(ya_ref[...], wb_ref[...], preferred_element_type=F32)
    x2 = x_ref[...] + y
    x2_ref[...] = x2
    h2_ref[...] = _rms(x2, g_ref[...]).astype(h2_ref.dtype)


def outproj(x, yp, ya, w_top, w_bot, g, *, tm):
    m, d = x.shape
    kp, ka = yp.shape[1], ya.shape[1]
    return pl.pallas_call(
        _outproj_kernel,
        grid=(m // tm,),
        in_specs=[
            pl.BlockSpec((tm, d), lambda i: (i, 0)),
            pl.BlockSpec((tm, kp), lambda i: (i, 0)),
            pl.BlockSpec((tm, ka), lambda i: (i, 0)),
            pl.BlockSpec((kp, d), lambda i: (0, 0)),
            pl.BlockSpec((ka, d), lambda i: (0, 0)),
            pl.BlockSpec((1, d), lambda i: (0, 0)),
        ],
        out_specs=[pl.BlockSpec((tm, d), lambda i: (i, 0)), pl.BlockSpec((tm, d), lambda i: (i, 0))],
        out_shape=[jax.ShapeDtypeStruct((m, d), F32), jax.ShapeDtypeStruct((m, d), BF16)],
        compiler_params=_cparams(1),
        name="outproj",
    )(x, yp, ya, w_top, w_bot, g.reshape(1, d))


def _ffn_kernel(h2_ref, wg_ref, wu_ref, wd_ref, x2_ref, g_ref, o_ref, acc_ref):
    f = pl.program_id(1)

    @pl.when(f == 0)
    def _():
        acc_ref[...] = jnp.zeros(acc_ref.shape, F32)

    h2 = h2_ref[...]
    gate = jnp.dot(h2, wg_ref[...], preferred_element_type=F32)
    up = jnp.dot(h2, wu_ref[...], preferred_element_type=F32)
    a = (gate * jax.nn.sigmoid(gate) * up).astype(BF16)
    acc_ref[...] += jnp.dot(a, wd_ref[...], preferred_element_type=F32)

    @pl.when(f == pl.num_programs(1) - 1)
    def _():
        o_ref[...] = _rms(x2_ref[...] + acc_ref[...], g_ref[...])


def ffn(h2, wg, wu, wd, x2, g, *, tm, tf):
    m, d = h2.shape
    dff = wg.shape[1]
    return pl.pallas_call(
        _ffn_kernel,
        grid=(m // tm, dff // tf),
        in_specs=[
            pl.BlockSpec((tm, d), lambda i, f: (i, 0)),
            pl.BlockSpec((d, tf), lambda i, f: (0, f)),
            pl.BlockSpec((d, tf), lambda i, f: (0, f)),
            pl.BlockSpec((tf, d), lambda i, f: (f, 0)),
            pl.BlockSpec((tm, d), lambda i, f: (i, 0)),
            pl.BlockSpec((1, d), lambda i, f: (0, 0)),
        ],
        out_specs=pl.BlockSpec((tm, d), lambda i, f: (i, 0)),
        out_shape=jax.ShapeDtypeStruct((m, d), F32),
        scratch_shapes=[pltpu.VMEM((tm, d), F32)],
        compiler_params=_cparams(2),
        name="ffn",
    )(h2, wg, wu, wd, x2, g.reshape(1, d))


def _rope_tables(pos):
    half = HEAD_DIM // 2
    inv = ROPE_THETA ** (-jnp.arange(half, dtype=F32) / half)
    ang = pos.astype(F32)[:, None] * inv[None, :]
    cos, sin = jnp.cos(ang), jnp.sin(ang)
    return jnp.concatenate([cos, cos], axis=-1), jnp.concatenate([-sin, sin], axis=-1)


def _mix_inputs(x2d, pos_rows, g_mix, w_in16, w_tail, *, tm, vt_chunk=None):
    cos, sin = _rope_tables(pos_rows)
    return mixproj(x2d, g_mix, w_in16, w_tail, cos, sin, tm=tm, vt_chunk=vt_chunk)


def kernel(x_prompt, x_sample, cache_k, cache_v, cache_idx_k, state_pool, page_table, g_mix, w_in, w_pool,
           s_pool, w_out, g_ffn, w_gate, w_up, w_down, g_final):
    B, S, D = x_prompt.shape
    DB, T, _ = x_sample.shape
    depth = w_in.shape[0]
    assert depth == 1
    n_pages = page_table.shape[1]
    page = cache_k.shape[2]
    n_heads = cache_k.shape[3]
    aw = n_heads * HEAD_DIM
    pw = w_pool.shape[1] * w_pool.shape[2]
    past = n_pages * page
    l = 0

    wide = pw + 3 * aw + IDX_HEADS * IDX_DIM
    assert (pw, aw, wide) == (PROJ_TN, PROJ_TN, 6 * PROJ_TN) and w_in.shape[2] == wide + IDX_DIM + IDX_HEADS
    w_in16 = w_in[l].astype(BF16)
    w_tail = jnp.pad(w_in16[:, wide:], ((0, 0), (0, LANES - IDX_HEADS)))
    w_pool16 = w_pool[l].astype(BF16)
    w_out16 = w_out[l].astype(BF16)
    w_top, w_bot = w_out16[:pw], w_out16[pw:]
    wg16, wu16, wd16 = w_gate[l].astype(BF16), w_up[l].astype(BF16), w_down[l].astype(BF16)

    xp2 = x_prompt.reshape(B * S, D)
    pos_p = jnp.arange(S)
    tk = ATTN_TK
    up, q, k32, k16, v32, qi, ki32, ki16, wi, vt = _mix_inputs(xp2, pos_p, g_mix[l], w_in16, w_tail, tm=512,
                                                               vt_chunk=tk)
    up3 = up.reshape(B, S, pw)
    yp = pool_mixer(up3, jnp.zeros((B, HALO, pw), F32), w_pool16, s_pool[l], tm=512, pos0=0)
    wit = wi.reshape(B, S, LANES)[:, :, :IDX_HEADS].transpose(0, 2, 1)
    ya = attn_prompt(q.reshape(B, S, aw), qi.reshape(B, S, -1), wit, k16.reshape(B, S, aw),
                     vt.reshape(B, S // tk, aw, tk), ki16.reshape(B, S, IDX_DIM), topk=min(TOPK_MAX, S // 4))
    x2, h2 = outproj(xp2, yp.reshape(B * S, pw), ya.reshape(B * S, aw), w_top, w_bot, g_ffn[l], tm=512)
    y_prompt = ffn(h2, wg16, wu16, wd16, x2, g_final, tm=512, tf=512).reshape(B, S, D)

    k_prompt = k32.reshape(1, B, S, n_heads, HEAD_DIM)
    v_prompt = v32.reshape(1, B, S, n_heads, HEAD_DIM)
    idx_k_prompt = ki32.reshape(1, B, S, IDX_DIM)
    pool_prompt = up3[:, S - POOL_STATE:, :][None]

    M = DB * T
    xs2 = x_sample.reshape(M, D)
    pos_s = jnp.tile(past + jnp.arange(T), DB)
    ups, qs, ks32, _, vs32, qis, kis32, kis16, wis = _mix_inputs(xs2, pos_s, g_mix[l], w_in16, w_tail, tm=M)
    ups3 = ups.reshape(DB, T, pw)
    prefix16 = jnp.pad(state_pool[l], ((0, 0), (HALO - POOL_STATE, 0), (0, 0)))
    yps = pool_mixer(ups3, prefix16, w_pool16, s_pool[l], tm=T, pos0=past)

    qi_ht = qis.reshape(DB, T, IDX_HEADS, IDX_DIM).transpose(0, 2, 1, 3).reshape(DB, IDX_HEADS * T, IDX_DIM)
    w_ht = wis.reshape(DB, T, LANES)[:, :, :IDX_HEADS].transpose(0, 2, 1).reshape(DB, IDX_HEADS * T, 1)
    w_ht = jnp.broadcast_to(w_ht * (IDX_DIM ** -0.5), (DB, IDX_HEADS * T, LANES))
    pad_rows = lambda a: jnp.pad(a.reshape(DB, T, -1), ((0, 0), (0, page - T), (0, 0)))
    sc, thr = idx_sample(page_table, qi_ht, w_ht, cache_idx_k, pad_rows(kis16), t_new=T,
                         topk=min(TOPK_MAX, (past + T) // 4))

    n_phys = cache_k.shape[1]
    key_head_rows = lambda a: pad_rows(a).reshape(DB, page * n_heads, HEAD_DIM)
    q_ht = qs.reshape(DB, T, n_heads, HEAD_DIM).transpose(0, 2, 1, 3)
    o_ht = attn_sample(page_table, q_ht, cache_k.reshape(depth, n_phys, page * n_heads, HEAD_DIM),
                       cache_v.reshape(depth, n_phys, page * n_heads, HEAD_DIM),
                       key_head_rows(ks32), key_head_rows(vs32), sc, thr)
    yas = o_ht.transpose(0, 2, 1, 3).reshape(M, aw).astype(BF16)

    x2s, h2s = outproj(xs2, yps.reshape(M, pw), yas, w_top, w_bot, g_ffn[l], tm=M)
    y_sample = ffn(h2s, wg16, wu16, wd16, x2s, g_final, tm=M, tf=512).reshape(DB, T, D)

    k_sample = ks32.reshape(1, DB, T, n_heads, HEAD_DIM)
    v_sample = vs32.reshape(1, DB, T, n_heads, HEAD_DIM)
    idx_k_sample = kis32.reshape(1, DB, T, IDX_DIM)
    pool_sample = jnp.concatenate([state_pool[l][:, T:, :], ups3], axis=1)[None]

    return (y_prompt, y_sample, k_prompt, v_prompt, idx_k_prompt, pool_prompt,
            k_sample, v_sample, idx_k_sample, pool_sample)
```

```python
import functools
import math

import jax
import jax.numpy as jnp
from jax import lax
from jax.experimental import pallas as pl
from jax.experimental.pallas import tpu as pltpu

F32 = jnp.float32
BF16 = jnp.bfloat16

LANES = 128
HEAD_DIM = 128
IDX_DIM = 128
IDX_HEADS = 16
POOL_GROUPS = 4
POOL_WINDOWS = (2, 4, 8, 16)
POOL_STATE = max(POOL_WINDOWS) - 1
HALO = 16
TOPK_MAX = 256
ATTN_TQ = 256
ATTN_TK = 128
CHUNK_TRIP_FACTOR = 4
IDX_PAGES_PER_STEP = 32
KV_PAGES_PER_STEP = 16
ROPE_THETA = 10000.0
EPS = 1e-6
NEG = -1e30
MAX_SEARCH_STEPS = 400
LOG2E = 1.4426950408889634
VMEM_LIMIT = 56 * 1024 * 1024

NT_DIMS = (((1,), (1,)), ((), ()))


def _cparams(n_grid):
    return pltpu.CompilerParams(dimension_semantics=("arbitrary",) * n_grid, vmem_limit_bytes=VMEM_LIMIT)


def _rms(x, g):
    return x * lax.rsqrt(jnp.mean(x * x, axis=-1, keepdims=True) + EPS) * g


PROJ_TN = 1024
PROJ_PIECE = 256


def _rope(y, cos, sin):
    return [y[:, g * HEAD_DIM:(g + 1) * HEAD_DIM] * cos
            + pltpu.roll(y[:, g * HEAD_DIM:(g + 1) * HEAD_DIM], HEAD_DIM // 2, axis=1) * sin
            for g in range(y.shape[1] // HEAD_DIM)]


def _mixproj_kernel(x_ref, g_ref, w_ref, wt_ref, cos_ref, sin_ref,
                    up_ref, q_ref, k32_ref, k16_ref, v32_ref, qi_ref, ki32_ref, ki16_ref, wi_ref, *rest):
    maybe_vt_ref, h_ref = rest if len(rest) == 2 else (None, rest[0])
    j = pl.program_id(1)

    def heads_to(pieces, refs, col0=0):
        for g, r in enumerate(pieces):
            for o_ref in refs:
                o_ref[:, col0 + g * HEAD_DIM:col0 + (g + 1) * HEAD_DIM] = r.astype(o_ref.dtype)

    @pl.when(j == 0)
    def _():
        h_ref[...] = _rms(x_ref[...], g_ref[...]).astype(h_ref.dtype)
        t = jnp.dot(h_ref[...], wt_ref[...], preferred_element_type=F32)
        heads_to(_rope(t[:, :IDX_DIM], cos_ref[...], sin_ref[...]), (ki32_ref, ki16_ref))
        wi_ref[...] = t[:, IDX_DIM:] * (IDX_HEADS ** -0.5)

    def col_pieces(fn):
        for c0 in range(0, PROJ_TN, PROJ_PIECE):
            fn(c0, jnp.dot(h_ref[...], w_ref[:, c0:c0 + PROJ_PIECE], preferred_element_type=F32))

    def rope_to(refs, col_base=0, scale=None):
        def fn(c0, y):
            y = y if scale is None else y * scale
            heads_to(_rope(y, cos_ref[...], sin_ref[...]), refs, col0=col_base + c0)
        return fn

    @pl.when(j == 0)
    def _():
        def fn(c0, y):
            up_ref[:, c0:c0 + PROJ_PIECE] = y
        col_pieces(fn)

    @pl.when(j == 1)
    def _():
        col_pieces(rope_to((q_ref,), scale=HEAD_DIM ** -0.5 * LOG2E))

    @pl.when(j == 2)
    def _():
        col_pieces(rope_to((k32_ref, k16_ref)))

    @pl.when(j == 3)
    def _():
        def fn(c0, y):
            v32_ref[:, c0:c0 + PROJ_PIECE] = y
            if maybe_vt_ref is not None:
                n_chunks, _, ck = maybe_vt_ref.shape
                for c in range(n_chunks):
                    for g in range(PROJ_PIECE // HEAD_DIM):
                        hs = slice(g * HEAD_DIM, (g + 1) * HEAD_DIM)
                        maybe_vt_ref[c, c0 + g * HEAD_DIM:c0 + (g + 1) * HEAD_DIM, :] = (
                            y[c * ck:(c + 1) * ck, hs].T.astype(maybe_vt_ref.dtype))
        col_pieces(fn)

    for t_qi in range(2):
        @pl.when(j == 4 + t_qi)
        def _():
            col_pieces(rope_to((qi_ref,), col_base=t_qi * PROJ_TN))


def mixproj(x, g, w_in16, w_tail, cos, sin, *, tm, vt_chunk=None):
    m, d = x.shape
    tn = PROJ_TN
    n_tiles = 6
    row = lambda width: pl.BlockSpec((tm, width), lambda i, j: (i, 0))
    rope_tiles = cos.shape[0] // tm
    rope_spec = pl.BlockSpec((tm, HEAD_DIM), lambda i, j: (i % rope_tiles, 0))
    widths = (tn, tn, tn, tn, tn, 2 * tn, IDX_DIM, IDX_DIM, LANES)
    dtypes = (F32, BF16, F32, BF16, F32, BF16, F32, BF16, F32)
    out_specs = [row(w) for w in widths]
    out_shape = [jax.ShapeDtypeStruct((m, w), dt) for w, dt in zip(widths, dtypes)]
    if vt_chunk is not None:
        out_specs.append(pl.BlockSpec((tm // vt_chunk, tn, vt_chunk), lambda i, j: (i, 0, 0)))
        out_shape.append(jax.ShapeDtypeStruct((m // vt_chunk, tn, vt_chunk), BF16))
    return pl.pallas_call(
        _mixproj_kernel,
        grid=(m // tm, n_tiles),
        in_specs=[
            row(d),
            pl.BlockSpec((1, d), lambda i, j: (0, 0)),
            pl.BlockSpec((d, tn), lambda i, j: (0, j)),
            pl.BlockSpec(w_tail.shape, lambda i, j: (0, 0)),
            rope_spec, rope_spec,
        ],
        out_specs=out_specs,
        out_shape=out_shape,
        scratch_shapes=[pltpu.VMEM((tm, d), BF16)],
        compiler_params=_cparams(2),
        name="mixproj",
    )(x, g.reshape(1, d), w_in16, w_tail, cos, sin)


def _pool_kernel(*refs, tm, pos0, has_prev):
    if has_prev:
        up_ref, prev_ref, prefix_ref, w_ref, s_ref, o_ref, ext_ref = refs
    else:
        up_ref, prefix_ref, w_ref, s_ref, o_ref, ext_ref = refs
    i = pl.program_id(1)
    if has_prev:
        halo = jnp.where(i == 0, prefix_ref[...], prev_ref[...])
    else:
        halo = prefix_ref[...]
    ext_ref[0:HALO, :] = halo
    ext_ref[HALO:HALO + tm, :] = up_ref[...]
    pos = pos0 + i * tm + lax.broadcasted_iota(jnp.int32, (tm, 1), 0)
    gw = up_ref.shape[1] // POOL_GROUPS
    for g, w in enumerate(POOL_WINDOWS):
        sl = slice(g * gw, (g + 1) * gw)
        cur = ext_ref[HALO:HALO + tm, sl]
        s = cur
        for j in range(1, w):
            s = s + ext_ref[HALO - j:HALO - j + tm, sl]
        cnt = jnp.minimum(w, pos + 1).astype(F32)
        d = s / cnt - cur
        y = jnp.dot(d.astype(BF16), w_ref[g], preferred_element_type=F32) * s_ref[:, sl]
        o_ref[:, sl] = y.astype(o_ref.dtype)


def pool_mixer(up, prefix16, w_pool, s_pool, *, tm, pos0):
    b, t, wd = up.shape
    has_prev = t > tm
    gw = wd // POOL_GROUPS
    in_specs = [pl.BlockSpec((None, tm, wd), lambda bi, i: (bi, i, 0))]
    args = [up]
    if has_prev:
        r = tm // HALO
        in_specs.append(pl.BlockSpec((None, HALO, wd), lambda bi, i: (bi, jnp.maximum(i * r - 1, 0), 0)))
        args.append(up)
    in_specs += [
        pl.BlockSpec((None, HALO, wd), lambda bi, i: (bi, 0, 0)),
        pl.BlockSpec((POOL_GROUPS, gw, gw), lambda bi, i: (0, 0, 0)),
        pl.BlockSpec((1, wd), lambda bi, i: (0, 0)),
    ]
    args += [prefix16, w_pool, s_pool.reshape(1, wd)]
    return pl.pallas_call(
        functools.partial(_pool_kernel, tm=tm, pos0=pos0, has_prev=has_prev),
        grid=(b, t // tm),
        in_specs=in_specs,
        out_specs=pl.BlockSpec((None, tm, wd), lambda bi, i: (bi, i, 0)),
        out_shape=jax.ShapeDtypeStruct((b, t, wd), BF16),
        scratch_shapes=[pltpu.VMEM((HALO + tm, wd), F32)],
        compiler_params=_cparams(2),
        name="pool_mixer",
    )(*args)


def _select_threshold(count_ge, rmin, rmax, n_valid, kq):
    c_hi = count_ge(rmax)
    top_tied = c_hi >= kq
    lo0 = jnp.where(top_tied, rmax, rmin)
    cl0 = jnp.where(top_tied, c_hi, n_valid)

    def not_done(cl, stuck):
        return jnp.max(jnp.where((cl == kq) | (stuck > 0.0), 0.0, 1.0))

    def cond(st):
        it, flag = st[0], st[1]
        return jnp.logical_and(it < MAX_SEARCH_STEPS, flag > 0.0)

    def body(st):
        it, _, lo, hi, cl, ch, stuck = st
        halve = (it % 2) == 1
        frac = jnp.where(halve, 0.5, (cl - kq - 0.5) / jnp.maximum(cl - ch, 1.0))
        mid = jnp.clip(lo + frac * (hi - lo), lo, hi)
        stuck = jnp.where(halve, jnp.where((mid <= lo) | (mid >= hi), 1.0, 0.0), stuck)
        c = count_ge(mid)
        ge = c >= kq
        lo2 = jnp.where(ge, mid, lo)
        cl2 = jnp.where(ge, c, cl)
        hi2 = jnp.where(ge, hi, mid)
        ch2 = jnp.where(ge, ch, c)
        return it + 1, not_done(cl2, stuck), lo2, hi2, cl2, ch2, stuck

    stuck0 = jnp.where(lo0 >= rmax, 1.0, 0.0)
    st = (jnp.int32(0), not_done(cl0, stuck0), lo0, rmax, cl0, c_hi, stuck0)
    st = lax.while_loop(cond, body, st)
    return st[2]


def _attn_prompt_kernel(q_ref, qi_ref, wit_ref, k_ref, vt_ref, ki_ref, o_ref,
                        sc_ref, m_ref, l_ref, acc_ref, *, tk, topk, n_heads):
    i = pl.program_id(1)
    tq = q_ref.shape[0]
    nc = ((i + 1) * tq + tk - 1) // tk
    qpos = i * tq + lax.broadcasted_iota(jnp.int32, (1, tq), 1)
    key_row = lax.broadcasted_iota(jnp.int32, (tk, tq), 0)
    w_rows = wit_ref[...] * (IDX_DIM ** -0.5)

    def idx_chunk(c, carry):
        off = pl.multiple_of(c * tk, tk)
        kic = ki_ref[pl.ds(off, tk), :]
        acc = jnp.zeros((tk, tq), F32)
        for h in range(IDX_HEADS):
            s = lax.dot_general(kic, qi_ref[:, h * IDX_DIM:(h + 1) * IDX_DIM], NT_DIMS,
                                preferred_element_type=F32)
            acc = acc + jnp.maximum(s, 0.0) * w_rows[h:h + 1, :]
        sc_ref[pl.ds(off, tk), :] = jnp.where(off + key_row <= qpos, acc, -jnp.inf)
        return carry

    def chunk_loop(n_chunks, chunk_fn):
        base = tq // tk
        per_trip = CHUNK_TRIP_FACTOR * base

        def run(first, count):
            for u in range(count):
                chunk_fn(first + u, 0)

        def body(t, carry):
            run(t * per_trip, per_trip)
            return carry
        lax.fori_loop(0, n_chunks // per_trip, body, 0)
        for r in range(1, CHUNK_TRIP_FACTOR):
            @pl.when(n_chunks % per_trip == r * base)
            def _():
                run((n_chunks // per_trip) * per_trip, r * base)

    chunk_loop(nc, idx_chunk)

    lanes_par = 4

    def reduce_keys(fn, init, red):
        rows = tq

        def body(c, a):
            blk = sc_ref[pl.ds(pl.multiple_of(c * rows, rows), rows), :]
            return fn(a, blk.reshape(lanes_par, rows // (8 * lanes_par), 8, tq))
        a = lax.fori_loop(0, (nc * tk) // rows, body, jnp.full((lanes_par, 8, tq), init, F32))
        return red(red(a, axis=0), axis=0, keepdims=True)

    def count_ge(t):
        return reduce_keys(lambda a, blk: a + jnp.sum(jnp.where(blk >= t, 1.0, 0.0), axis=1), 0.0, jnp.sum)

    rmax = reduce_keys(lambda a, blk: jnp.maximum(a, jnp.max(blk, axis=1)), -jnp.inf, jnp.max)
    rmin = reduce_keys(lambda a, blk: jnp.minimum(a, jnp.min(jnp.where(blk == -jnp.inf, jnp.inf, blk), axis=1)),
                       jnp.inf, jnp.min)
    n_valid = (qpos + 1).astype(F32)
    kq = jnp.minimum(float(topk), n_valid)
    thr = _select_threshold(count_ge, rmin, rmax, n_valid, kq)

    m_ref[...] = jnp.full(m_ref.shape, -jnp.inf, F32)
    l_ref[...] = jnp.zeros(l_ref.shape, F32)
    acc_ref[...] = jnp.zeros(acc_ref.shape, F32)

    def att_chunk(c, carry):
        off = pl.multiple_of(c * tk, tk)
        mask = sc_ref[pl.ds(off, tk), :] >= thr
        for h in range(n_heads):
            hs = slice(h * HEAD_DIM, (h + 1) * HEAD_DIM)
            s = lax.dot_general(k_ref[pl.ds(off, tk), hs], q_ref[:, hs], NT_DIMS,
                                preferred_element_type=F32)
            s = jnp.where(mask, s, NEG)
            m_prev = m_ref[h]
            m_new = jnp.maximum(m_prev, jnp.max(s, axis=0, keepdims=True))
            alpha = jnp.exp2(m_prev - m_new)
            p = jnp.exp2(s - m_new)
            l_ref[h] = alpha * l_ref[h] + jnp.sum(p, axis=0, keepdims=True)
            m_ref[h] = m_new
            pv = jnp.dot(vt_ref[c, hs, :], p.astype(BF16), preferred_element_type=F32)
            acc_ref[h] = alpha * acc_ref[h] + pv
        return carry

    chunk_loop(nc, att_chunk)

    for h in range(n_heads):
        hs = slice(h * HEAD_DIM, (h + 1) * HEAD_DIM)
        o_ref[:, hs] = (acc_ref[h] / l_ref[h]).T.astype(o_ref.dtype)


def attn_prompt(q, qi, wit, k, vt, ki, *, topk):
    b, s, aw = q.shape
    n_heads = aw // HEAD_DIM
    tq = ATTN_TQ
    tk = vt.shape[3]
    kern = functools.partial(_attn_prompt_kernel, tk=ATTN_TK, topk=topk, n_heads=n_heads)
    return pl.pallas_call(
        kern,
        grid=(b, s // tq),
        in_specs=[
            pl.BlockSpec((None, tq, aw), lambda bi, i: (bi, i, 0)),
            pl.BlockSpec((None, tq, qi.shape[2]), lambda bi, i: (bi, i, 0)),
            pl.BlockSpec((None, IDX_HEADS, tq), lambda bi, i: (bi, 0, i)),
            pl.BlockSpec((None, s, aw), lambda bi, i: (bi, 0, 0)),
            pl.BlockSpec((None, s // tk, aw, tk), lambda bi, i: (bi, 0, 0, 0)),
            pl.BlockSpec((None, s, IDX_DIM), lambda bi, i: (bi, 0, 0)),
        ],
        out_specs=pl.BlockSpec((None, tq, aw), lambda bi, i: (bi, i, 0)),
        out_shape=jax.ShapeDtypeStruct((b, s, aw), BF16),
        scratch_shapes=[
            pltpu.VMEM((s, tq), F32),
            pltpu.VMEM((n_heads, 1, tq), F32),
            pltpu.VMEM((n_heads, 1, tq), F32),
            pltpu.VMEM((n_heads, HEAD_DIM, tq), F32),
        ],
        compiler_params=_cparams(2),
        name="attn_prompt",
    )(q, qi, wit, k, vt, ki)


def _idx_sample_kernel(pt_ref, qi_ref, w_ref, *rest, n_pages, group, t_new, topk):
    cik_refs = rest[:group]
    kin_ref, sc_ref, thr_ref = rest[group:]
    b, p = pl.program_id(0), pl.program_id(1)
    db = sc_ref.shape[0]
    rows = qi_ref.shape[0]

    def scores(keys_bf16):
        s = lax.dot_general(qi_ref[...], keys_bf16, NT_DIMS, preferred_element_type=F32)
        r = jnp.maximum(s, 0.0) * w_ref[...]
        return jnp.sum(r.reshape(rows // t_new, t_new, LANES), axis=0)

    for g in range(group):
        sc_ref[b, p * group + g] = scores(cik_refs[g][...].astype(BF16))

    @pl.when(p == 0)
    def _():
        s = scores(kin_ref[...])
        tok = lax.broadcasted_iota(jnp.int32, (t_new, LANES), 0)
        lane = lax.broadcasted_iota(jnp.int32, (t_new, LANES), 1)
        sc_ref[b, n_pages] = jnp.where((lane <= tok) & (lane < t_new), s, -jnp.inf)

    @pl.when((b == db - 1) & (p == pl.num_programs(1) - 1))
    def _():
        def reduce_keys(fn, red):
            return red(red(fn(sc_ref[...]), axis=1), axis=2, keepdims=True)

        def count_ge(t):
            return reduce_keys(lambda x: jnp.where(x >= t[:, None], 1.0, 0.0), jnp.sum)

        rmax = reduce_keys(lambda x: x, jnp.max)
        rmin = reduce_keys(lambda x: jnp.where(x == -jnp.inf, jnp.inf, x), jnp.min)
        tokc = lax.broadcasted_iota(jnp.int32, (db, t_new, 1), 1)
        n_valid = (n_pages * LANES + tokc + 1).astype(F32)
        kq = jnp.minimum(float(topk), n_valid)
        thr = _select_threshold(count_ge, rmin, rmax, n_valid, kq)
        thr_ref[...] = jnp.broadcast_to(thr, thr_ref.shape)


def _page_map(b, p, pt, *, g, group):
    return (0, pt[b, p * group + g], 0, 0)


def idx_sample(page_table, qi_ht, w_ht, cache_idx_k, ki_new_pad, *, t_new, topk):
    db, n_pages = page_table.shape
    group = math.gcd(n_pages, IDX_PAGES_PER_STEP)
    rows = qi_ht.shape[1]
    page = cache_idx_k.shape[2]
    assert n_pages % group == 0
    kern = functools.partial(_idx_sample_kernel, n_pages=n_pages, group=group, t_new=t_new, topk=topk)
    page_specs = [pl.BlockSpec((None, None, page, IDX_DIM), functools.partial(_page_map, g=g, group=group))
                  for g in range(group)]
    grid_spec = pltpu.PrefetchScalarGridSpec(
        num_scalar_prefetch=1,
        grid=(db, n_pages // group),
        in_specs=[
            pl.BlockSpec((None, rows, IDX_DIM), lambda b, p, pt: (b, 0, 0)),
            pl.BlockSpec((None, rows, LANES), lambda b, p, pt: (b, 0, 0)),
            *page_specs,
            pl.BlockSpec((None, page, IDX_DIM), lambda b, p, pt: (b, 0, 0)),
        ],
        out_specs=[
            pl.BlockSpec((db, n_pages + 1, t_new, LANES), lambda b, p, pt: (0, 0, 0, 0)),
            pl.BlockSpec((db, t_new, LANES), lambda b, p, pt: (0, 0, 0)),
        ],
    )
    return pl.pallas_call(
        kern,
        grid_spec=grid_spec,
        out_shape=[jax.ShapeDtypeStruct((db, n_pages + 1, t_new, LANES), F32),
                   jax.ShapeDtypeStruct((db, t_new, LANES), F32)],
        compiler_params=_cparams(2),
        name="idx_sample",
    )(page_table, qi_ht, w_ht, *([cache_idx_k] * group), ki_new_pad)


def _attn_sample_kernel(pt_ref, q_ref, *rest, group, page, n_heads):
    ck_refs = rest[:group]
    cv_refs = rest[group:2 * group]
    kn_ref, vn_ref, sc_ref, scn_ref, thr_ref, o_ref, m_ref, l_ref, acc_ref = rest[2 * group:]
    p = pl.program_id(1)

    @pl.when(p == 0)
    def _():
        m_ref[...] = jnp.full(m_ref.shape, -jnp.inf, F32)
        l_ref[...] = jnp.zeros(l_ref.shape, F32)
        acc_ref[...] = jnp.zeros(acc_ref.shape, F32)

    thr = thr_ref[...]

    def head_rows(ref, h):
        return ref[pl.ds(h, page, stride=n_heads), :].astype(BF16)

    def attend(k_refs, v_refs, sc, sc_thr):
        n = len(k_refs)
        s = jnp.stack([
            lax.dot_general(q_ref[h], jnp.concatenate([head_rows(r, h) for r in k_refs], axis=0), NT_DIMS,
                            preferred_element_type=F32)
            for h in range(n_heads)])
        s = jnp.where(sc[None] >= sc_thr[None], s, NEG)
        m_prev = m_ref[...]
        m_new = jnp.maximum(m_prev, jnp.max(s, axis=2, keepdims=True))
        alpha = jnp.exp2(m_prev - m_new)
        pr = jnp.exp2(s - jnp.concatenate([m_new] * n, axis=2))
        l_ref[...] = alpha * l_ref[...] + jnp.sum(pr, axis=2, keepdims=True)
        m_ref[...] = m_new
        pb = pr.astype(BF16)
        for h in range(n_heads):
            vh = jnp.concatenate([head_rows(r, h) for r in v_refs], axis=0)
            acc_ref[h] = alpha[h] * acc_ref[h] + jnp.dot(pb[h], vh, preferred_element_type=F32)

    attend(ck_refs, cv_refs, jnp.concatenate([sc_ref[g] for g in range(group)], axis=1),
           jnp.concatenate([thr] * group, axis=1))

    @pl.when(p == pl.num_programs(1) - 1)
    def _():
        attend([kn_ref], [vn_ref], scn_ref[...], thr)
        o_ref[...] = acc_ref[...] / l_ref[...]


def attn_sample(page_table, q_ht, cache_k, cache_v, k_new_pad, v_new_pad, sc, thr):
    db, n_pages = page_table.shape
    n_heads, t_new = q_ht.shape[1], q_ht.shape[2]
    prow = cache_k.shape[2]
    page = prow // n_heads
    group = math.gcd(n_pages, KV_PAGES_PER_STEP)
    kern = functools.partial(_attn_sample_kernel, group=group, page=page, n_heads=n_heads)
    cache_specs = [pl.BlockSpec((None, None, prow, HEAD_DIM), functools.partial(_page_map, g=g, group=group))
                   for g in range(group)]
    new_spec = pl.BlockSpec((None, prow, HEAD_DIM), lambda b, p, pt: (b, 0, 0))
    head_spec = pl.BlockSpec((None, n_heads, t_new, HEAD_DIM), lambda b, p, pt: (b, 0, 0, 0))
    grid_spec = pltpu.PrefetchScalarGridSpec(
        num_scalar_prefetch=1,
        grid=(db, n_pages // group),
        in_specs=[
            head_spec,
            *cache_specs, *cache_specs, new_spec, new_spec,
            pl.BlockSpec((None, group, t_new, LANES), lambda b, p, pt: (b, p, 0, 0)),
            pl.BlockSpec((None, None, t_new, LANES), lambda b, p, pt: (b, n_pages, 0, 0)),
            pl.BlockSpec((None, t_new, LANES), lambda b, p, pt: (b, 0, 0)),
        ],
        out_specs=head_spec,
        scratch_shapes=[
            pltpu.VMEM((n_heads, t_new, LANES), F32),
            pltpu.VMEM((n_heads, t_new, LANES), F32),
            pltpu.VMEM((n_heads, t_new, HEAD_DIM), F32),
        ],
    )
    return pl.pallas_call(
        kern,
        grid_spec=grid_spec,
        out_shape=jax.ShapeDtypeStruct((db, n_heads, t_new, HEAD_DIM), F32),
        compiler_params=_cparams(2),
        name="attn_sample",
    )(page_table, q_ht, *([cache_k] * group), *([cache_v] * group), k_new_pad, v_new_pad, sc, sc, thr)


def _outproj_kernel(x_ref, yp_ref, ya_ref, wt_ref, wb_ref, g_ref, x2_ref, h2_ref):
    y = jnp.dot(yp_ref[...], wt_ref[...], preferred_element_type=F32)
    y = y + jnp.dot(ya_ref[...], wb_ref[...], preferred_element_type=F32)
    x2 = x_ref[...] + y
    x2_ref[...] = x2
    h2_ref[...] = _rms(x2, g_ref[...]).astype(h2_ref.dtype)


def outproj(x, yp, ya, w_top, w_bot, g, *, tm):
    m, d = x.shape
    kp, ka = yp.shape[1], ya.shape[1]
    return pl.pallas_call(
        _outproj_kernel,
        grid=(m // tm,),
        in_specs=[
            pl.BlockSpec((tm, d), lambda i: (i, 0)),
            pl.BlockSpec((tm, kp), lambda i: (i, 0)),
            pl.BlockSpec((tm, ka), lambda i: (i, 0)),
            pl.BlockSpec((kp, d), lambda i: (0, 0)),
            pl.BlockSpec((ka, d), lambda i: (0, 0)),
            pl.BlockSpec((1, d), lambda i: (0, 0)),
        ],
        out_specs=[pl.BlockSpec((tm, d), lambda i: (i, 0)), pl.BlockSpec((tm, d), lambda i: (i, 0))],
        out_shape=[jax.ShapeDtypeStruct((m, d), F32), jax.ShapeDtypeStruct((m, d), BF16)],
        compiler_params=_cparams(1),
        name="outproj",
    )(x, yp, ya, w_top, w_bot, g.reshape(1, d))


def _ffn_kernel(h2_ref, wg_ref, wu_ref, wd_ref, x2_ref, g_ref, o_ref, acc_ref):
    f = pl.program_id(1)

    @pl.when(f == 0)
    def _():
        acc_ref[...] = jnp.zeros(acc_ref.shape, F32)

    h2 = h2_ref[...]
    gate = jnp.dot(h2, wg_ref[...], preferred_element_type=F32)
    up = jnp.dot(h2, wu_ref[...], preferred_element_type=F32)
    a = (gate * jax.nn.sigmoid(gate) * up).astype(BF16)
    acc_ref[...] += jnp.dot(a, wd_ref[...], preferred_element_type=F32)

    @pl.when(f == pl.num_programs(1) - 1)
    def _():
        o_ref[...] = _rms(x2_ref[...] + acc_ref[...], g_ref[...])


def ffn(h2, wg, wu, wd, x2, g, *, tm, tf):
    m, d = h2.shape
    dff = wg.shape[1]
    return pl.pallas_call(
        _ffn_kernel,
        grid=(m // tm, dff // tf),
        in_specs=[
            pl.BlockSpec((tm, d), lambda i, f: (i, 0)),
            pl.BlockSpec((d, tf), lambda i, f: (0, f)),
            pl.BlockSpec((d, tf), lambda i, f: (0, f)),
            pl.BlockSpec((tf, d), lambda i, f: (f, 0)),
            pl.BlockSpec((tm, d), lambda i, f: (i, 0)),
            pl.BlockSpec((1, d), lambda i, f: (0, 0)),
        ],
        out_specs=pl.BlockSpec((tm, d), lambda i, f: (i, 0)),
        out_shape=jax.ShapeDtypeStruct((m, d), F32),
        scratch_shapes=[pltpu.VMEM((tm, d), F32)],
        compiler_params=_cparams(2),
        name="ffn",
    )(h2, wg, wu, wd, x2, g.reshape(1, d))


def _rope_tables(pos):
    half = HEAD_DIM // 2
    inv = ROPE_THETA ** (-jnp.arange(half, dtype=F32) / half)
    ang = pos.astype(F32)[:, None] * inv[None, :]
    cos, sin = jnp.cos(ang), jnp.sin(ang)
    return jnp.concatenate([cos, cos], axis=-1), jnp.concatenate([-sin, sin], axis=-1)


def _mix_inputs(x2d, pos_rows, g_mix, w_in16, w_tail, *, tm, vt_chunk=None):
    cos, sin = _rope_tables(pos_rows)
    return mixproj(x2d, g_mix, w_in16, w_tail, cos, sin, tm=tm, vt_chunk=vt_chunk)


def kernel(x_prompt, x_sample, cache_k, cache_v, cache_idx_k, state_pool, page_table, g_mix, w_in, w_pool,
           s_pool, w_out, g_ffn, w_gate, w_up, w_down, g_final):
    B, S, D = x_prompt.shape
    DB, T, _ = x_sample.shape
    depth = w_in.shape[0]
    assert depth == 1
    n_pages = page_table.shape[1]
    page = cache_k.shape[2]
    n_heads = cache_k.shape[3]
    aw = n_heads * HEAD_DIM
    pw = w_pool.shape[1] * w_pool.shape[2]
    past = n_pages * page
    l = 0

    wide = pw + 3 * aw + IDX_HEADS * IDX_DIM
    assert (pw, aw, wide) == (PROJ_TN, PROJ_TN, 6 * PROJ_TN) and w_in.shape[2] == wide + IDX_DIM + IDX_HEADS
    w_in16 = w_in[l].astype(BF16)
    w_tail = jnp.pad(w_in16[:, wide:], ((0, 0), (0, LANES - IDX_HEADS)))
    w_pool16 = w_pool[l].astype(BF16)
    w_out16 = w_out[l].astype(BF16)
    w_top, w_bot = w_out16[:pw], w_out16[pw:]
    wg16, wu16, wd16 = w_gate[l].astype(BF16), w_up[l].astype(BF16), w_down[l].astype(BF16)

    xp2 = x_prompt.reshape(B * S, D)
    pos_p = jnp.arange(S)
    tk = ATTN_TK
    up, q, k32, k16, v32, qi, ki32, ki16, wi, vt = _mix_inputs(xp2, pos_p, g_mix[l], w_in16, w_tail, tm=512,
                                                               vt_chunk=tk)
    up3 = up.reshape(B, S, pw)
    yp = pool_mixer(up3, jnp.zeros((B, HALO, pw), F32), w_pool16, s_pool[l], tm=512, pos0=0)
    wit = wi.reshape(B, S, LANES)[:, :, :IDX_HEADS].transpose(0, 2, 1)
    ya = attn_prompt(q.reshape(B, S, aw), qi.reshape(B, S, -1), wit, k16.reshape(B, S, aw),
                     vt.reshape(B, S // tk, aw, tk), ki16.reshape(B, S, IDX_DIM), topk=min(TOPK_MAX, S // 4))
    x2, h2 = outproj(xp2, yp.reshape(B * S, pw), ya.reshape(B * S, aw), w_top, w_bot, g_ffn[l], tm=512)
    y_prompt = ffn(h2, wg16, wu16, wd16, x2, g_final, tm=512, tf=512).reshape(B, S, D)

    k_prompt = k32.reshape(1, B, S, n_heads, HEAD_DIM)
    v_prompt = v32.reshape(1, B, S, n_heads, HEAD_DIM)
    idx_k_prompt = ki32.reshape(1, B, S, IDX_DIM)
    pool_prompt = up3[:, S - POOL_STATE:, :][None]

    M = DB * T
    xs2 = x_sample.reshape(M, D)
    pos_s = jnp.tile(past + jnp.arange(T), DB)
    ups, qs, ks32, _, vs32, qis, kis32, kis16, wis = _mix_inputs(xs2, pos_s, g_mix[l], w_in16, w_tail, tm=M)
    ups3 = ups.reshape(DB, T, pw)
    prefix16 = jnp.pad(state_pool[l], ((0, 0), (HALO - POOL_STATE, 0), (0, 0)))
    yps = pool_mixer(ups3, prefix16, w_pool16, s_pool[l], tm=T, pos0=past)

    qi_ht = qis.reshape(DB, T, IDX_HEADS, IDX_DIM).transpose(0, 2, 1, 3).reshape(DB, IDX_HEADS * T, IDX_DIM)
    w_ht = wis.reshape(DB, T, LANES)[:, :, :IDX_HEADS].transpose(0, 2, 1).reshape(DB, IDX_HEADS * T, 1)
    w_ht = jnp.broadcast_to(w_ht * (IDX_DIM ** -0.5), (DB, IDX_HEADS * T, LANES))
    pad_rows = lambda a: jnp.pad(a.reshape(DB, T, -1), ((0, 0), (0, page - T), (0, 0)))
    sc, thr = idx_sample(page_table, qi_ht, w_ht, cache_idx_k, pad_rows(kis16), t_new=T,
                         topk=min(TOPK_MAX, (past + T) // 4))

    n_phys = cache_k.shape[1]
    key_head_rows = lambda a: pad_rows(a).reshape(DB, page * n_heads, HEAD_DIM)
    q_ht = qs.reshape(DB, T, n_heads, HEAD_DIM).transpose(0, 2, 1, 3)
    o_ht = attn_sample(page_table, q_ht, cache_k.reshape(depth, n_phys, page * n_heads, HEAD_DIM),
                       cache_v.reshape(depth, n_phys, page * n_heads, HEAD_DIM),
                       key_head_rows(ks32), key_head_rows(vs32), sc, thr)
    yas = o_ht.transpose(0, 2, 1, 3).reshape(M, aw).astype(BF16)

    x2s, h2s = outproj(xs2, yps.reshape(M, pw), yas, w_top, w_bot, g_ffn[l], tm=M)
    y_sample = ffn(h2s, wg16, wu16, wd16, x2s, g_final, tm=M, tf=512).reshape(DB, T, D)

    k_sample = ks32.reshape(1, DB, T, n_heads, HEAD_DIM)
    v_sample = vs32.reshape(1, DB, T, n_heads, HEAD_DIM)
    idx_k_sample = kis32.reshape(1, DB, T, IDX_DIM)
    pool_sample = jnp.concatenate([state_pool[l][:, T:, :], ups3], axis=1)[None]

    return (y_prompt, y_sample, k_prompt, v_prompt, idx_k_prompt, pool_prompt,
            k_sample, v_sample, idx_k_sample, pool_sample)
```

```python
import functools
import math

import jax
import jax.numpy as jnp
from jax import lax
from jax.experimental import pallas as pl
from jax.experimental.pallas import tpu as pltpu

F32 = jnp.float32
BF16 = jnp.bfloat16

LANES = 128
HEAD_DIM = 128
IDX_DIM = 128
IDX_HEADS = 16
POOL_GROUPS = 4
POOL_WINDOWS = (2, 4, 8, 16)
POOL_STATE = max(POOL_WINDOWS) - 1
HALO = 16
TOPK_MAX = 256
ATTN_TQ = 256
ATTN_TK = 128
CHUNK_TRIP_FACTOR = 4
IDX_PAGES_PER_STEP = 32
KV_PAGES_PER_STEP = 16
ROPE_THETA = 10000.0
EPS = 1e-6
NEG = -1e30
MAX_SEARCH_STEPS = 1024
LOG2E = 1.4426950408889634
VMEM_LIMIT = 56 * 1024 * 1024

NT_DIMS = (((1,), (1,)), ((), ()))


def _cparams(n_grid):
    return pltpu.CompilerParams(dimension_semantics=("arbitrary",) * n_grid, vmem_limit_bytes=VMEM_LIMIT)


def _rms(x, g):
    return x * lax.rsqrt(jnp.mean(x * x, axis=-1, keepdims=True) + EPS) * g


PROJ_TN = 1024
PROJ_PIECE = 256


def _rope(y, cos, sin):
    return [y[:, g * HEAD_DIM:(g + 1) * HEAD_DIM] * cos
            + pltpu.roll(y[:, g * HEAD_DIM:(g + 1) * HEAD_DIM], HEAD_DIM // 2, axis=1) * sin
            for g in range(y.shape[1] // HEAD_DIM)]


def _mixproj_kernel(x_ref, g_ref, w_ref, wt_ref, cos_ref, sin_ref,
                    up_ref, q_ref, k32_ref, k16_ref, v32_ref, qi_ref, ki32_ref, ki16_ref, wi_ref, *rest):
    maybe_vt_ref, h_ref = rest if len(rest) == 2 else (None, rest[0])
    j = pl.program_id(1)

    def heads_to(pieces, refs, col0=0):
        for g, r in enumerate(pieces):
            for o_ref in refs:
                o_ref[:, col0 + g * HEAD_DIM:col0 + (g + 1) * HEAD_DIM] = r.astype(o_ref.dtype)

    @pl.when(j == 0)
    def _():
        h_ref[...] = _rms(x_ref[...], g_ref[...]).astype(h_ref.dtype)
        t = jnp.dot(h_ref[...], wt_ref[...], preferred_element_type=F32)
        heads_to(_rope(t[:, :IDX_DIM], cos_ref[...], sin_ref[...]), (ki32_ref, ki16_ref))
        wi_ref[...] = t[:, IDX_DIM:] * (IDX_HEADS ** -0.5)

    def col_pieces(fn):
        for c0 in range(0, PROJ_TN, PROJ_PIECE):
            fn(c0, jnp.dot(h_ref[...], w_ref[:, c0:c0 + PROJ_PIECE], preferred_element_type=F32))

    def rope_to(refs, col_base=0, scale=None):
        def fn(c0, y):
            y = y if scale is None else y * scale
            heads_to(_rope(y, cos_ref[...], sin_ref[...]), refs, col0=col_base + c0)
        return fn

    @pl.when(j == 0)
    def _():
        def fn(c0, y):
            up_ref[:, c0:c0 + PROJ_PIECE] = y
        col_pieces(fn)

    @pl.when(j == 1)
    def _():
        col_pieces(rope_to((q_ref,), scale=HEAD_DIM ** -0.5 * LOG2E))

    @pl.when(j == 2)
    def _():
        col_pieces(rope_to((k32_ref, k16_ref)))

    @pl.when(j == 3)
    def _():
        def fn(c0, y):
            v32_ref[:, c0:c0 + PROJ_PIECE] = y
            if maybe_vt_ref is not None:
                n_chunks, _, ck = maybe_vt_ref.shape
                for c in range(n_chunks):
                    for g in range(PROJ_PIECE // HEAD_DIM):
                        hs = slice(g * HEAD_DIM, (g + 1) * HEAD_DIM)
                        maybe_vt_ref[c, c0 + g * HEAD_DIM:c0 + (g + 1) * HEAD_DIM, :] = (
                            y[c * ck:(c + 1) * ck, hs].T.astype(maybe_vt_ref.dtype))
        col_pieces(fn)

    for t_qi in range(2):
        @pl.when(j == 4 + t_qi)
        def _():
            col_pieces(rope_to((qi_ref,), col_base=t_qi * PROJ_TN))


def mixproj(x, g, w_in16, w_tail, cos, sin, *, tm, vt_chunk=None):
    m, d = x.shape
    tn = PROJ_TN
    n_tiles = 6
    row = lambda width: pl.BlockSpec((tm, width), lambda i, j: (i, 0))
    rope_tiles = cos.shape[0] // tm
    rope_spec = pl.BlockSpec((tm, HEAD_DIM), lambda i, j: (i % rope_tiles, 0))
    widths = (tn, tn, tn, tn, tn, 2 * tn, IDX_DIM, IDX_DIM, LANES)
    dtypes = (F32, BF16, F32, BF16, F32, BF16, F32, BF16, F32)
    out_specs = [row(w) for w in widths]
    out_shape = [jax.ShapeDtypeStruct((m, w), dt) for w, dt in zip(widths, dtypes)]
    if vt_chunk is not None:
        out_specs.append(pl.BlockSpec((tm // vt_chunk, tn, vt_chunk), lambda i, j: (i, 0, 0)))
        out_shape.append(jax.ShapeDtypeStruct((m // vt_chunk, tn, vt_chunk), BF16))
    return pl.pallas_call(
        _mixproj_kernel,
        grid=(m // tm, n_tiles),
        in_specs=[
            row(d),
            pl.BlockSpec((1, d), lambda i, j: (0, 0)),
            pl.BlockSpec((d, tn), lambda i, j: (0, j)),
            pl.BlockSpec(w_tail.shape, lambda i, j: (0, 0)),
            rope_spec, rope_spec,
        ],
        out_specs=out_specs,
        out_shape=out_shape,
        scratch_shapes=[pltpu.VMEM((tm, d), BF16)],
        compiler_params=_cparams(2),
        name="mixproj",
    )(x, g.reshape(1, d), w_in16, w_tail, cos, sin)


def _pool_kernel(*refs, tm, pos0, has_prev):
    if has_prev:
        up_ref, prev_ref, prefix_ref, w_ref, s_ref, o_ref, ext_ref = refs
    else:
        up_ref, prefix_ref, w_ref, s_ref, o_ref, ext_ref = refs
    i = pl.program_id(1)
    if has_prev:
        halo = jnp.where(i == 0, prefix_ref[...], prev_ref[...])
    else:
        halo = prefix_ref[...]
    ext_ref[0:HALO, :] = halo
    ext_ref[HALO:HALO + tm, :] = up_ref[...]
    pos = pos0 + i * tm + lax.broadcasted_iota(jnp.int32, (tm, 1), 0)
    gw = up_ref.shape[1] // POOL_GROUPS
    for g, w in enumerate(POOL_WINDOWS):
        sl = slice(g * gw, (g + 1) * gw)
        cur = ext_ref[HALO:HALO + tm, sl]
        s = cur
        for j in range(1, w):
            s = s + ext_ref[HALO - j:HALO - j + tm, sl]
        cnt = jnp.minimum(w, pos + 1).astype(F32)
        d = s / cnt - cur
        y = jnp.dot(d.astype(BF16), w_ref[g], preferred_element_type=F32) * s_ref[:, sl]
        o_ref[:, sl] = y.astype(o_ref.dtype)


def pool_mixer(up, prefix16, w_pool, s_pool, *, tm, pos0):
    b, t, wd = up.shape
    has_prev = t > tm
    gw = wd // POOL_GROUPS
    in_specs = [pl.BlockSpec((None, tm, wd), lambda bi, i: (bi, i, 0))]
    args = [up]
    if has_prev:
        r = tm // HALO
        in_specs.append(pl.BlockSpec((None, HALO, wd), lambda bi, i: (bi, jnp.maximum(i * r - 1, 0), 0)))
        args.append(up)
    in_specs += [
        pl.BlockSpec((None, HALO, wd), lambda bi, i: (bi, 0, 0)),
        pl.BlockSpec((POOL_GROUPS, gw, gw), lambda bi, i: (0, 0, 0)),
        pl.BlockSpec((1, wd), lambda bi, i: (0, 0)),
    ]
    args += [prefix16, w_pool, s_pool.reshape(1, wd)]
    return pl.pallas_call(
        functools.partial(_pool_kernel, tm=tm, pos0=pos0, has_prev=has_prev),
        grid=(b, t // tm),
        in_specs=in_specs,
        out_specs=pl.BlockSpec((None, tm, wd), lambda bi, i: (bi, i, 0)),
        out_shape=jax.ShapeDtypeStruct((b, t, wd), BF16),
        scratch_shapes=[pltpu.VMEM((HALO + tm, wd), F32)],
        compiler_params=_cparams(2),
        name="pool_mixer",
    )(*args)


def _select_threshold(count_ge, rmin, rmax, n_valid, kq):
    c_hi = count_ge(rmax)
    top_tied = c_hi >= kq
    lo0 = jnp.where(top_tied, rmax, rmin)
    cl0 = jnp.where(top_tied, c_hi, n_valid)

    def not_done(cl, stuck):
        return jnp.max(jnp.where((cl == kq) | (stuck > 0.0), 0.0, 1.0))

    def cond(st):
        it, flag = st[0], st[1]
        return jnp.logical_and(it < MAX_SEARCH_STEPS, flag > 0.0)

    def body(st):
        it, _, lo, hi, cl, ch, stuck = st
        halve = (it % 2) == 1
        frac = jnp.where(halve, 0.5, (cl - kq - 0.5) / jnp.maximum(cl - ch, 1.0))
        mid = jnp.clip(lo + frac * (hi - lo), lo, hi)
        stuck = jnp.where(halve, jnp.where((mid <= lo) | (mid >= hi), 1.0, 0.0), stuck)
        c = count_ge(mid)
        ge = c >= kq
        lo2 = jnp.where(ge, mid, lo)
        cl2 = jnp.where(ge, c, cl)
        hi2 = jnp.where(ge, hi, mid)
        ch2 = jnp.where(ge, ch, c)
        return it + 1, not_done(cl2, stuck), lo2, hi2, cl2, ch2, stuck

    stuck0 = jnp.where(lo0 >= rmax, 1.0, 0.0)
    st = (jnp.int32(0), not_done(cl0, stuck0), lo0, rmax, cl0, c_hi, stuck0)
    st = lax.while_loop(cond, body, st)
    return st[2], st[4]


def _last_tied_index(count_tied_upto, need, n_keys):
    def body(_, st):
        lo, hi = st
        mid = jnp.floor((lo + hi) * 0.5)
        ok = count_tied_upto(mid) >= need
        return jnp.where(ok, lo, mid), jnp.where(ok, mid, hi)

    steps = max(1, math.ceil(math.log2(n_keys + 1)))
    _, hi = lax.fori_loop(0, steps, body, (jnp.full(need.shape, -1.0, F32), jnp.full(need.shape, n_keys - 1.0, F32)))
    return hi


def _attn_prompt_kernel(q_ref, qi_ref, wit_ref, k_ref, vt_ref, ki_ref, o_ref,
                        sc_ref, m_ref, l_ref, acc_ref, *, tk, topk, n_heads):
    i = pl.program_id(1)
    tq = q_ref.shape[0]
    nc = ((i + 1) * tq + tk - 1) // tk
    qpos = i * tq + lax.broadcasted_iota(jnp.int32, (1, tq), 1)
    key_row = lax.broadcasted_iota(jnp.int32, (tk, tq), 0)
    w_rows = wit_ref[...] * (IDX_DIM ** -0.5)

    def idx_chunk(c, carry):
        off = pl.multiple_of(c * tk, tk)
        kic = ki_ref[pl.ds(off, tk), :]
        acc = jnp.zeros((tk, tq), F32)
        for h in range(IDX_HEADS):
            s = lax.dot_general(kic, qi_ref[:, h * IDX_DIM:(h + 1) * IDX_DIM], NT_DIMS,
                                preferred_element_type=F32)
            acc = acc + jnp.maximum(s, 0.0) * w_rows[h:h + 1, :]
        sc_ref[pl.ds(off, tk), :] = jnp.where(off + key_row <= qpos, acc, -jnp.inf)
        return carry

    def chunk_loop(n_chunks, chunk_fn):
        base = tq // tk
        per_trip = CHUNK_TRIP_FACTOR * base

        def run(first, count):
            for u in range(count):
                chunk_fn(first + u, 0)

        def body(t, carry):
            run(t * per_trip, per_trip)
            return carry
        lax.fori_loop(0, n_chunks // per_trip, body, 0)
        for r in range(1, CHUNK_TRIP_FACTOR):
            @pl.when(n_chunks % per_trip == r * base)
            def _():
                run((n_chunks // per_trip) * per_trip, r * base)

    chunk_loop(nc, idx_chunk)

    lanes_par = 4

    def reduce_keys(fn, init, red):
        rows = tq

        def body(c, a):
            blk = sc_ref[pl.ds(pl.multiple_of(c * rows, rows), rows), :]
            return fn(a, blk.reshape(lanes_par, rows // (8 * lanes_par), 8, tq))
        a = lax.fori_loop(0, (nc * tk) // rows, body, jnp.full((lanes_par, 8, tq), init, F32))
        return red(red(a, axis=0), axis=0, keepdims=True)

    def count_ge(t):
        return reduce_keys(lambda a, blk: a + jnp.sum(jnp.where(blk >= t, 1.0, 0.0), axis=1), 0.0, jnp.sum)

    rmax = reduce_keys(lambda a, blk: jnp.maximum(a, jnp.max(blk, axis=1)), -jnp.inf, jnp.max)
    rmin = reduce_keys(lambda a, blk: jnp.minimum(a, jnp.min(jnp.where(blk == -jnp.inf, jnp.inf, blk), axis=1)),
                       jnp.inf, jnp.min)
    n_valid = (qpos + 1).astype(F32)
    kq = jnp.minimum(float(topk), n_valid)
    thr, n_ge = _select_threshold(count_ge, rmin, rmax, n_valid, kq)

    @pl.when(jnp.max(n_ge - kq) > 0.0)
    def _():
        def key_sum(fn):
            def body(c, a):
                off = pl.multiple_of(c * tk, tk)
                hit = fn(sc_ref[pl.ds(off, tk), :], (off + key_row).astype(F32))
                return a + jnp.sum(jnp.where(hit, 1.0, 0.0), axis=0, keepdims=True)
            return lax.fori_loop(0, nc, body, jnp.zeros((1, tq), F32))

        need = kq - key_sum(lambda blk, key: blk > thr)
        last = _last_tied_index(lambda j: key_sum(lambda blk, key: (blk == thr) & (key <= j)), need,
                                sc_ref.shape[0])

        def drop(c, carry):
            off = pl.multiple_of(c * tk, tk)
            blk = sc_ref[pl.ds(off, tk), :]
            beyond = (blk == thr) & ((off + key_row).astype(F32) > last)
            sc_ref[pl.ds(off, tk), :] = jnp.where(beyond, -jnp.inf, blk)
            return carry
        lax.fori_loop(0, nc, drop, 0)

    m_ref[...] = jnp.full(m_ref.shape, -jnp.inf, F32)
    l_ref[...] = jnp.zeros(l_ref.shape, F32)
    acc_ref[...] = jnp.zeros(acc_ref.shape, F32)

    def att_chunk(c, carry):
        off = pl.multiple_of(c * tk, tk)
        mask = sc_ref[pl.ds(off, tk), :] >= thr
        for h in range(n_heads):
            hs = slice(h * HEAD_DIM, (h + 1) * HEAD_DIM)
            s = lax.dot_general(k_ref[pl.ds(off, tk), hs], q_ref[:, hs], NT_DIMS,
                                preferred_element_type=F32)
            s = jnp.where(mask, s, NEG)
            m_prev = m_ref[h]
            m_new = jnp.maximum(m_prev, jnp.max(s, axis=0, keepdims=True))
            alpha = jnp.exp2(m_prev - m_new)
            p = jnp.exp2(s - m_new)
            l_ref[h] = alpha * l_ref[h] + jnp.sum(p, axis=0, keepdims=True)
            m_ref[h] = m_new
            pv = jnp.dot(vt_ref[c, hs, :], p.astype(BF16), preferred_element_type=F32)
            acc_ref[h] = alpha * acc_ref[h] + pv
        return carry

    chunk_loop(nc, att_chunk)

    for h in range(n_heads):
        hs = slice(h * HEAD_DIM, (h + 1) * HEAD_DIM)
        o_ref[:, hs] = (acc_ref[h] / l_ref[h]).T.astype(o_ref.dtype)


def attn_prompt(q, qi, wit, k, vt, ki, *, topk):
    b, s, aw = q.shape
    n_heads = aw // HEAD_DIM
    tq = ATTN_TQ
    tk = vt.shape[3]
    kern = functools.partial(_attn_prompt_kernel, tk=ATTN_TK, topk=topk, n_heads=n_heads)
    return pl.pallas_call(
        kern,
        grid=(b, s // tq),
        in_specs=[
            pl.BlockSpec((None, tq, aw), lambda bi, i: (bi, i, 0)),
            pl.BlockSpec((None, tq, qi.shape[2]), lambda bi, i: (bi, i, 0)),
            pl.BlockSpec((None, IDX_HEADS, tq), lambda bi, i: (bi, 0, i)),
            pl.BlockSpec((None, s, aw), lambda bi, i: (bi, 0, 0)),
            pl.BlockSpec((None, s // tk, aw, tk), lambda bi, i: (bi, 0, 0, 0)),
            pl.BlockSpec((None, s, IDX_DIM), lambda bi, i: (bi, 0, 0)),
        ],
        out_specs=pl.BlockSpec((None, tq, aw), lambda bi, i: (bi, i, 0)),
        out_shape=jax.ShapeDtypeStruct((b, s, aw), BF16),
        scratch_shapes=[
            pltpu.VMEM((s, tq), F32),
            pltpu.VMEM((n_heads, 1, tq), F32),
            pltpu.VMEM((n_heads, 1, tq), F32),
            pltpu.VMEM((n_heads, HEAD_DIM, tq), F32),
        ],
        compiler_params=_cparams(2),
        name="attn_prompt",
    )(q, qi, wit, k, vt, ki)


def _idx_sample_kernel(pt_ref, qi_ref, w_ref, *rest, n_pages, group, t_new, topk):
    cik_refs = rest[:group]
    kin_ref, sc_ref, thr_ref = rest[group:]
    b, p = pl.program_id(0), pl.program_id(1)
    db = sc_ref.shape[0]
    rows = qi_ref.shape[0]

    def scores(keys_bf16):
        s = lax.dot_general(qi_ref[...], keys_bf16, NT_DIMS, preferred_element_type=F32)
        r = jnp.maximum(s, 0.0) * w_ref[...]
        return jnp.sum(r.reshape(rows // t_new, t_new, LANES), axis=0)

    for g in range(group):
        sc_ref[b, p * group + g] = scores(cik_refs[g][...].astype(BF16))

    @pl.when(p == 0)
    def _():
        s = scores(kin_ref[...])
        tok = lax.broadcasted_iota(jnp.int32, (t_new, LANES), 0)
        lane = lax.broadcasted_iota(jnp.int32, (t_new, LANES), 1)
        sc_ref[b, n_pages] = jnp.where((lane <= tok) & (lane < t_new), s, -jnp.inf)

    @pl.when((b == db - 1) & (p == pl.num_programs(1) - 1))
    def _():
        def reduce_keys(fn, red):
            return red(red(fn(sc_ref[...]), axis=1), axis=2, keepdims=True)

        def count_ge(t):
            return reduce_keys(lambda x: jnp.where(x >= t[:, None], 1.0, 0.0), jnp.sum)

        rmax = reduce_keys(lambda x: x, jnp.max)
        rmin = reduce_keys(lambda x: jnp.where(x == -jnp.inf, jnp.inf, x), jnp.min)
        tokc = lax.broadcasted_iota(jnp.int32, (db, t_new, 1), 1)
        n_valid = (n_pages * LANES + tokc + 1).astype(F32)
        kq = jnp.minimum(float(topk), n_valid)
        thr, n_ge = _select_threshold(count_ge, rmin, rmax, n_valid, kq)
        thr_ref[...] = jnp.broadcast_to(thr, thr_ref.shape)

        @pl.when(jnp.max(n_ge - kq) > 0.0)
        def _():
            n_slabs = n_pages + 1
            key = (lax.broadcasted_iota(jnp.int32, (1, n_slabs, 1, LANES), 1) * LANES
                   + lax.broadcasted_iota(jnp.int32, (1, n_slabs, 1, LANES), 3)).astype(F32)
            thr4 = thr[:, None]

            def key_sum(hit):
                return jnp.sum(jnp.sum(jnp.where(hit, 1.0, 0.0), axis=1), axis=2, keepdims=True)

            need = kq - key_sum(sc_ref[...] > thr4)
            last = _last_tied_index(lambda j: key_sum((sc_ref[...] == thr4) & (key <= j[:, None])), need,
                                    n_slabs * LANES)
            sc = sc_ref[...]
            sc_ref[...] = jnp.where((sc == thr4) & (key > last[:, None]), -jnp.inf, sc)


def _page_map(b, p, pt, *, g, group):
    return (0, pt[b, p * group + g], 0, 0)


def idx_sample(page_table, qi_ht, w_ht, cache_idx_k, ki_new_pad, *, t_new, topk):
    db, n_pages = page_table.shape
    group = math.gcd(n_pages, IDX_PAGES_PER_STEP)
    rows = qi_ht.shape[1]
    page = cache_idx_k.shape[2]
    assert n_pages % group == 0
    kern = functools.partial(_idx_sample_kernel, n_pages=n_pages, group=group, t_new=t_new, topk=topk)
    page_specs = [pl.BlockSpec((None, None, page, IDX_DIM), functools.partial(_page_map, g=g, group=group))
                  for g in range(group)]
    grid_spec = pltpu.PrefetchScalarGridSpec(
        num_scalar_prefetch=1,
        grid=(db, n_pages // group),
        in_specs=[
            pl.BlockSpec((None, rows, IDX_DIM), lambda b, p, pt: (b, 0, 0)),
            pl.BlockSpec((None, rows, LANES), lambda b, p, pt: (b, 0, 0)),
            *page_specs,
            pl.BlockSpec((None, page, IDX_DIM), lambda b, p, pt: (b, 0, 0)),
        ],
        out_specs=[
            pl.BlockSpec((db, n_pages + 1, t_new, LANES), lambda b, p, pt: (0, 0, 0, 0)),
            pl.BlockSpec((db, t_new, LANES), lambda b, p, pt: (0, 0, 0)),
        ],
    )
    return pl.pallas_call(
        kern,
        grid_spec=grid_spec,
        out_shape=[jax.ShapeDtypeStruct((db, n_pages + 1, t_new, LANES), F32),
                   jax.ShapeDtypeStruct((db, t_new, LANES), F32)],
        compiler_params=_cparams(2),
        name="idx_sample",
    )(page_table, qi_ht, w_ht, *([cache_idx_k] * group), ki_new_pad)


def _attn_sample_kernel(pt_ref, q_ref, *rest, group, page, n_heads):
    ck_refs = rest[:group]
    cv_refs = rest[group:2 * group]
    kn_ref, vn_ref, sc_ref, scn_ref, thr_ref, o_ref, m_ref, l_ref, acc_ref = rest[2 * group:]
    p = pl.program_id(1)

    @pl.when(p == 0)
    def _():
        m_ref[...] = jnp.full(m_ref.shape, -jnp.inf, F32)
        l_ref[...] = jnp.zeros(l_ref.shape, F32)
        acc_ref[...] = jnp.zeros(acc_ref.shape, F32)

    thr = thr_ref[...]

    def head_rows(ref, h):
        return ref[pl.ds(h, page, stride=n_heads), :].astype(BF16)

    def attend(k_refs, v_refs, sc, sc_thr):
        n = len(k_refs)
        s = jnp.stack([
            lax.dot_general(q_ref[h], jnp.concatenate([head_rows(r, h) for r in k_refs], axis=0), NT_DIMS,
                            preferred_element_type=F32)
            for h in range(n_heads)])
        s = jnp.where(sc[None] >= sc_thr[None], s, NEG)
        m_prev = m_ref[...]
        m_new = jnp.maximum(m_prev, jnp.max(s, axis=2, keepdims=True))
        alpha = jnp.exp2(m_prev - m_new)
        pr = jnp.exp2(s - jnp.concatenate([m_new] * n, axis=2))
        l_ref[...] = alpha * l_ref[...] + jnp.sum(pr, axis=2, keepdims=True)
        m_ref[...] = m_new
        pb = pr.astype(BF16)
        for h in range(n_heads):
            vh = jnp.concatenate([head_rows(r, h) for r in v_refs], axis=0)
            acc_ref[h] = alpha[h] * acc_ref[h] + jnp.dot(pb[h], vh, preferred_element_type=F32)

    attend(ck_refs, cv_refs, jnp.concatenate([sc_ref[g] for g in range(group)], axis=1),
           jnp.concatenate([thr] * group, axis=1))

    @pl.when(p == pl.num_programs(1) - 1)
    def _():
        attend([kn_ref], [vn_ref], scn_ref[...], thr)
        o_ref[...] = acc_ref[...] / l_ref[...]


def attn_sample(page_table, q_ht, cache_k, cache_v, k_new_pad, v_new_pad, sc, thr):
    db, n_pages = page_table.shape
    n_heads, t_new = q_ht.shape[1], q_ht.shape[2]
    prow = cache_k.shape[2]
    page = prow // n_heads
    group = math.gcd(n_pages, KV_PAGES_PER_STEP)
    kern = functools.partial(_attn_sample_kernel, group=group, page=page, n_heads=n_heads)
    cache_specs = [pl.BlockSpec((None, None, prow, HEAD_DIM), functools.partial(_page_map, g=g, group=group))
                   for g in range(group)]
    new_spec = pl.BlockSpec((None, prow, HEAD_DIM), lambda b, p, pt: (b, 0, 0))
    head_spec = pl.BlockSpec((None, n_heads, t_new, HEAD_DIM), lambda b, p, pt: (b, 0, 0, 0))
    grid_spec = pltpu.PrefetchScalarGridSpec(
        num_scalar_prefetch=1,
        grid=(db, n_pages // group),
        in_specs=[
            head_spec,
            *cache_specs, *cache_specs, new_spec, new_spec,
            pl.BlockSpec((None, group, t_new, LANES), lambda b, p, pt: (b, p, 0, 0)),
            pl.BlockSpec((None, None, t_new, LANES), lambda b, p, pt: (b, n_pages, 0, 0)),
            pl.BlockSpec((None, t_new, LANES), lambda b, p, pt: (b, 0, 0)),
        ],
        out_specs=head_spec,
        scratch_shapes=[
            pltpu.VMEM((n_heads, t_new, LANES), F32),
            pltpu.VMEM((n_heads, t_new, LANES), F32),
            pltpu.VMEM((n_heads, t_new, HEAD_DIM), F32),
        ],
    )
    return pl.pallas_call(
        kern,
        grid_spec=grid_spec,
        out_shape=jax.ShapeDtypeStruct((db, n_heads, t_new, HEAD_DIM), F32),
        compiler_params=_cparams(2),
        name="attn_sample",
    )(page_table, q_ht, *([cache_k] * group), *([cache_v] * group), k_new_pad, v_new_pad, sc, sc, thr)


def _outproj_kernel(x_ref, yp_ref, ya_ref, wt_ref, wb_ref, g_ref, x2_ref, h2_ref):
    y = jnp.dot(yp_ref[...], wt_ref[...], preferred_element_type=F32)
    y = y + jnp.dot(ya_ref[...], wb_ref[...], preferred_element_type=F32)
    x2 = x_ref[...] + y
    x2_ref[...] = x2
    h2_ref[...] = _rms(x2, g_ref[...]).astype(h2_ref.dtype)


def outproj(x, yp, ya, w_top, w_bot, g, *, tm):
    m, d = x.shape
    kp, ka = yp.shape[1], ya.shape[1]
    return pl.pallas_call(
        _outproj_kernel,
        grid=(m // tm,),
        in_specs=[
            pl.BlockSpec((tm, d), lambda i: (i, 0)),
            pl.BlockSpec((tm, kp), lambda i: (i, 0)),
            pl.BlockSpec((tm, ka), lambda i: (i, 0)),
            pl.BlockSpec((kp, d), lambda i: (0, 0)),
            pl.BlockSpec((ka, d), lambda i: (0, 0)),
            pl.BlockSpec((1, d), lambda i: (0, 0)),
        ],
        out_specs=[pl.BlockSpec((tm, d), lambda i: (i, 0)), pl.BlockSpec((tm, d), lambda i: (i, 0))],
        out_shape=[jax.ShapeDtypeStruct((m, d), F32), jax.ShapeDtypeStruct((m, d), BF16)],
        compiler_params=_cparams(1),
        name="outproj",
    )(x, yp, ya, w_top, w_bot, g.reshape(1, d))


def _ffn_kernel(h2_ref, wg_ref, wu_ref, wd_ref, x2_ref, g_ref, o_ref, acc_ref):
    f = pl.program_id(1)

    @pl.when(f == 0)
    def _():
        acc_ref[...] = jnp.zeros(acc_ref.shape, F32)

    h2 = h2_ref[...]
    gate = jnp.dot(h2, wg_ref[...], preferred_element_type=F32)
    up = jnp.dot(h2, wu_ref[...], preferred_element_type=F32)
    a = (gate * jax.nn.sigmoid(gate) * up).astype(BF16)
    acc_ref[...] += jnp.dot(a, wd_ref[...], preferred_element_type=F32)

    @pl.when(f == pl.num_programs(1) - 1)
    def _():
        o_ref[...] = _rms(x2_ref[...] + acc_ref[...], g_ref[...])


def ffn(h2, wg, wu, wd, x2, g, *, tm, tf):
    m, d = h2.shape
    dff = wg.shape[1]
    return pl.pallas_call(
        _ffn_kernel,
        grid=(m // tm, dff // tf),
        in_specs=[
            pl.BlockSpec((tm, d), lambda i, f: (i, 0)),
            pl.BlockSpec((d, tf), lambda i, f: (0, f)),
            pl.BlockSpec((d, tf), lambda i, f: (0, f)),
            pl.BlockSpec((tf, d), lambda i, f: (f, 0)),
            pl.BlockSpec((tm, d), lambda i, f: (i, 0)),
            pl.BlockSpec((1, d), lambda i, f: (0, 0)),
        ],
        out_specs=pl.BlockSpec((tm, d), lambda i, f: (i, 0)),
        out_shape=jax.ShapeDtypeStruct((m, d), F32),
        scratch_shapes=[pltpu.VMEM((tm, d), F32)],
        compiler_params=_cparams(2),
        name="ffn",
    )(h2, wg, wu, wd, x2, g.reshape(1, d))


def _rope_tables(pos):
    half = HEAD_DIM // 2
    inv = ROPE_THETA ** (-jnp.arange(half, dtype=F32) / half)
    ang = pos.astype(F32)[:, None] * inv[None, :]
    cos, sin = jnp.cos(ang), jnp.sin(ang)
    return jnp.concatenate([cos, cos], axis=-1), jnp.concatenate([-sin, sin], axis=-1)


def _mix_inputs(x2d, pos_rows, g_mix, w_in16, w_tail, *, tm, vt_chunk=None):
    cos, sin = _rope_tables(pos_rows)
    return mixproj(x2d, g_mix, w_in16, w_tail, cos, sin, tm=tm, vt_chunk=vt_chunk)


def kernel(x_prompt, x_sample, cache_k, cache_v, cache_idx_k, state_pool, page_table, g_mix, w_in, w_pool,
           s_pool, w_out, g_ffn, w_gate, w_up, w_down, g_final):
    B, S, D = x_prompt.shape
    DB, T, _ = x_sample.shape
    depth = w_in.shape[0]
    assert depth == 1
    n_pages = page_table.shape[1]
    page = cache_k.shape[2]
    n_heads = cache_k.shape[3]
    aw = n_heads * HEAD_DIM
    pw = w_pool.shape[1] * w_pool.shape[2]
    past = n_pages * page
    l = 0

    wide = pw + 3 * aw + IDX_HEADS * IDX_DIM
    assert (pw, aw, wide) == (PROJ_TN, PROJ_TN, 6 * PROJ_TN) and w_in.shape[2] == wide + IDX_DIM + IDX_HEADS
    w_in16 = w_in[l].astype(BF16)
    w_tail = jnp.pad(w_in16[:, wide:], ((0, 0), (0, LANES - IDX_HEADS)))
    w_pool16 = w_pool[l].astype(BF16)
    w_out16 = w_out[l].astype(BF16)
    w_top, w_bot = w_out16[:pw], w_out16[pw:]
    wg16, wu16, wd16 = w_gate[l].astype(BF16), w_up[l].astype(BF16), w_down[l].astype(BF16)

    xp2 = x_prompt.reshape(B * S, D)
    pos_p = jnp.arange(S)
    tk = ATTN_TK
    up, q, k32, k16, v32, qi, ki32, ki16, wi, vt = _mix_inputs(xp2, pos_p, g_mix[l], w_in16, w_tail, tm=512,
                                                               vt_chunk=tk)
    up3 = up.reshape(B, S, pw)
    yp = pool_mixer(up3, jnp.zeros((B, HALO, pw), F32), w_pool16, s_pool[l], tm=512, pos0=0)
    wit = wi.reshape(B, S, LANES)[:, :, :IDX_HEADS].transpose(0, 2, 1)
    ya = attn_prompt(q.reshape(B, S, aw), qi.reshape(B, S, -1), wit, k16.reshape(B, S, aw),
                     vt.reshape(B, S // tk, aw, tk), ki16.reshape(B, S, IDX_DIM), topk=min(TOPK_MAX, S // 4))
    x2, h2 = outproj(xp2, yp.reshape(B * S, pw), ya.reshape(B * S, aw), w_top, w_bot, g_ffn[l], tm=512)
    y_prompt = ffn(h2, wg16, wu16, wd16, x2, g_final, tm=512, tf=512).reshape(B, S, D)

    k_prompt = k32.reshape(1, B, S, n_heads, HEAD_DIM)
    v_prompt = v32.reshape(1, B, S, n_heads, HEAD_DIM)
    idx_k_prompt = ki32.reshape(1, B, S, IDX_DIM)
    pool_prompt = up3[:, S - POOL_STATE:, :][None]

    M = DB * T
    xs2 = x_sample.reshape(M, D)
    pos_s = jnp.tile(past + jnp.arange(T), DB)
    ups, qs, ks32, _, vs32, qis, kis32, kis16, wis = _mix_inputs(xs2, pos_s, g_mix[l], w_in16, w_tail, tm=M)
    ups3 = ups.reshape(DB, T, pw)
    prefix16 = jnp.pad(state_pool[l], ((0, 0), (HALO - POOL_STATE, 0), (0, 0)))
    yps = pool_mixer(ups3, prefix16, w_pool16, s_pool[l], tm=T, pos0=past)

    qi_ht = qis.reshape(DB, T, IDX_HEADS, IDX_DIM).transpose(0, 2, 1, 3).reshape(DB, IDX_HEADS * T, IDX_DIM)
    w_ht = wis.reshape(DB, T, LANES)[:, :, :IDX_HEADS].transpose(0, 2, 1).reshape(DB, IDX_HEADS * T, 1)
    w_ht = jnp.broadcast_to(w_ht * (IDX_DIM ** -0.5), (DB, IDX_HEADS * T, LANES))
    pad_rows = lambda a: jnp.pad(a.reshape(DB, T, -1), ((0, 0), (0, page - T), (0, 0)))
    sc, thr = idx_sample(page_table, qi_ht, w_ht, cache_idx_k, pad_rows(kis16), t_new=T,
                         topk=min(TOPK_MAX, (past + T) // 4))

    n_phys = cache_k.shape[1]
    key_head_rows = lambda a: pad_rows(a).reshape(DB, page * n_heads, HEAD_DIM)
    q_ht = qs.reshape(DB, T, n_heads, HEAD_DIM).transpose(0, 2, 1, 3)
    o_ht = attn_sample(page_table, q_ht, cache_k.reshape(depth, n_phys, page * n_heads, HEAD_DIM),
                       cache_v.reshape(depth, n_phys, page * n_heads, HEAD_DIM),
                       key_head_rows(ks32), key_head_rows(vs32), sc, thr)
    yas = o_ht.transpose(0, 2, 1, 3).reshape(M, aw).astype(BF16)

    x2s, h2s = outproj(xs2, yps.reshape(M, pw), yas, w_top, w_bot, g_ffn[l], tm=M)
    y_sample = ffn(h2s, wg16, wu16, wd16, x2s, g_final, tm=M, tf=512).reshape(DB, T, D)

    k_sample = ks32.reshape(1, DB, T, n_heads, HEAD_DIM)
    v_sample = vs32.reshape(1, DB, T, n_heads, HEAD_DIM)
    idx_k_sample = kis32.reshape(1, DB, T, IDX_DIM)
    pool_sample = jnp.concatenate([state_pool[l][:, T:, :], ups3], axis=1)[None]

    return (y_prompt, y_sample, k_prompt, v_prompt, idx_k_prompt, pool_prompt,
            k_sample, v_sample, idx_k_sample, pool_sample)
```

```python
import functools
import math

import jax
import jax.numpy as jnp
from jax import lax
from jax.experimental import pallas as pl
from jax.experimental.pallas import tpu as pltpu

F32 = jnp.float32
BF16 = jnp.bfloat16

LANES = 128
HEAD_DIM = 128
IDX_DIM = 128
IDX_HEADS = 16
POOL_GROUPS = 4
POOL_WINDOWS = (2, 4, 8, 16)
POOL_STATE = max(POOL_WINDOWS) - 1
HALO = 16
TOPK_MAX = 256
ATTN_TQ = 256
ATTN_TK = 128
CHUNK_TRIP_FACTOR = 4
IDX_PAGES_PER_STEP = 32
KV_PAGES_PER_STEP = 16
ROPE_THETA = 10000.0
EPS = 1e-6
NEG = -1e30
MAX_SEARCH_STEPS = 1024
LOG2E = 1.4426950408889634
VMEM_LIMIT = 56 * 1024 * 1024

NT_DIMS = (((1,), (1,)), ((), ()))


def _cparams(n_grid):
    return pltpu.CompilerParams(dimension_semantics=("arbitrary",) * n_grid, vmem_limit_bytes=VMEM_LIMIT)


def _rms(x, g):
    return x * lax.rsqrt(jnp.mean(x * x, axis=-1, keepdims=True) + EPS) * g


PROJ_TN = 1024
PROJ_PIECE = 256


def _rope(y, cos, sin):
    return [y[:, g * HEAD_DIM:(g + 1) * HEAD_DIM] * cos
            + pltpu.roll(y[:, g * HEAD_DIM:(g + 1) * HEAD_DIM], HEAD_DIM // 2, axis=1) * sin
            for g in range(y.shape[1] // HEAD_DIM)]


def _mixproj_kernel(x_ref, g_ref, w_ref, wt_ref, cos_ref, sin_ref,
                    up_ref, q_ref, k32_ref, k16_ref, v32_ref, qi_ref, ki32_ref, ki16_ref, wi_ref, *rest):
    maybe_vt_ref, h_ref = rest if len(rest) == 2 else (None, rest[0])
    j = pl.program_id(1)

    def heads_to(pieces, refs, col0=0):
        for g, r in enumerate(pieces):
            for o_ref in refs:
                o_ref[:, col0 + g * HEAD_DIM:col0 + (g + 1) * HEAD_DIM] = r.astype(o_ref.dtype)

    @pl.when(j == 0)
    def _():
        h_ref[...] = _rms(x_ref[...], g_ref[...]).astype(h_ref.dtype)
        t = jnp.dot(h_ref[...], wt_ref[...], preferred_element_type=F32)
        heads_to(_rope(t[:, :IDX_DIM], cos_ref[...], sin_ref[...]), (ki32_ref, ki16_ref))
        wi_ref[...] = t[:, IDX_DIM:] * (IDX_HEADS ** -0.5)

    def col_pieces(fn):
        for c0 in range(0, PROJ_TN, PROJ_PIECE):
            fn(c0, jnp.dot(h_ref[...], w_ref[:, c0:c0 + PROJ_PIECE], preferred_element_type=F32))

    def rope_to(refs, col_base=0, scale=None):
        def fn(c0, y):
            y = y if scale is None else y * scale
            heads_to(_rope(y, cos_ref[...], sin_ref[...]), refs, col0=col_base + c0)
        return fn

    @pl.when(j == 0)
    def _():
        def fn(c0, y):
            up_ref[:, c0:c0 + PROJ_PIECE] = y
        col_pieces(fn)

    @pl.when(j == 1)
    def _():
        col_pieces(rope_to((q_ref,), scale=HEAD_DIM ** -0.5 * LOG2E))

    @pl.when(j == 2)
    def _():
        col_pieces(rope_to((k32_ref, k16_ref)))

    @pl.when(j == 3)
    def _():
        def fn(c0, y):
            v32_ref[:, c0:c0 + PROJ_PIECE] = y
            if maybe_vt_ref is not None:
                n_chunks, _, ck = maybe_vt_ref.shape
                for c in range(n_chunks):
                    for g in range(PROJ_PIECE // HEAD_DIM):
                        hs = slice(g * HEAD_DIM, (g + 1) * HEAD_DIM)
                        maybe_vt_ref[c, c0 + g * HEAD_DIM:c0 + (g + 1) * HEAD_DIM, :] = (
                            y[c * ck:(c + 1) * ck, hs].T.astype(maybe_vt_ref.dtype))
        col_pieces(fn)

    for t_qi in range(2):
        @pl.when(j == 4 + t_qi)
        def _():
            col_pieces(rope_to((qi_ref,), col_base=t_qi * PROJ_TN))


def mixproj(x, g, w_in16, w_tail, cos, sin, *, tm, vt_chunk=None):
    m, d = x.shape
    tn = PROJ_TN
    n_tiles = 6
    row = lambda width: pl.BlockSpec((tm, width), lambda i, j: (i, 0))
    rope_tiles = cos.shape[0] // tm
    rope_spec = pl.BlockSpec((tm, HEAD_DIM), lambda i, j: (i % rope_tiles, 0))
    widths = (tn, tn, tn, tn, tn, 2 * tn, IDX_DIM, IDX_DIM, LANES)
    dtypes = (F32, BF16, F32, BF16, F32, BF16, F32, BF16, F32)
    out_specs = [row(w) for w in widths]
    out_shape = [jax.ShapeDtypeStruct((m, w), dt) for w, dt in zip(widths, dtypes)]
    if vt_chunk is not None:
        out_specs.append(pl.BlockSpec((tm // vt_chunk, tn, vt_chunk), lambda i, j: (i, 0, 0)))
        out_shape.append(jax.ShapeDtypeStruct((m // vt_chunk, tn, vt_chunk), BF16))
    return pl.pallas_call(
        _mixproj_kernel,
        grid=(m // tm, n_tiles),
        in_specs=[
            row(d),
            pl.BlockSpec((1, d), lambda i, j: (0, 0)),
            pl.BlockSpec((d, tn), lambda i, j: (0, j)),
            pl.BlockSpec(w_tail.shape, lambda i, j: (0, 0)),
            rope_spec, rope_spec,
        ],
        out_specs=out_specs,
        out_shape=out_shape,
        scratch_shapes=[pltpu.VMEM((tm, d), BF16)],
        compiler_params=_cparams(2),
        name="mixproj",
    )(x, g.reshape(1, d), w_in16, w_tail, cos, sin)


def _pool_kernel(*refs, tm, pos0, has_prev):
    if has_prev:
        up_ref, prev_ref, prefix_ref, w_ref, s_ref, o_ref, ext_ref = refs
    else:
        up_ref, prefix_ref, w_ref, s_ref, o_ref, ext_ref = refs
    i = pl.program_id(1)
    if has_prev:
        halo = jnp.where(i == 0, prefix_ref[...], prev_ref[...])
    else:
        halo = prefix_ref[...]
    ext_ref[0:HALO, :] = halo
    ext_ref[HALO:HALO + tm, :] = up_ref[...]
    pos = pos0 + i * tm + lax.broadcasted_iota(jnp.int32, (tm, 1), 0)
    gw = up_ref.shape[1] // POOL_GROUPS
    for g, w in enumerate(POOL_WINDOWS):
        sl = slice(g * gw, (g + 1) * gw)
        cur = ext_ref[HALO:HALO + tm, sl]
        s = cur
        for j in range(1, w):
            s = s + ext_ref[HALO - j:HALO - j + tm, sl]
        cnt = jnp.minimum(w, pos + 1).astype(F32)
        d = s / cnt - cur
        y = jnp.dot(d.astype(BF16), w_ref[g], preferred_element_type=F32) * s_ref[:, sl]
        o_ref[:, sl] = y.astype(o_ref.dtype)


def pool_mixer(up, prefix16, w_pool, s_pool, *, tm, pos0):
    b, t, wd = up.shape
    has_prev = t > tm
    gw = wd // POOL_GROUPS
    in_specs = [pl.BlockSpec((None, tm, wd), lambda bi, i: (bi, i, 0))]
    args = [up]
    if has_prev:
        r = tm // HALO
        in_specs.append(pl.BlockSpec((None, HALO, wd), lambda bi, i: (bi, jnp.maximum(i * r - 1, 0), 0)))
        args.append(up)
    in_specs += [
        pl.BlockSpec((None, HALO, wd), lambda bi, i: (bi, 0, 0)),
        pl.BlockSpec((POOL_GROUPS, gw, gw), lambda bi, i: (0, 0, 0)),
        pl.BlockSpec((1, wd), lambda bi, i: (0, 0)),
    ]
    args += [prefix16, w_pool, s_pool.reshape(1, wd)]
    return pl.pallas_call(
        functools.partial(_pool_kernel, tm=tm, pos0=pos0, has_prev=has_prev),
        grid=(b, t // tm),
        in_specs=in_specs,
        out_specs=pl.BlockSpec((None, tm, wd), lambda bi, i: (bi, i, 0)),
        out_shape=jax.ShapeDtypeStruct((b, t, wd), BF16),
        scratch_shapes=[pltpu.VMEM((HALO + tm, wd), F32)],
        compiler_params=_cparams(2),
        name="pool_mixer",
    )(*args)


PEEL_BRACKET = 8


def _select_threshold(count_ge, max_below, rmin, rmax, n_valid, kq):
    def any_row(pred):
        return jnp.max(jnp.where(pred, 1.0, 0.0))

    def halve(lo, hi, cl, ch):
        mid = lo + 0.5 * (hi - lo)
        stuck = (mid <= lo) | (mid >= hi)
        c = count_ge(mid)
        ge = c >= kq
        return (jnp.where(ge, mid, lo), jnp.where(ge, hi, mid), jnp.where(ge, c, cl), jnp.where(ge, ch, c),
                jnp.where(stuck, 1.0, 0.0))

    def halving_loop(st, bracket_cap):
        def open_rows(cl, ch, stuck):
            return any_row((cl != kq) & (stuck <= 0.0) & (cl - ch > bracket_cap))

        def cond(st):
            return jnp.logical_and(st[0] < MAX_SEARCH_STEPS, st[1] > 0.0)

        def body(st):
            steps, _, lo, hi, cl, ch, _ = st
            lo, hi, cl, ch, stuck = halve(lo, hi, cl, ch)
            return steps + 1, open_rows(cl, ch, stuck), lo, hi, cl, ch, stuck

        st = (st[0], open_rows(st[4], st[5], st[6])) + tuple(st[2:])
        return lax.while_loop(cond, body, st)

    hi0 = rmax + jnp.maximum(jnp.abs(rmax) * 1e-6, 1e-30)
    zero = jnp.zeros_like(rmax)
    st = halving_loop((jnp.int32(0), jnp.float32(0.0), rmin, hi0, n_valid, zero, zero), float(PEEL_BRACKET))
    _, _, lo, hi, cl, ch, stuck = st

    walk = (cl != kq) & (stuck <= 0.0)

    def walk_cond(st):
        return jnp.logical_and(st[0] < PEEL_BRACKET, st[1] > 0.0)

    def walk_body(st):
        steps, _, top, n_top = st
        active = walk & (n_top < kq)
        top = jnp.where(active, max_below(top), top)
        n_top = jnp.where(active, n_top + 1.0, n_top)
        return steps + 1, any_row(walk & (n_top < kq)), top, n_top

    _, _, top, _ = lax.while_loop(walk_cond, walk_body, (jnp.int32(0), any_row(walk), hi, ch))
    thr = jnp.where(walk, top, lo)
    n_ge = count_ge(thr)

    redo = n_ge != kq
    st = halving_loop((jnp.int32(0), jnp.float32(0.0), lo, hi, cl, ch, jnp.where(redo, stuck, 1.0)), 0.0)
    return jnp.where(redo, st[2], thr), jnp.where(redo, st[4], n_ge)


def _last_tied_index(count_tied_upto, need, n_keys):
    def body(_, st):
        lo, hi = st
        mid = jnp.floor((lo + hi) * 0.5)
        ok = count_tied_upto(mid) >= need
        return jnp.where(ok, lo, mid), jnp.where(ok, mid, hi)

    steps = max(1, math.ceil(math.log2(n_keys + 1)))
    _, hi = lax.fori_loop(0, steps, body, (jnp.full(need.shape, -1.0, F32), jnp.full(need.shape, n_keys - 1.0, F32)))
    return hi


def _attn_prompt_kernel(q_ref, qi_ref, wit_ref, k_ref, vt_ref, ki_ref, o_ref,
                        sc_ref, ext_ref, m_ref, l_ref, acc_ref, *, tk, topk, n_heads):
    i = pl.program_id(1)
    tq = q_ref.shape[0]
    nc = ((i + 1) * tq + tk - 1) // tk
    qpos = i * tq + lax.broadcasted_iota(jnp.int32, (1, tq), 1)
    key_row = lax.broadcasted_iota(jnp.int32, (tk, tq), 0)
    w_rows = wit_ref[...] * (IDX_DIM ** -0.5)

    def idx_chunk(c, carry):
        off = pl.multiple_of(c * tk, tk)
        kic = ki_ref[pl.ds(off, tk), :]
        acc = jnp.zeros((tk, tq), F32)
        for h in range(IDX_HEADS):
            s = lax.dot_general(kic, qi_ref[:, h * IDX_DIM:(h + 1) * IDX_DIM], NT_DIMS,
                                preferred_element_type=F32)
            acc = acc + jnp.maximum(s, 0.0) * w_rows[h:h + 1, :]
        causal = off + key_row <= qpos
        sc = jnp.where(causal, acc, -jnp.inf)
        sc_ref[pl.ds(off, tk), :] = sc
        ext_ref[0] = jnp.minimum(ext_ref[0], jnp.min(jnp.where(causal, acc, jnp.inf).reshape(tk // 8, 8, tq), axis=0))
        ext_ref[1] = jnp.maximum(ext_ref[1], jnp.max(sc.reshape(tk // 8, 8, tq), axis=0))
        return carry

    ext_ref[0] = jnp.full((8, tq), jnp.inf, F32)
    ext_ref[1] = jnp.full((8, tq), -jnp.inf, F32)

    def chunk_loop(n_chunks, chunk_fn):
        base = tq // tk
        per_trip = CHUNK_TRIP_FACTOR * base

        def run(first, count):
            for u in range(count):
                chunk_fn(first + u, 0)

        def body(t, carry):
            run(t * per_trip, per_trip)
            return carry
        lax.fori_loop(0, n_chunks // per_trip, body, 0)
        for r in range(1, CHUNK_TRIP_FACTOR):
            @pl.when(n_chunks % per_trip == r * base)
            def _():
                run((n_chunks // per_trip) * per_trip, r * base)

    chunk_loop(nc, idx_chunk)

    lanes_par = 4

    def reduce_keys(fn, init, red):
        rows = tq

        def body(c, a):
            blk = sc_ref[pl.ds(pl.multiple_of(c * rows, rows), rows), :]
            return fn(a, blk.reshape(lanes_par, rows // (8 * lanes_par), 8, tq))
        a = lax.fori_loop(0, (nc * tk) // rows, body, jnp.full((lanes_par, 8, tq), init, F32))
        return red(red(a, axis=0), axis=0, keepdims=True)

    def count_ge(t):
        return reduce_keys(lambda a, blk: a + jnp.sum(jnp.where(blk >= t, 1.0, 0.0), axis=1), 0.0, jnp.sum)

    def max_below(t):
        return reduce_keys(lambda a, blk: jnp.maximum(a, jnp.max(jnp.where(blk < t, blk, -jnp.inf), axis=1)),
                           -jnp.inf, jnp.max)

    rmin = jnp.min(ext_ref[0], axis=0, keepdims=True)
    rmax = jnp.max(ext_ref[1], axis=0, keepdims=True)
    n_valid = (qpos + 1).astype(F32)
    kq = jnp.minimum(float(topk), n_valid)
    thr, n_ge = _select_threshold(count_ge, max_below, rmin, rmax, n_valid, kq)

    @pl.when(jnp.max(n_ge - kq) > 0.0)
    def _():
        def key_sum(fn):
            def body(c, a):
                off = pl.multiple_of(c * tk, tk)
                hit = fn(sc_ref[pl.ds(off, tk), :], (off + key_row).astype(F32))
                return a + jnp.sum(jnp.where(hit, 1.0, 0.0), axis=0, keepdims=True)
            return lax.fori_loop(0, nc, body, jnp.zeros((1, tq), F32))

        need = kq - key_sum(lambda blk, key: blk > thr)
        last = _last_tied_index(lambda j: key_sum(lambda blk, key: (blk == thr) & (key <= j)), need,
                                sc_ref.shape[0])

        def drop(c, carry):
            off = pl.multiple_of(c * tk, tk)
            blk = sc_ref[pl.ds(off, tk), :]
            beyond = (blk == thr) & ((off + key_row).astype(F32) > last)
            sc_ref[pl.ds(off, tk), :] = jnp.where(beyond, -jnp.inf, blk)
            return carry
        lax.fori_loop(0, nc, drop, 0)

    m_ref[...] = jnp.full(m_ref.shape, -jnp.inf, F32)
    l_ref[...] = jnp.zeros(l_ref.shape, F32)
    acc_ref[...] = jnp.zeros(acc_ref.shape, F32)

    def att_chunk(c, carry):
        off = pl.multiple_of(c * tk, tk)
        mask = sc_ref[pl.ds(off, tk), :] >= thr
        for h in range(n_heads):
            hs = slice(h * HEAD_DIM, (h + 1) * HEAD_DIM)
            s = lax.dot_general(k_ref[pl.ds(off, tk), hs], q_ref[:, hs], NT_DIMS,
                                preferred_element_type=F32)
            s = jnp.where(mask, s, NEG)
            m_prev = m_ref[h]
            m_new = jnp.maximum(m_prev, jnp.max(s, axis=0, keepdims=True))
            alpha = jnp.exp2(m_prev - m_new)
            p = jnp.exp2(s - m_new)
            l_ref[h] = alpha * l_ref[h] + jnp.sum(p, axis=0, keepdims=True)
            m_ref[h] = m_new
            pv = jnp.dot(vt_ref[c, hs, :], p.astype(BF16), preferred_element_type=F32)
            acc_ref[h] = alpha * acc_ref[h] + pv
        return carry

    chunk_loop(nc, att_chunk)

    for h in range(n_heads):
        hs = slice(h * HEAD_DIM, (h + 1) * HEAD_DIM)
        o_ref[:, hs] = (acc_ref[h] / l_ref[h]).T.astype(o_ref.dtype)


def attn_prompt(q, qi, wit, k, vt, ki, *, topk):
    b, s, aw = q.shape
    n_heads = aw // HEAD_DIM
    tq = ATTN_TQ
    tk = vt.shape[3]
    kern = functools.partial(_attn_prompt_kernel, tk=ATTN_TK, topk=topk, n_heads=n_heads)
    return pl.pallas_call(
        kern,
        grid=(b, s // tq),
        in_specs=[
            pl.BlockSpec((None, tq, aw), lambda bi, i: (bi, i, 0)),
            pl.BlockSpec((None, tq, qi.shape[2]), lambda bi, i: (bi, i, 0)),
            pl.BlockSpec((None, IDX_HEADS, tq), lambda bi, i: (bi, 0, i)),
            pl.BlockSpec((None, s, aw), lambda bi, i: (bi, 0, 0)),
            pl.BlockSpec((None, s // tk, aw, tk), lambda bi, i: (bi, 0, 0, 0)),
            pl.BlockSpec((None, s, IDX_DIM), lambda bi, i: (bi, 0, 0)),
        ],
        out_specs=pl.BlockSpec((None, tq, aw), lambda bi, i: (bi, i, 0)),
        out_shape=jax.ShapeDtypeStruct((b, s, aw), BF16),
        scratch_shapes=[
            pltpu.VMEM((s, tq), F32),
            pltpu.VMEM((2, 8, tq), F32),
            pltpu.VMEM((n_heads, 1, tq), F32),
            pltpu.VMEM((n_heads, 1, tq), F32),
            pltpu.VMEM((n_heads, HEAD_DIM, tq), F32),
        ],
        compiler_params=_cparams(2),
        name="attn_prompt",
    )(q, qi, wit, k, vt, ki)


def _idx_sample_kernel(pt_ref, qi_ref, w_ref, *rest, n_pages, group, t_new, topk):
    cik_refs = rest[:group]
    kin_ref, sc_ref, thr_ref = rest[group:]
    b, p = pl.program_id(0), pl.program_id(1)
    db = sc_ref.shape[0]
    rows = qi_ref.shape[0]

    def scores(keys_bf16):
        s = lax.dot_general(qi_ref[...], keys_bf16, NT_DIMS, preferred_element_type=F32)
        r = jnp.maximum(s, 0.0) * w_ref[...]
        return jnp.sum(r.reshape(rows // t_new, t_new, LANES), axis=0)

    for g in range(group):
        sc_ref[b, p * group + g] = scores(cik_refs[g][...].astype(BF16))

    @pl.when(p == 0)
    def _():
        s = scores(kin_ref[...])
        tok = lax.broadcasted_iota(jnp.int32, (t_new, LANES), 0)
        lane = lax.broadcasted_iota(jnp.int32, (t_new, LANES), 1)
        sc_ref[b, n_pages] = jnp.where((lane <= tok) & (lane < t_new), s, -jnp.inf)

    @pl.when((b == db - 1) & (p == pl.num_programs(1) - 1))
    def _():
        def reduce_keys(fn, red):
            return red(red(fn(sc_ref[...]), axis=1), axis=2, keepdims=True)

        def count_ge(t):
            return reduce_keys(lambda x: jnp.where(x >= t[:, None], 1.0, 0.0), jnp.sum)

        def max_below(t):
            return reduce_keys(lambda x: jnp.where(x < t[:, None], x, -jnp.inf), jnp.max)

        rmax = reduce_keys(lambda x: x, jnp.max)
        rmin = reduce_keys(lambda x: jnp.where(x == -jnp.inf, jnp.inf, x), jnp.min)
        tokc = lax.broadcasted_iota(jnp.int32, (db, t_new, 1), 1)
        n_valid = (n_pages * LANES + tokc + 1).astype(F32)
        kq = jnp.minimum(float(topk), n_valid)
        thr, n_ge = _select_threshold(count_ge, max_below, rmin, rmax, n_valid, kq)
        thr_ref[...] = jnp.broadcast_to(thr, thr_ref.shape)

        @pl.when(jnp.max(n_ge - kq) > 0.0)
        def _():
            n_slabs = n_pages + 1
            key = (lax.broadcasted_iota(jnp.int32, (1, n_slabs, 1, LANES), 1) * LANES
                   + lax.broadcasted_iota(jnp.int32, (1, n_slabs, 1, LANES), 3)).astype(F32)
            thr4 = thr[:, None]

            def key_sum(hit):
                return jnp.sum(jnp.sum(jnp.where(hit, 1.0, 0.0), axis=1), axis=2, keepdims=True)

            need = kq - key_sum(sc_ref[...] > thr4)
            last = _last_tied_index(lambda j: key_sum((sc_ref[...] == thr4) & (key <= j[:, None])), need,
                                    n_slabs * LANES)
            sc = sc_ref[...]
            sc_ref[...] = jnp.where((sc == thr4) & (key > last[:, None]), -jnp.inf, sc)


def _page_map(b, p, pt, *, g, group):
    return (0, pt[b, p * group + g], 0, 0)


def idx_sample(page_table, qi_ht, w_ht, cache_idx_k, ki_new_pad, *, t_new, topk):
    db, n_pages = page_table.shape
    group = math.gcd(n_pages, IDX_PAGES_PER_STEP)
    rows = qi_ht.shape[1]
    page = cache_idx_k.shape[2]
    assert n_pages % group == 0
    kern = functools.partial(_idx_sample_kernel, n_pages=n_pages, group=group, t_new=t_new, topk=topk)
    page_specs = [pl.BlockSpec((None, None, page, IDX_DIM), functools.partial(_page_map, g=g, group=group))
                  for g in range(group)]
    grid_spec = pltpu.PrefetchScalarGridSpec(
        num_scalar_prefetch=1,
        grid=(db, n_pages // group),
        in_specs=[
            pl.BlockSpec((None, rows, IDX_DIM), lambda b, p, pt: (b, 0, 0)),
            pl.BlockSpec((None, rows, LANES), lambda b, p, pt: (b, 0, 0)),
            *page_specs,
            pl.BlockSpec((None, page, IDX_DIM), lambda b, p, pt: (b, 0, 0)),
        ],
        out_specs=[
            pl.BlockSpec((db, n_pages + 1, t_new, LANES), lambda b, p, pt: (0, 0, 0, 0)),
            pl.BlockSpec((db, t_new, LANES), lambda b, p, pt: (0, 0, 0)),
        ],
    )
    return pl.pallas_call(
        kern,
        grid_spec=grid_spec,
        out_shape=[jax.ShapeDtypeStruct((db, n_pages + 1, t_new, LANES), F32),
                   jax.ShapeDtypeStruct((db, t_new, LANES), F32)],
        compiler_params=_cparams(2),
        name="idx_sample",
    )(page_table, qi_ht, w_ht, *([cache_idx_k] * group), ki_new_pad)


def _attn_sample_kernel(pt_ref, q_ref, *rest, group, page, n_heads):
    ck_refs = rest[:group]
    cv_refs = rest[group:2 * group]
    kn_ref, vn_ref, sc_ref, scn_ref, thr_ref, o_ref, m_ref, l_ref, acc_ref = rest[2 * group:]
    p = pl.program_id(1)

    @pl.when(p == 0)
    def _():
        m_ref[...] = jnp.full(m_ref.shape, -jnp.inf, F32)
        l_ref[...] = jnp.zeros(l_ref.shape, F32)
        acc_ref[...] = jnp.zeros(acc_ref.shape, F32)

    thr = thr_ref[...]

    def head_rows(ref, h):
        return ref[pl.ds(h, page, stride=n_heads), :].astype(BF16)

    def attend(k_refs, v_refs, sc, sc_thr):
        n = len(k_refs)
        s = jnp.stack([
            lax.dot_general(q_ref[h], jnp.concatenate([head_rows(r, h) for r in k_refs], axis=0), NT_DIMS,
                            preferred_element_type=F32)
            for h in range(n_heads)])
        s = jnp.where(sc[None] >= sc_thr[None], s, NEG)
        m_prev = m_ref[...]
        m_new = jnp.maximum(m_prev, jnp.max(s, axis=2, keepdims=True))
        alpha = jnp.exp2(m_prev - m_new)
        pr = jnp.exp2(s - jnp.concatenate([m_new] * n, axis=2))
        l_ref[...] = alpha * l_ref[...] + jnp.sum(pr, axis=2, keepdims=True)
        m_ref[...] = m_new
        pb = pr.astype(BF16)
        for h in range(n_heads):
            vh = jnp.concatenate([head_rows(r, h) for r in v_refs], axis=0)
            acc_ref[h] = alpha[h] * acc_ref[h] + jnp.dot(pb[h], vh, preferred_element_type=F32)

    attend(ck_refs, cv_refs, jnp.concatenate([sc_ref[g] for g in range(group)], axis=1),
           jnp.concatenate([thr] * group, axis=1))

    @pl.when(p == pl.num_programs(1) - 1)
    def _():
        attend([kn_ref], [vn_ref], scn_ref[...], thr)
        o_ref[...] = acc_ref[...] / l_ref[...]


def attn_sample(page_table, q_ht, cache_k, cache_v, k_new_pad, v_new_pad, sc, thr):
    db, n_pages = page_table.shape
    n_heads, t_new = q_ht.shape[1], q_ht.shape[2]
    prow = cache_k.shape[2]
    page = prow // n_heads
    group = math.gcd(n_pages, KV_PAGES_PER_STEP)
    kern = functools.partial(_attn_sample_kernel, group=group, page=page, n_heads=n_heads)
    cache_specs = [pl.BlockSpec((None, None, prow, HEAD_DIM), functools.partial(_page_map, g=g, group=group))
                   for g in range(group)]
    new_spec = pl.BlockSpec((None, prow, HEAD_DIM), lambda b, p, pt: (b, 0, 0))
    head_spec = pl.BlockSpec((None, n_heads, t_new, HEAD_DIM), lambda b, p, pt: (b, 0, 0, 0))
    grid_spec = pltpu.PrefetchScalarGridSpec(
        num_scalar_prefetch=1,
        grid=(db, n_pages // group),
        in_specs=[
            head_spec,
            *cache_specs, *cache_specs, new_spec, new_spec,
            pl.BlockSpec((None, group, t_new, LANES), lambda b, p, pt: (b, p, 0, 0)),
            pl.BlockSpec((None, None, t_new, LANES), lambda b, p, pt: (b, n_pages, 0, 0)),
            pl.BlockSpec((None, t_new, LANES), lambda b, p, pt: (b, 0, 0)),
        ],
        out_specs=head_spec,
        scratch_shapes=[
            pltpu.VMEM((n_heads, t_new, LANES), F32),
            pltpu.VMEM((n_heads, t_new, LANES), F32),
            pltpu.VMEM((n_heads, t_new, HEAD_DIM), F32),
        ],
    )
    return pl.pallas_call(
        kern,
        grid_spec=grid_spec,
        out_shape=jax.ShapeDtypeStruct((db, n_heads, t_new, HEAD_DIM), F32),
        compiler_params=_cparams(2),
        name="attn_sample",
    )(page_table, q_ht, *([cache_k] * group), *([cache_v] * group), k_new_pad, v_new_pad, sc, sc, thr)


def _outproj_kernel(x_ref, yp_ref, ya_ref, wt_ref, wb_ref, g_ref, x2_ref, h2_ref):
    y = jnp.dot(yp_ref[...], wt_ref[...], preferred_element_type=F32)
    y = y + jnp.dot(ya_ref[...], wb_ref[...], preferred_element_type=F32)
    x2 = x_ref[...] + y
    x2_ref[...] = x2
    h2_ref[...] = _rms(x2, g_ref[...]).astype(h2_ref.dtype)


def outproj(x, yp, ya, w_top, w_bot, g, *, tm):
    m, d = x.shape
    kp, ka = yp.shape[1], ya.shape[1]
    return pl.pallas_call(
        _outproj_kernel,
        grid=(m // tm,),
        in_specs=[
            pl.BlockSpec((tm, d), lambda i: (i, 0)),
            pl.BlockSpec((tm, kp), lambda i: (i, 0)),
            pl.BlockSpec((tm, ka), lambda i: (i, 0)),
            pl.BlockSpec((kp, d), lambda i: (0, 0)),
            pl.BlockSpec((ka, d), lambda i: (0, 0)),
            pl.BlockSpec((1, d), lambda i: (0, 0)),
        ],
        out_specs=[pl.BlockSpec((tm, d), lambda i: (i, 0)), pl.BlockSpec((tm, d), lambda i: (i, 0))],
        out_shape=[jax.ShapeDtypeStruct((m, d), F32), jax.ShapeDtypeStruct((m, d), BF16)],
        compiler_params=_cparams(1),
        name="outproj",
    )(x, yp, ya, w_top, w_bot, g.reshape(1, d))


def _ffn_kernel(h2_ref, wg_ref, wu_ref, wd_ref, x2_ref, g_ref, o_ref, acc_ref):
    f = pl.program_id(1)

    @pl.when(f == 0)
    def _():
        acc_ref[...] = jnp.zeros(acc_ref.shape, F32)

    h2 = h2_ref[...]
    gate = jnp.dot(h2, wg_ref[...], preferred_element_type=F32)
    up = jnp.dot(h2, wu_ref[...], preferred_element_type=F32)
    a = (gate * jax.nn.sigmoid(gate) * up).astype(BF16)
    acc_ref[...] += jnp.dot(a, wd_ref[...], preferred_element_type=F32)

    @pl.when(f == pl.num_programs(1) - 1)
    def _():
        o_ref[...] = _rms(x2_ref[...] + acc_ref[...], g_ref[...])


def ffn(h2, wg, wu, wd, x2, g, *, tm, tf):
    m, d = h2.shape
    dff = wg.shape[1]
    return pl.pallas_call(
        _ffn_kernel,
        grid=(m // tm, dff // tf),
        in_specs=[
            pl.BlockSpec((tm, d), lambda i, f: (i, 0)),
            pl.BlockSpec((d, tf), lambda i, f: (0, f)),
            pl.BlockSpec((d, tf), lambda i, f: (0, f)),
            pl.BlockSpec((tf, d), lambda i, f: (f, 0)),
            pl.BlockSpec((tm, d), lambda i, f: (i, 0)),
            pl.BlockSpec((1, d), lambda i, f: (0, 0)),
        ],
        out_specs=pl.BlockSpec((tm, d), lambda i, f: (i, 0)),
        out_shape=jax.ShapeDtypeStruct((m, d), F32),
        scratch_shapes=[pltpu.VMEM((tm, d), F32)],
        compiler_params=_cparams(2),
        name="ffn",
    )(h2, wg, wu, wd, x2, g.reshape(1, d))


def _rope_tables(pos):
    half = HEAD_DIM // 2
    inv = ROPE_THETA ** (-jnp.arange(half, dtype=F32) / half)
    ang = pos.astype(F32)[:, None] * inv[None, :]
    cos, sin = jnp.cos(ang), jnp.sin(ang)
    return jnp.concatenate([cos, cos], axis=-1), jnp.concatenate([-sin, sin], axis=-1)


def _mix_inputs(x2d, pos_rows, g_mix, w_in16, w_tail, *, tm, vt_chunk=None):
    cos, sin = _rope_tables(pos_rows)
    return mixproj(x2d, g_mix, w_in16, w_tail, cos, sin, tm=tm, vt_chunk=vt_chunk)


def kernel(x_prompt, x_sample, cache_k, cache_v, cache_idx_k, state_pool, page_table, g_mix, w_in, w_pool,
           s_pool, w_out, g_ffn, w_gate, w_up, w_down, g_final):
    B, S, D = x_prompt.shape
    DB, T, _ = x_sample.shape
    depth = w_in.shape[0]
    assert depth == 1
    n_pages = page_table.shape[1]
    page = cache_k.shape[2]
    n_heads = cache_k.shape[3]
    aw = n_heads * HEAD_DIM
    pw = w_pool.shape[1] * w_pool.shape[2]
    past = n_pages * page
    l = 0

    wide = pw + 3 * aw + IDX_HEADS * IDX_DIM
    assert (pw, aw, wide) == (PROJ_TN, PROJ_TN, 6 * PROJ_TN) and w_in.shape[2] == wide + IDX_DIM + IDX_HEADS
    w_in16 = w_in[l].astype(BF16)
    w_tail = jnp.pad(w_in16[:, wide:], ((0, 0), (0, LANES - IDX_HEADS)))
    w_pool16 = w_pool[l].astype(BF16)
    w_out16 = w_out[l].astype(BF16)
    w_top, w_bot = w_out16[:pw], w_out16[pw:]
    wg16, wu16, wd16 = w_gate[l].astype(BF16), w_up[l].astype(BF16), w_down[l].astype(BF16)

    xp2 = x_prompt.reshape(B * S, D)
    pos_p = jnp.arange(S)
    tk = ATTN_TK
    up, q, k32, k16, v32, qi, ki32, ki16, wi, vt = _mix_inputs(xp2, pos_p, g_mix[l], w_in16, w_tail, tm=512,
                                                               vt_chunk=tk)
    up3 = up.reshape(B, S, pw)
    yp = pool_mixer(up3, jnp.zeros((B, HALO, pw), F32), w_pool16, s_pool[l], tm=512, pos0=0)
    wit = wi.reshape(B, S, LANES)[:, :, :IDX_HEADS].transpose(0, 2, 1)
    ya = attn_prompt(q.reshape(B, S, aw), qi.reshape(B, S, -1), wit, k16.reshape(B, S, aw),
                     vt.reshape(B, S // tk, aw, tk), ki16.reshape(B, S, IDX_DIM), topk=min(TOPK_MAX, S // 4))
    x2, h2 = outproj(xp2, yp.reshape(B * S, pw), ya.reshape(B * S, aw), w_top, w_bot, g_ffn[l], tm=512)
    y_prompt = ffn(h2, wg16, wu16, wd16, x2, g_final, tm=512, tf=512).reshape(B, S, D)

    k_prompt = k32.reshape(1, B, S, n_heads, HEAD_DIM)
    v_prompt = v32.reshape(1, B, S, n_heads, HEAD_DIM)
    idx_k_prompt = ki32.reshape(1, B, S, IDX_DIM)
    pool_prompt = up3[:, S - POOL_STATE:, :][None]

    M = DB * T
    xs2 = x_sample.reshape(M, D)
    pos_s = jnp.tile(past + jnp.arange(T), DB)
    ups, qs, ks32, _, vs32, qis, kis32, kis16, wis = _mix_inputs(xs2, pos_s, g_mix[l], w_in16, w_tail, tm=M)
    ups3 = ups.reshape(DB, T, pw)
    prefix16 = jnp.pad(state_pool[l], ((0, 0), (HALO - POOL_STATE, 0), (0, 0)))
    yps = pool_mixer(ups3, prefix16, w_pool16, s_pool[l], tm=T, pos0=past)

    qi_ht = qis.reshape(DB, T, IDX_HEADS, IDX_DIM).transpose(0, 2, 1, 3).reshape(DB, IDX_HEADS * T, IDX_DIM)
    w_ht = wis.reshape(DB, T, LANES)[:, :, :IDX_HEADS].transpose(0, 2, 1).reshape(DB, IDX_HEADS * T, 1)
    w_ht = jnp.broadcast_to(w_ht * (IDX_DIM ** -0.5), (DB, IDX_HEADS * T, LANES))
    pad_rows = lambda a: jnp.pad(a.reshape(DB, T, -1), ((0, 0), (0, page - T), (0, 0)))
    sc, thr = idx_sample(page_table, qi_ht, w_ht, cache_idx_k, pad_rows(kis16), t_new=T,
                         topk=min(TOPK_MAX, (past + T) // 4))

    n_phys = cache_k.shape[1]
    key_head_rows = lambda a: pad_rows(a).reshape(DB, page * n_heads, HEAD_DIM)
    q_ht = qs.reshape(DB, T, n_heads, HEAD_DIM).transpose(0, 2, 1, 3)
    o_ht = attn_sample(page_table, q_ht, cache_k.reshape(depth, n_phys, page * n_heads, HEAD_DIM),
                       cache_v.reshape(depth, n_phys, page * n_heads, HEAD_DIM),
                       key_head_rows(ks32), key_head_rows(vs32), sc, thr)
    yas = o_ht.transpose(0, 2, 1, 3).reshape(M, aw).astype(BF16)

    x2s, h2s = outproj(xs2, yps.reshape(M, pw), yas, w_top, w_bot, g_ffn[l], tm=M)
    y_sample = ffn(h2s, wg16, wu16, wd16, x2s, g_final, tm=M, tf=512).reshape(DB, T, D)

    k_sample = ks32.reshape(1, DB, T, n_heads, HEAD_DIM)
    v_sample = vs32.reshape(1, DB, T, n_heads, HEAD_DIM)
    idx_k_sample = kis32.reshape(1, DB, T, IDX_DIM)
    pool_sample = jnp.concatenate([state_pool[l][:, T:, :], ups3], axis=1)[None]

    return (y_prompt, y_sample, k_prompt, v_prompt, idx_k_prompt, pool_prompt,
            k_sample, v_sample, idx_k_sample, pool_sample)
```

```python
import functools
import math

import jax
import jax.numpy as jnp
from jax import lax
from jax.experimental import pallas as pl
from jax.experimental.pallas import tpu as pltpu

F32 = jnp.float32
BF16 = jnp.bfloat16

LANES = 128
HEAD_DIM = 128
IDX_DIM = 128
IDX_HEADS = 16
POOL_GROUPS = 4
POOL_WINDOWS = (2, 4, 8, 16)
POOL_STATE = max(POOL_WINDOWS) - 1
HALO = 16
TOPK_MAX = 256
ATTN_TQ = 256
ATTN_TK = 128
CHUNK_TRIP_FACTOR = 4
IDX_PAGES_PER_STEP = 32
KV_PAGES_PER_STEP = 16
ROPE_THETA = 10000.0
EPS = 1e-6
NEG = -1e30
MAX_SEARCH_STEPS = 1024
LOG2E = 1.4426950408889634
VMEM_LIMIT = 56 * 1024 * 1024

NT_DIMS = (((1,), (1,)), ((), ()))


def _cparams(n_grid):
    return pltpu.CompilerParams(dimension_semantics=("arbitrary",) * n_grid, vmem_limit_bytes=VMEM_LIMIT)


def _rms(x, g):
    return x * lax.rsqrt(jnp.mean(x * x, axis=-1, keepdims=True) + EPS) * g


PROJ_TN = 1024
PROJ_PIECE = 256


def _rope(y, cos, sin):
    return [y[:, g * HEAD_DIM:(g + 1) * HEAD_DIM] * cos
            + pltpu.roll(y[:, g * HEAD_DIM:(g + 1) * HEAD_DIM], HEAD_DIM // 2, axis=1) * sin
            for g in range(y.shape[1] // HEAD_DIM)]


def _mixproj_kernel(x_ref, g_ref, w_ref, wt_ref, cos_ref, sin_ref,
                    up_ref, q_ref, k32_ref, k16_ref, v32_ref, qi_ref, ki32_ref, ki16_ref, wi_ref, *rest):
    maybe_vt_ref, h_ref = rest if len(rest) == 2 else (None, rest[0])
    j = pl.program_id(1)

    def heads_to(pieces, refs, col0=0):
        for g, r in enumerate(pieces):
            for o_ref in refs:
                o_ref[:, col0 + g * HEAD_DIM:col0 + (g + 1) * HEAD_DIM] = r.astype(o_ref.dtype)

    @pl.when(j == 0)
    def _():
        h_ref[...] = _rms(x_ref[...], g_ref[...]).astype(h_ref.dtype)
        t = jnp.dot(h_ref[...], wt_ref[...], preferred_element_type=F32)
        heads_to(_rope(t[:, :IDX_DIM], cos_ref[...], sin_ref[...]), (ki32_ref, ki16_ref))
        wi_ref[...] = t[:, IDX_DIM:] * (IDX_HEADS ** -0.5)

    def col_pieces(fn):
        for c0 in range(0, PROJ_TN, PROJ_PIECE):
            fn(c0, jnp.dot(h_ref[...], w_ref[:, c0:c0 + PROJ_PIECE], preferred_element_type=F32))

    def rope_to(refs, col_base=0, scale=None):
        def fn(c0, y):
            y = y if scale is None else y * scale
            heads_to(_rope(y, cos_ref[...], sin_ref[...]), refs, col0=col_base + c0)
        return fn

    @pl.when(j == 0)
    def _():
        def fn(c0, y):
            up_ref[:, c0:c0 + PROJ_PIECE] = y
        col_pieces(fn)

    @pl.when(j == 1)
    def _():
        col_pieces(rope_to((q_ref,), scale=HEAD_DIM ** -0.5 * LOG2E))

    @pl.when(j == 2)
    def _():
        col_pieces(rope_to((k32_ref, k16_ref)))

    @pl.when(j == 3)
    def _():
        def fn(c0, y):
            v32_ref[:, c0:c0 + PROJ_PIECE] = y
            if maybe_vt_ref is not None:
                n_chunks, _, ck = maybe_vt_ref.shape
                for c in range(n_chunks):
                    for g in range(PROJ_PIECE // HEAD_DIM):
                        hs = slice(g * HEAD_DIM, (g + 1) * HEAD_DIM)
                        maybe_vt_ref[c, c0 + g * HEAD_DIM:c0 + (g + 1) * HEAD_DIM, :] = (
                            y[c * ck:(c + 1) * ck, hs].T.astype(maybe_vt_ref.dtype))
        col_pieces(fn)

    for t_qi in range(2):
        @pl.when(j == 4 + t_qi)
        def _():
            col_pieces(rope_to((qi_ref,), col_base=t_qi * PROJ_TN))


def mixproj(x, g, w_in16, w_tail, cos, sin, *, tm, vt_chunk=None):
    m, d = x.shape
    tn = PROJ_TN
    n_tiles = 6
    row = lambda width: pl.BlockSpec((tm, width), lambda i, j: (i, 0))
    rope_tiles = cos.shape[0] // tm
    rope_spec = pl.BlockSpec((tm, HEAD_DIM), lambda i, j: (i % rope_tiles, 0))
    widths = (tn, tn, tn, tn, tn, 2 * tn, IDX_DIM, IDX_DIM, LANES)
    dtypes = (F32, BF16, F32, BF16, F32, BF16, F32, BF16, F32)
    out_specs = [row(w) for w in widths]
    out_shape = [jax.ShapeDtypeStruct((m, w), dt) for w, dt in zip(widths, dtypes)]
    if vt_chunk is not None:
        out_specs.append(pl.BlockSpec((tm // vt_chunk, tn, vt_chunk), lambda i, j: (i, 0, 0)))
        out_shape.append(jax.ShapeDtypeStruct((m // vt_chunk, tn, vt_chunk), BF16))
    return pl.pallas_call(
        _mixproj_kernel,
        grid=(m // tm, n_tiles),
        in_specs=[
            row(d),
            pl.BlockSpec((1, d), lambda i, j: (0, 0)),
            pl.BlockSpec((d, tn), lambda i, j: (0, j)),
            pl.BlockSpec(w_tail.shape, lambda i, j: (0, 0)),
            rope_spec, rope_spec,
        ],
        out_specs=out_specs,
        out_shape=out_shape,
        scratch_shapes=[pltpu.VMEM((tm, d), BF16)],
        compiler_params=_cparams(2),
        name="mixproj",
    )(x, g.reshape(1, d), w_in16, w_tail, cos, sin)


def _pool_kernel(*refs, tm, pos0, has_prev):
    if has_prev:
        up_ref, prev_ref, prefix_ref, w_ref, s_ref, o_ref, ext_ref = refs
    else:
        up_ref, prefix_ref, w_ref, s_ref, o_ref, ext_ref = refs
    i = pl.program_id(1)
    if has_prev:
        halo = jnp.where(i == 0, prefix_ref[...], prev_ref[...])
    else:
        halo = prefix_ref[...]
    ext_ref[0:HALO, :] = halo
    ext_ref[HALO:HALO + tm, :] = up_ref[...]
    pos = pos0 + i * tm + lax.broadcasted_iota(jnp.int32, (tm, 1), 0)
    gw = up_ref.shape[1] // POOL_GROUPS
    for g, w in enumerate(POOL_WINDOWS):
        sl = slice(g * gw, (g + 1) * gw)
        cur = ext_ref[HALO:HALO + tm, sl]
        s = cur
        for j in range(1, w):
            s = s + ext_ref[HALO - j:HALO - j + tm, sl]
        cnt = jnp.minimum(w, pos + 1).astype(F32)
        d = s / cnt - cur
        y = jnp.dot(d.astype(BF16), w_ref[g], preferred_element_type=F32) * s_ref[:, sl]
        o_ref[:, sl] = y.astype(o_ref.dtype)


def pool_mixer(up, prefix16, w_pool, s_pool, *, tm, pos0):
    b, t, wd = up.shape
    has_prev = t > tm
    gw = wd // POOL_GROUPS
    in_specs = [pl.BlockSpec((None, tm, wd), lambda bi, i: (bi, i, 0))]
    args = [up]
    if has_prev:
        r = tm // HALO
        in_specs.append(pl.BlockSpec((None, HALO, wd), lambda bi, i: (bi, jnp.maximum(i * r - 1, 0), 0)))
        args.append(up)
    in_specs += [
        pl.BlockSpec((None, HALO, wd), lambda bi, i: (bi, 0, 0)),
        pl.BlockSpec((POOL_GROUPS, gw, gw), lambda bi, i: (0, 0, 0)),
        pl.BlockSpec((1, wd), lambda bi, i: (0, 0)),
    ]
    args += [prefix16, w_pool, s_pool.reshape(1, wd)]
    return pl.pallas_call(
        functools.partial(_pool_kernel, tm=tm, pos0=pos0, has_prev=has_prev),
        grid=(b, t // tm),
        in_specs=in_specs,
        out_specs=pl.BlockSpec((None, tm, wd), lambda bi, i: (bi, i, 0)),
        out_shape=jax.ShapeDtypeStruct((b, t, wd), BF16),
        scratch_shapes=[pltpu.VMEM((HALO + tm, wd), F32)],
        compiler_params=_cparams(2),
        name="pool_mixer",
    )(*args)


PEEL_BRACKET = 8
UNTESTED_HALVINGS = 6


def _select_threshold(count_ge, max_below, rmin, rmax, n_valid, kq):
    def any_row(pred):
        return jnp.max(jnp.where(pred, 1.0, 0.0))

    def halve(lo, hi, cl, ch):
        mid = lo + 0.5 * (hi - lo)
        stuck = (mid <= lo) | (mid >= hi)
        c = count_ge(mid)
        ge = c >= kq
        return (jnp.where(ge, mid, lo), jnp.where(ge, hi, mid), jnp.where(ge, c, cl), jnp.where(ge, ch, c),
                jnp.where(stuck, 1.0, 0.0))

    def halving_loop(st, bracket_cap):
        def open_rows(cl, ch, stuck):
            return any_row((cl != kq) & (stuck <= 0.0) & (cl - ch > bracket_cap))

        def cond(st):
            return jnp.logical_and(st[0] < MAX_SEARCH_STEPS, st[1] > 0.0)

        def body(st):
            steps, _, lo, hi, cl, ch, _ = st
            lo, hi, cl, ch, stuck = halve(lo, hi, cl, ch)
            return steps + 1, open_rows(cl, ch, stuck), lo, hi, cl, ch, stuck

        st = (st[0], open_rows(st[4], st[5], st[6])) + tuple(st[2:])
        return lax.while_loop(cond, body, st)

    hi0 = rmax + jnp.maximum(jnp.abs(rmax) * 1e-6, 1e-30)
    zero = jnp.zeros_like(rmax)
    st = lax.fori_loop(0, UNTESTED_HALVINGS, lambda _, st: halve(*st[:4]), (rmin, hi0, n_valid, zero, zero))
    st = halving_loop((jnp.int32(0), jnp.float32(0.0)) + tuple(st), float(PEEL_BRACKET))
    _, _, lo, hi, cl, ch, stuck = st

    walk = (cl != kq) & (stuck <= 0.0)

    def walk_cond(st):
        return jnp.logical_and(st[0] < PEEL_BRACKET, st[1] > 0.0)

    def walk_body(st):
        steps, _, top, n_top = st
        active = walk & (n_top < kq)
        top = jnp.where(active, max_below(top), top)
        n_top = jnp.where(active, n_top + 1.0, n_top)
        return steps + 1, any_row(walk & (n_top < kq)), top, n_top

    _, _, top, _ = lax.while_loop(walk_cond, walk_body, (jnp.int32(0), any_row(walk), hi, ch))
    thr = jnp.where(walk, top, lo)
    n_ge = count_ge(thr)

    redo = n_ge != kq
    st = halving_loop((jnp.int32(0), jnp.float32(0.0), lo, hi, cl, ch, jnp.where(redo, stuck, 1.0)), 0.0)
    return jnp.where(redo, st[2], thr), jnp.where(redo, st[4], n_ge)


def _last_tied_index(count_tied_upto, need, n_keys):
    def body(_, st):
        lo, hi = st
        mid = jnp.floor((lo + hi) * 0.5)
        ok = count_tied_upto(mid) >= need
        return jnp.where(ok, lo, mid), jnp.where(ok, mid, hi)

    steps = max(1, math.ceil(math.log2(n_keys + 1)))
    _, hi = lax.fori_loop(0, steps, body, (jnp.full(need.shape, -1.0, F32), jnp.full(need.shape, n_keys - 1.0, F32)))
    return hi


def _attn_prompt_kernel(q_ref, qi_ref, wit_ref, k_ref, vt_ref, ki_ref, o_ref,
                        sc_ref, ext_ref, m_ref, l_ref, acc_ref, *, tk, topk, n_heads):
    i = pl.program_id(1)
    tq = q_ref.shape[0]
    nc = ((i + 1) * tq + tk - 1) // tk
    qpos = i * tq + lax.broadcasted_iota(jnp.int32, (1, tq), 1)
    key_row = lax.broadcasted_iota(jnp.int32, (tk, tq), 0)
    w_rows = wit_ref[...] * (IDX_DIM ** -0.5)

    def idx_chunk(c, carry):
        off = pl.multiple_of(c * tk, tk)
        kic = ki_ref[pl.ds(off, tk), :]
        acc = jnp.zeros((tk, tq), F32)
        for h in range(IDX_HEADS):
            s = lax.dot_general(kic, qi_ref[:, h * IDX_DIM:(h + 1) * IDX_DIM], NT_DIMS,
                                preferred_element_type=F32)
            acc = acc + jnp.maximum(s, 0.0) * w_rows[h:h + 1, :]
        causal = off + key_row <= qpos
        sc = jnp.where(causal, acc, -jnp.inf)
        sc_ref[pl.ds(off, tk), :] = sc
        ext_ref[0] = jnp.minimum(ext_ref[0], jnp.min(jnp.where(causal, acc, jnp.inf).reshape(tk // 8, 8, tq), axis=0))
        ext_ref[1] = jnp.maximum(ext_ref[1], jnp.max(sc.reshape(tk // 8, 8, tq), axis=0))
        return carry

    ext_ref[0] = jnp.full((8, tq), jnp.inf, F32)
    ext_ref[1] = jnp.full((8, tq), -jnp.inf, F32)

    def chunk_loop(n_chunks, chunk_fn):
        base = tq // tk
        per_trip = CHUNK_TRIP_FACTOR * base

        def run(first, count):
            for u in range(count):
                chunk_fn(first + u, 0)

        def body(t, carry):
            run(t * per_trip, per_trip)
            return carry
        lax.fori_loop(0, n_chunks // per_trip, body, 0)
        for r in range(1, CHUNK_TRIP_FACTOR):
            @pl.when(n_chunks % per_trip == r * base)
            def _():
                run((n_chunks // per_trip) * per_trip, r * base)

    chunk_loop(nc, idx_chunk)

    lanes_par = 4

    def reduce_keys(fn, init, red):
        rows = tq

        def body(c, a):
            blk = sc_ref[pl.ds(pl.multiple_of(c * rows, rows), rows), :]
            return fn(a, blk.reshape(lanes_par, rows // (8 * lanes_par), 8, tq))
        a = lax.fori_loop(0, (nc * tk) // rows, body, jnp.full((lanes_par, 8, tq), init, F32))
        return red(red(a, axis=0), axis=0, keepdims=True)

    def count_ge(t):
        return reduce_keys(lambda a, blk: a + jnp.sum(jnp.where(blk >= t, 1.0, 0.0), axis=1), 0.0, jnp.sum)

    def max_below(t):
        return reduce_keys(lambda a, blk: jnp.maximum(a, jnp.max(jnp.where(blk < t, blk, -jnp.inf), axis=1)),
                           -jnp.inf, jnp.max)

    rmin = jnp.min(ext_ref[0], axis=0, keepdims=True)
    rmax = jnp.max(ext_ref[1], axis=0, keepdims=True)
    n_valid = (qpos + 1).astype(F32)
    kq = jnp.minimum(float(topk), n_valid)
    thr, n_ge = _select_threshold(count_ge, max_below, rmin, rmax, n_valid, kq)

    @pl.when(jnp.max(n_ge - kq) > 0.0)
    def _():
        def key_sum(fn):
            def body(c, a):
                off = pl.multiple_of(c * tk, tk)
                hit = fn(sc_ref[pl.ds(off, tk), :], (off + key_row).astype(F32))
                return a + jnp.sum(jnp.where(hit, 1.0, 0.0), axis=0, keepdims=True)
            return lax.fori_loop(0, nc, body, jnp.zeros((1, tq), F32))

        need = kq - key_sum(lambda blk, key: blk > thr)
        last = _last_tied_index(lambda j: key_sum(lambda blk, key: (blk == thr) & (key <= j)), need,
                                sc_ref.shape[0])

        def drop(c, carry):
            off = pl.multiple_of(c * tk, tk)
            blk = sc_ref[pl.ds(off, tk), :]
            beyond = (blk == thr) & ((off + key_row).astype(F32) > last)
            sc_ref[pl.ds(off, tk), :] = jnp.where(beyond, -jnp.inf, blk)
            return carry
        lax.fori_loop(0, nc, drop, 0)

    m_ref[...] = jnp.full(m_ref.shape, -jnp.inf, F32)
    l_ref[...] = jnp.zeros(l_ref.shape, F32)
    acc_ref[...] = jnp.zeros(acc_ref.shape, F32)

    def att_chunk(c, carry):
        off = pl.multiple_of(c * tk, tk)
        mask = sc_ref[pl.ds(off, tk), :] >= thr
        for h in range(n_heads):
            hs = slice(h * HEAD_DIM, (h + 1) * HEAD_DIM)
            s = lax.dot_general(k_ref[pl.ds(off, tk), hs], q_ref[:, hs], NT_DIMS,
                                preferred_element_type=F32)
            s = jnp.where(mask, s, NEG)
            m_prev = m_ref[h]
            m_new = jnp.maximum(m_prev, jnp.max(s, axis=0, keepdims=True))
            alpha = jnp.exp2(m_prev - m_new)
            p = jnp.exp2(s - m_new)
            l_ref[h] = alpha * l_ref[h] + jnp.sum(p, axis=0, keepdims=True)
            m_ref[h] = m_new
            pv = jnp.dot(vt_ref[c, hs, :], p.astype(BF16), preferred_element_type=F32)
            acc_ref[h] = alpha * acc_ref[h] + pv
        return carry

    chunk_loop(nc, att_chunk)

    for h in range(n_heads):
        hs = slice(h * HEAD_DIM, (h + 1) * HEAD_DIM)
        o_ref[:, hs] = (acc_ref[h] / l_ref[h]).T.astype(o_ref.dtype)


def attn_prompt(q, qi, wit, k, vt, ki, *, topk):
    b, s, aw = q.shape
    n_heads = aw // HEAD_DIM
    tq = ATTN_TQ
    tk = vt.shape[3]
    kern = functools.partial(_attn_prompt_kernel, tk=ATTN_TK, topk=topk, n_heads=n_heads)
    return pl.pallas_call(
        kern,
        grid=(b, s // tq),
        in_specs=[
            pl.BlockSpec((None, tq, aw), lambda bi, i: (bi, i, 0)),
            pl.BlockSpec((None, tq, qi.shape[2]), lambda bi, i: (bi, i, 0)),
            pl.BlockSpec((None, IDX_HEADS, tq), lambda bi, i: (bi, 0, i)),
            pl.BlockSpec((None, s, aw), lambda bi, i: (bi, 0, 0)),
            pl.BlockSpec((None, s // tk, aw, tk), lambda bi, i: (bi, 0, 0, 0)),
            pl.BlockSpec((None, s, IDX_DIM), lambda bi, i: (bi, 0, 0)),
        ],
        out_specs=pl.BlockSpec((None, tq, aw), lambda bi, i: (bi, i, 0)),
        out_shape=jax.ShapeDtypeStruct((b, s, aw), BF16),
        scratch_shapes=[
            pltpu.VMEM((s, tq), F32),
            pltpu.VMEM((2, 8, tq), F32),
            pltpu.VMEM((n_heads, 1, tq), F32),
            pltpu.VMEM((n_heads, 1, tq), F32),
            pltpu.VMEM((n_heads, HEAD_DIM, tq), F32),
        ],
        compiler_params=_cparams(2),
        name="attn_prompt",
    )(q, qi, wit, k, vt, ki)


def _idx_sample_kernel(pt_ref, qi_ref, w_ref, *rest, n_pages, group, t_new, topk):
    cik_refs = rest[:group]
    kin_ref, sc_ref, thr_ref = rest[group:]
    b, p = pl.program_id(0), pl.program_id(1)
    db = sc_ref.shape[0]
    rows = qi_ref.shape[0]

    def scores(keys_bf16):
        s = lax.dot_general(qi_ref[...], keys_bf16, NT_DIMS, preferred_element_type=F32)
        r = jnp.maximum(s, 0.0) * w_ref[...]
        return jnp.sum(r.reshape(rows // t_new, t_new, LANES), axis=0)

    for g in range(group):
        sc_ref[b, p * group + g] = scores(cik_refs[g][...].astype(BF16))

    @pl.when(p == 0)
    def _():
        s = scores(kin_ref[...])
        tok = lax.broadcasted_iota(jnp.int32, (t_new, LANES), 0)
        lane = lax.broadcasted_iota(jnp.int32, (t_new, LANES), 1)
        sc_ref[b, n_pages] = jnp.where((lane <= tok) & (lane < t_new), s, -jnp.inf)

    @pl.when((b == db - 1) & (p == pl.num_programs(1) - 1))
    def _():
        def reduce_keys(fn, red):
            return red(red(fn(sc_ref[...]), axis=1), axis=2, keepdims=True)

        def count_ge(t):
            return reduce_keys(lambda x: jnp.where(x >= t[:, None], 1.0, 0.0), jnp.sum)

        def max_below(t):
            return reduce_keys(lambda x: jnp.where(x < t[:, None], x, -jnp.inf), jnp.max)

        rmax = reduce_keys(lambda x: x, jnp.max)
        rmin = reduce_keys(lambda x: jnp.where(x == -jnp.inf, jnp.inf, x), jnp.min)
        tokc = lax.broadcasted_iota(jnp.int32, (db, t_new, 1), 1)
        n_valid = (n_pages * LANES + tokc + 1).astype(F32)
        kq = jnp.minimum(float(topk), n_valid)
        thr, n_ge = _select_threshold(count_ge, max_below, rmin, rmax, n_valid, kq)
        thr_ref[...] = jnp.broadcast_to(thr, thr_ref.shape)

        @pl.when(jnp.max(n_ge - kq) > 0.0)
        def _():
            n_slabs = n_pages + 1
            key = (lax.broadcasted_iota(jnp.int32, (1, n_slabs, 1, LANES), 1) * LANES
                   + lax.broadcasted_iota(jnp.int32, (1, n_slabs, 1, LANES), 3)).astype(F32)
            thr4 = thr[:, None]

            def key_sum(hit):
                return jnp.sum(jnp.sum(jnp.where(hit, 1.0, 0.0), axis=1), axis=2, keepdims=True)

            need = kq - key_sum(sc_ref[...] > thr4)
            last = _last_tied_index(lambda j: key_sum((sc_ref[...] == thr4) & (key <= j[:, None])), need,
                                    n_slabs * LANES)
            sc = sc_ref[...]
            sc_ref[...] = jnp.where((sc == thr4) & (key > last[:, None]), -jnp.inf, sc)


def _page_map(b, p, pt, *, g, group):
    return (0, pt[b, p * group + g], 0, 0)


def idx_sample(page_table, qi_ht, w_ht, cache_idx_k, ki_new_pad, *, t_new, topk):
    db, n_pages = page_table.shape
    group = math.gcd(n_pages, IDX_PAGES_PER_STEP)
    rows = qi_ht.shape[1]
    page = cache_idx_k.shape[2]
    assert n_pages % group == 0
    kern = functools.partial(_idx_sample_kernel, n_pages=n_pages, group=group, t_new=t_new, topk=topk)
    page_specs = [pl.BlockSpec((None, None, page, IDX_DIM), functools.partial(_page_map, g=g, group=group))
                  for g in range(group)]
    grid_spec = pltpu.PrefetchScalarGridSpec(
        num_scalar_prefetch=1,
        grid=(db, n_pages // group),
        in_specs=[
            pl.BlockSpec((None, rows, IDX_DIM), lambda b, p, pt: (b, 0, 0)),
            pl.BlockSpec((None, rows, LANES), lambda b, p, pt: (b, 0, 0)),
            *page_specs,
            pl.BlockSpec((None, page, IDX_DIM), lambda b, p, pt: (b, 0, 0)),
        ],
        out_specs=[
            pl.BlockSpec((db, n_pages + 1, t_new, LANES), lambda b, p, pt: (0, 0, 0, 0)),
            pl.BlockSpec((db, t_new, LANES), lambda b, p, pt: (0, 0, 0)),
        ],
    )
    return pl.pallas_call(
        kern,
        grid_spec=grid_spec,
        out_shape=[jax.ShapeDtypeStruct((db, n_pages + 1, t_new, LANES), F32),
                   jax.ShapeDtypeStruct((db, t_new, LANES), F32)],
        compiler_params=_cparams(2),
        name="idx_sample",
    )(page_table, qi_ht, w_ht, *([cache_idx_k] * group), ki_new_pad)


def _attn_sample_kernel(pt_ref, q_ref, *rest, group, page, n_heads):
    ck_refs = rest[:group]
    cv_refs = rest[group:2 * group]
    kn_ref, vn_ref, sc_ref, scn_ref, thr_ref, o_ref, m_ref, l_ref, acc_ref = rest[2 * group:]
    p = pl.program_id(1)

    @pl.when(p == 0)
    def _():
        m_ref[...] = jnp.full(m_ref.shape, -jnp.inf, F32)
        l_ref[...] = jnp.zeros(l_ref.shape, F32)
        acc_ref[...] = jnp.zeros(acc_ref.shape, F32)

    thr = thr_ref[...]

    def head_rows(ref, h):
        return ref[pl.ds(h, page, stride=n_heads), :].astype(BF16)

    def attend(k_refs, v_refs, sc, sc_thr):
        n = len(k_refs)
        s = jnp.stack([
            lax.dot_general(q_ref[h], jnp.concatenate([head_rows(r, h) for r in k_refs], axis=0), NT_DIMS,
                            preferred_element_type=F32)
            for h in range(n_heads)])
        s = jnp.where(sc[None] >= sc_thr[None], s, NEG)
        m_prev = m_ref[...]
        m_new = jnp.maximum(m_prev, jnp.max(s, axis=2, keepdims=True))
        alpha = jnp.exp2(m_prev - m_new)
        pr = jnp.exp2(s - jnp.concatenate([m_new] * n, axis=2))
        l_ref[...] = alpha * l_ref[...] + jnp.sum(pr, axis=2, keepdims=True)
        m_ref[...] = m_new
        pb = pr.astype(BF16)
        for h in range(n_heads):
            vh = jnp.concatenate([head_rows(r, h) for r in v_refs], axis=0)
            acc_ref[h] = alpha[h] * acc_ref[h] + jnp.dot(pb[h], vh, preferred_element_type=F32)

    attend(ck_refs, cv_refs, jnp.concatenate([sc_ref[g] for g in range(group)], axis=1),
           jnp.concatenate([thr] * group, axis=1))

    @pl.when(p == pl.num_programs(1) - 1)
    def _():
        attend([kn_ref], [vn_ref], scn_ref[...], thr)
        o_ref[...] = acc_ref[...] / l_ref[...]


def attn_sample(page_table, q_ht, cache_k, cache_v, k_new_pad, v_new_pad, sc, thr):
    db, n_pages = page_table.shape
    n_heads, t_new = q_ht.shape[1], q_ht.shape[2]
    prow = cache_k.shape[2]
    page = prow // n_heads
    group = math.gcd(n_pages, KV_PAGES_PER_STEP)
    kern = functools.partial(_attn_sample_kernel, group=group, page=page, n_heads=n_heads)
    cache_specs = [pl.BlockSpec((None, None, prow, HEAD_DIM), functools.partial(_page_map, g=g, group=group))
                   for g in range(group)]
    new_spec = pl.BlockSpec((None, prow, HEAD_DIM), lambda b, p, pt: (b, 0, 0))
    head_spec = pl.BlockSpec((None, n_heads, t_new, HEAD_DIM), lambda b, p, pt: (b, 0, 0, 0))
    grid_spec = pltpu.PrefetchScalarGridSpec(
        num_scalar_prefetch=1,
        grid=(db, n_pages // group),
        in_specs=[
            head_spec,
            *cache_specs, *cache_specs, new_spec, new_spec,
            pl.BlockSpec((None, group, t_new, LANES), lambda b, p, pt: (b, p, 0, 0)),
            pl.BlockSpec((None, None, t_new, LANES), lambda b, p, pt: (b, n_pages, 0, 0)),
            pl.BlockSpec((None, t_new, LANES), lambda b, p, pt: (b, 0, 0)),
        ],
        out_specs=head_spec,
        scratch_shapes=[
            pltpu.VMEM((n_heads, t_new, LANES), F32),
            pltpu.VMEM((n_heads, t_new, LANES), F32),
            pltpu.VMEM((n_heads, t_new, HEAD_DIM), F32),
        ],
    )
    return pl.pallas_call(
        kern,
        grid_spec=grid_spec,
        out_shape=jax.ShapeDtypeStruct((db, n_heads, t_new, HEAD_DIM), F32),
        compiler_params=_cparams(2),
        name="attn_sample",
    )(page_table, q_ht, *([cache_k] * group), *([cache_v] * group), k_new_pad, v_new_pad, sc, sc, thr)


def _outproj_kernel(x_ref, yp_ref, ya_ref, wt_ref, wb_ref, g_ref, x2_ref, h2_ref):
    y = jnp.dot(yp_ref[...], wt_ref[...], preferred_element_type=F32)
    y = y + jnp.dot(ya_ref[...], wb_ref[...], preferred_element_type=F32)
    x2 = x_ref[...] + y
    x2_ref[...] = x2
    h2_ref[...] = _rms(x2, g_ref[...]).astype(h2_ref.dtype)


def outproj(x, yp, ya, w_out, g, *, tm):
    m, d = x.shape
    kp, ka = yp.shape[1], ya.shape[1]
    assert kp == ka and w_out.shape == (kp + ka, d)
    return pl.pallas_call(
        _outproj_kernel,
        grid=(m // tm,),
        in_specs=[
            pl.BlockSpec((tm, d), lambda i: (i, 0)),
            pl.BlockSpec((tm, kp), lambda i: (i, 0)),
            pl.BlockSpec((tm, ka), lambda i: (i, 0)),
            pl.BlockSpec((kp, d), lambda i: (0, 0)),
            pl.BlockSpec((ka, d), lambda i: (1, 0)),
            pl.BlockSpec((1, d), lambda i: (0, 0)),
        ],
        out_specs=[pl.BlockSpec((tm, d), lambda i: (i, 0)), pl.BlockSpec((tm, d), lambda i: (i, 0))],
        out_shape=[jax.ShapeDtypeStruct((m, d), F32), jax.ShapeDtypeStruct((m, d), BF16)],
        compiler_params=_cparams(1),
        name="outproj",
    )(x, yp, ya, w_out, w_out, g.reshape(1, d))


def _ffn_kernel(h2_ref, wg_ref, wu_ref, wd_ref, x2_ref, g_ref, o_ref, acc_ref):
    f = pl.program_id(1)

    @pl.when(f == 0)
    def _():
        acc_ref[...] = jnp.zeros(acc_ref.shape, F32)

    h2 = h2_ref[...]
    gate = jnp.dot(h2, wg_ref[...], preferred_element_type=F32)
    up = jnp.dot(h2, wu_ref[...], preferred_element_type=F32)
    a = (gate * jax.nn.sigmoid(gate) * up).astype(BF16)
    acc_ref[...] += jnp.dot(a, wd_ref[...], preferred_element_type=F32)

    @pl.when(f == pl.num_programs(1) - 1)
    def _():
        o_ref[...] = _rms(x2_ref[...] + acc_ref[...], g_ref[...])


def ffn(h2, wg, wu, wd, x2, g, *, tm, tf):
    m, d = h2.shape
    dff = wg.shape[1]
    return pl.pallas_call(
        _ffn_kernel,
        grid=(m // tm, dff // tf),
        in_specs=[
            pl.BlockSpec((tm, d), lambda i, f: (i, 0)),
            pl.BlockSpec((d, tf), lambda i, f: (0, f)),
            pl.BlockSpec((d, tf), lambda i, f: (0, f)),
            pl.BlockSpec((tf, d), lambda i, f: (f, 0)),
            pl.BlockSpec((tm, d), lambda i, f: (i, 0)),
            pl.BlockSpec((1, d), lambda i, f: (0, 0)),
        ],
        out_specs=pl.BlockSpec((tm, d), lambda i, f: (i, 0)),
        out_shape=jax.ShapeDtypeStruct((m, d), F32),
        scratch_shapes=[pltpu.VMEM((tm, d), F32)],
        compiler_params=_cparams(2),
        name="ffn",
    )(h2, wg, wu, wd, x2, g.reshape(1, d))


def _rope_tables(pos):
    half = HEAD_DIM // 2
    inv = ROPE_THETA ** (-jnp.arange(half, dtype=F32) / half)
    ang = pos.astype(F32)[:, None] * inv[None, :]
    cos, sin = jnp.cos(ang), jnp.sin(ang)
    return jnp.concatenate([cos, cos], axis=-1), jnp.concatenate([-sin, sin], axis=-1)


def _mix_inputs(x2d, pos_rows, g_mix, w_in16, w_tail, *, tm, vt_chunk=None):
    cos, sin = _rope_tables(pos_rows)
    return mixproj(x2d, g_mix, w_in16, w_tail, cos, sin, tm=tm, vt_chunk=vt_chunk)


def kernel(x_prompt, x_sample, cache_k, cache_v, cache_idx_k, state_pool, page_table, g_mix, w_in, w_pool,
           s_pool, w_out, g_ffn, w_gate, w_up, w_down, g_final):
    B, S, D = x_prompt.shape
    DB, T, _ = x_sample.shape
    depth = w_in.shape[0]
    assert depth == 1
    n_pages = page_table.shape[1]
    page = cache_k.shape[2]
    n_heads = cache_k.shape[3]
    aw = n_heads * HEAD_DIM
    pw = w_pool.shape[1] * w_pool.shape[2]
    past = n_pages * page
    l = 0

    wide = pw + 3 * aw + IDX_HEADS * IDX_DIM
    assert (pw, aw, wide) == (PROJ_TN, PROJ_TN, 6 * PROJ_TN) and w_in.shape[2] == wide + IDX_DIM + IDX_HEADS
    w_in16 = w_in[l].astype(BF16)
    w_tail = jnp.pad(w_in16[:, wide:], ((0, 0), (0, LANES - IDX_HEADS)))
    w_pool16 = w_pool[l].astype(BF16)
    w_out16 = w_out[l].astype(BF16)
    wg16, wu16, wd16 = w_gate[l].astype(BF16), w_up[l].astype(BF16), w_down[l].astype(BF16)

    xp2 = x_prompt.reshape(B * S, D)
    pos_p = jnp.arange(S)
    tk = ATTN_TK
    up, q, k32, k16, v32, qi, ki32, ki16, wi, vt = _mix_inputs(xp2, pos_p, g_mix[l], w_in16, w_tail, tm=512,
                                                               vt_chunk=tk)
    up3 = up.reshape(B, S, pw)
    yp = pool_mixer(up3, jnp.zeros((B, HALO, pw), F32), w_pool16, s_pool[l], tm=512, pos0=0)
    wit = wi.reshape(B, S, LANES)[:, :, :IDX_HEADS].transpose(0, 2, 1)
    ya = attn_prompt(q.reshape(B, S, aw), qi.reshape(B, S, -1), wit, k16.reshape(B, S, aw),
                     vt.reshape(B, S // tk, aw, tk), ki16.reshape(B, S, IDX_DIM), topk=min(TOPK_MAX, S // 4))
    x2, h2 = outproj(xp2, yp.reshape(B * S, pw), ya.reshape(B * S, aw), w_out16, g_ffn[l], tm=512)
    y_prompt = ffn(h2, wg16, wu16, wd16, x2, g_final, tm=512, tf=512).reshape(B, S, D)

    k_prompt = k32.reshape(1, B, S, n_heads, HEAD_DIM)
    v_prompt = v32.reshape(1, B, S, n_heads, HEAD_DIM)
    idx_k_prompt = ki32.reshape(1, B, S, IDX_DIM)
    pool_prompt = up3[:, S - POOL_STATE:, :][None]

    M = DB * T
    xs2 = x_sample.reshape(M, D)
    pos_s = jnp.tile(past + jnp.arange(T), DB)
    ups, qs, ks32, _, vs32, qis, kis32, kis16, wis = _mix_inputs(xs2, pos_s, g_mix[l], w_in16, w_tail, tm=M)
    ups3 = ups.reshape(DB, T, pw)
    prefix16 = jnp.pad(state_pool[l], ((0, 0), (HALO - POOL_STATE, 0), (0, 0)))
    yps = pool_mixer(ups3, prefix16, w_pool16, s_pool[l], tm=T, pos0=past)

    qi_ht = qis.reshape(DB, T, IDX_HEADS, IDX_DIM).transpose(0, 2, 1, 3).reshape(DB, IDX_HEADS * T, IDX_DIM)
    w_ht = wis.reshape(DB, T, LANES)[:, :, :IDX_HEADS].transpose(0, 2, 1).reshape(DB, IDX_HEADS * T, 1)
    w_ht = jnp.broadcast_to(w_ht * (IDX_DIM ** -0.5), (DB, IDX_HEADS * T, LANES))
    pad_rows = lambda a: jnp.pad(a.reshape(DB, T, -1), ((0, 0), (0, page - T), (0, 0)))
    sc, thr = idx_sample(page_table, qi_ht, w_ht, cache_idx_k, pad_rows(kis16), t_new=T,
                         topk=min(TOPK_MAX, (past + T) // 4))

    n_phys = cache_k.shape[1]
    key_head_rows = lambda a: pad_rows(a).reshape(DB, page * n_heads, HEAD_DIM)
    q_ht = qs.reshape(DB, T, n_heads, HEAD_DIM).transpose(0, 2, 1, 3)
    o_ht = attn_sample(page_table, q_ht, cache_k.reshape(depth, n_phys, page * n_heads, HEAD_DIM),
                       cache_v.reshape(depth, n_phys, page * n_heads, HEAD_DIM),
                       key_head_rows(ks32), key_head_rows(vs32), sc, thr)
    yas = o_ht.transpose(0, 2, 1, 3).reshape(M, aw).astype(BF16)

    x2s, h2s = outproj(xs2, yps.reshape(M, pw), yas, w_out16, g_ffn[l], tm=M)
    y_sample = ffn(h2s, wg16, wu16, wd16, x2s, g_final, tm=M, tf=512).reshape(DB, T, D)

    k_sample = ks32.reshape(1, DB, T, n_heads, HEAD_DIM)
    v_sample = vs32.reshape(1, DB, T, n_heads, HEAD_DIM)
    idx_k_sample = kis32.reshape(1, DB, T, IDX_DIM)
    pool_sample = jnp.concatenate([state_pool[l][:, T:, :], ups3], axis=1)[None]

    return (y_prompt, y_sample, k_prompt, v_prompt, idx_k_prompt, pool_prompt,
            k_sample, v_sample, idx_k_sample, pool_sample)
```

```python
import functools
import math

import jax
import jax.numpy as jnp
from jax import lax
from jax.experimental import pallas as pl
from jax.experimental.pallas import tpu as pltpu

F32 = jnp.float32
BF16 = jnp.bfloat16

LANES = 128
HEAD_DIM = 128
IDX_DIM = 128
IDX_HEADS = 16
POOL_GROUPS = 4
POOL_WINDOWS = (2, 4, 8, 16)
POOL_STATE = max(POOL_WINDOWS) - 1
HALO = 16
TOPK_MAX = 256
ROW_TILE = 512
FFN_TILE = 512
ATTN_TQ = 256
ATTN_TK = 128
CHUNK_TRIP_FACTOR = 4
IDX_PAGES_PER_STEP = 32
KV_PAGES_PER_STEP = 16
ROPE_THETA = 10000.0
EPS = 1e-6
NEG = -1e30
MAX_SEARCH_STEPS = 1024
LOG2E = 1.4426950408889634
VMEM_LIMIT = 56 * 1024 * 1024

NT_DIMS = (((1,), (1,)), ((), ()))


def _cparams(n_grid):
    return pltpu.CompilerParams(dimension_semantics=("arbitrary",) * n_grid, vmem_limit_bytes=VMEM_LIMIT)


def _rms(x, g):
    return x * lax.rsqrt(jnp.mean(x * x, axis=-1, keepdims=True) + EPS) * g


PROJ_TN = 1024
PROJ_PIECE = 256


def _rope(y, cos, sin):
    return [y[:, g * HEAD_DIM:(g + 1) * HEAD_DIM] * cos
            + pltpu.roll(y[:, g * HEAD_DIM:(g + 1) * HEAD_DIM], HEAD_DIM // 2, axis=1) * sin
            for g in range(y.shape[1] // HEAD_DIM)]


def _mixproj_kernel(x_ref, g_ref, w_ref, wt_ref, cos_ref, sin_ref,
                    up_ref, q_ref, k32_ref, k16_ref, v32_ref, qi_ref, ki32_ref, ki16_ref, wi_ref, *rest):
    maybe_vt_ref, h_ref = rest if len(rest) == 2 else (None, rest[0])
    j = pl.program_id(1)

    def heads_to(pieces, refs, col0=0):
        for g, r in enumerate(pieces):
            for o_ref in refs:
                o_ref[:, col0 + g * HEAD_DIM:col0 + (g + 1) * HEAD_DIM] = r.astype(o_ref.dtype)

    @pl.when(j == 0)
    def _():
        h_ref[...] = _rms(x_ref[...], g_ref[...]).astype(h_ref.dtype)
        t = jnp.dot(h_ref[...], wt_ref[...], preferred_element_type=F32)
        heads_to(_rope(t[:, :IDX_DIM], cos_ref[...], sin_ref[...]), (ki32_ref, ki16_ref))
        wi_ref[...] = t[:, IDX_DIM:] * (IDX_HEADS ** -0.5)

    def col_pieces(fn):
        for c0 in range(0, PROJ_TN, PROJ_PIECE):
            fn(c0, jnp.dot(h_ref[...], w_ref[:, c0:c0 + PROJ_PIECE], preferred_element_type=F32))

    def rope_to(refs, col_base=0, scale=None):
        def fn(c0, y):
            y = y if scale is None else y * scale
            heads_to(_rope(y, cos_ref[...], sin_ref[...]), refs, col0=col_base + c0)
        return fn

    @pl.when(j == 0)
    def _():
        def fn(c0, y):
            up_ref[:, c0:c0 + PROJ_PIECE] = y
        col_pieces(fn)

    @pl.when(j == 1)
    def _():
        col_pieces(rope_to((q_ref,), scale=HEAD_DIM ** -0.5 * LOG2E))

    @pl.when(j == 2)
    def _():
        col_pieces(rope_to((k32_ref, k16_ref)))

    @pl.when(j == 3)
    def _():
        def fn(c0, y):
            v32_ref[:, c0:c0 + PROJ_PIECE] = y
            if maybe_vt_ref is not None:
                n_chunks, _, ck = maybe_vt_ref.shape
                for c in range(n_chunks):
                    for g in range(PROJ_PIECE // HEAD_DIM):
                        hs = slice(g * HEAD_DIM, (g + 1) * HEAD_DIM)
                        maybe_vt_ref[c, c0 + g * HEAD_DIM:c0 + (g + 1) * HEAD_DIM, :] = (
                            y[c * ck:(c + 1) * ck, hs].T.astype(maybe_vt_ref.dtype))
        col_pieces(fn)

    for t_qi in range(2):
        @pl.when(j == 4 + t_qi)
        def _():
            col_pieces(rope_to((qi_ref,), col_base=t_qi * PROJ_TN))


def mixproj(x, g, w_in16, w_tail, cos, sin, *, tm, vt_chunk=None):
    m, d = x.shape
    tn = PROJ_TN
    n_tiles = 6
    row = lambda width: pl.BlockSpec((tm, width), lambda i, j: (i, 0))
    rope_tiles = cos.shape[0] // tm
    rope_spec = pl.BlockSpec((tm, HEAD_DIM), lambda i, j: (i % rope_tiles, 0))
    widths = (tn, tn, tn, tn, tn, 2 * tn, IDX_DIM, IDX_DIM, LANES)
    dtypes = (F32, BF16, F32, BF16, F32, BF16, F32, BF16, F32)
    out_specs = [row(w) for w in widths]
    out_shape = [jax.ShapeDtypeStruct((m, w), dt) for w, dt in zip(widths, dtypes)]
    if vt_chunk is not None:
        out_specs.append(pl.BlockSpec((tm // vt_chunk, tn, vt_chunk), lambda i, j: (i, 0, 0)))
        out_shape.append(jax.ShapeDtypeStruct((m // vt_chunk, tn, vt_chunk), BF16))
    return pl.pallas_call(
        _mixproj_kernel,
        grid=(m // tm, n_tiles),
        in_specs=[
            row(d),
            pl.BlockSpec((1, d), lambda i, j: (0, 0)),
            pl.BlockSpec((d, tn), lambda i, j: (0, j)),
            pl.BlockSpec(w_tail.shape, lambda i, j: (0, 0)),
            rope_spec, rope_spec,
        ],
        out_specs=out_specs,
        out_shape=out_shape,
        scratch_shapes=[pltpu.VMEM((tm, d), BF16)],
        compiler_params=_cparams(2),
        name="mixproj",
    )(x, g.reshape(1, d), w_in16, w_tail, cos, sin)


def _pool_kernel(*refs, tm, pos0, has_prev):
    if has_prev:
        up_ref, prev_ref, prefix_ref, w_ref, s_ref, o_ref, ext_ref = refs
    else:
        up_ref, prefix_ref, w_ref, s_ref, o_ref, ext_ref = refs
    i = pl.program_id(1)
    if has_prev:
        halo = jnp.where(i == 0, prefix_ref[...], prev_ref[...])
    else:
        halo = prefix_ref[...]
    ext_ref[0:HALO, :] = halo
    ext_ref[HALO:HALO + tm, :] = up_ref[...]
    pos = pos0 + i * tm + lax.broadcasted_iota(jnp.int32, (tm, 1), 0)
    gw = up_ref.shape[1] // POOL_GROUPS
    for g, w in enumerate(POOL_WINDOWS):
        sl = slice(g * gw, (g + 1) * gw)
        cur = ext_ref[HALO:HALO + tm, sl]
        s = cur
        for j in range(1, w):
            s = s + ext_ref[HALO - j:HALO - j + tm, sl]
        cnt = jnp.minimum(w, pos + 1).astype(F32)
        d = s / cnt - cur
        y = jnp.dot(d.astype(BF16), w_ref[g], preferred_element_type=F32) * s_ref[:, sl]
        o_ref[:, sl] = y.astype(o_ref.dtype)


def pool_mixer(up, prefix16, w_pool, s_pool, *, tm, pos0):
    b, t, wd = up.shape
    has_prev = t > tm
    gw = wd // POOL_GROUPS
    in_specs = [pl.BlockSpec((None, tm, wd), lambda bi, i: (bi, i, 0))]
    args = [up]
    if has_prev:
        r = tm // HALO
        in_specs.append(pl.BlockSpec((None, HALO, wd), lambda bi, i: (bi, jnp.maximum(i * r - 1, 0), 0)))
        args.append(up)
    in_specs += [
        pl.BlockSpec((None, HALO, wd), lambda bi, i: (bi, 0, 0)),
        pl.BlockSpec((POOL_GROUPS, gw, gw), lambda bi, i: (0, 0, 0)),
        pl.BlockSpec((1, wd), lambda bi, i: (0, 0)),
    ]
    args += [prefix16, w_pool, s_pool.reshape(1, wd)]
    return pl.pallas_call(
        functools.partial(_pool_kernel, tm=tm, pos0=pos0, has_prev=has_prev),
        grid=(b, t // tm),
        in_specs=in_specs,
        out_specs=pl.BlockSpec((None, tm, wd), lambda bi, i: (bi, i, 0)),
        out_shape=jax.ShapeDtypeStruct((b, t, wd), BF16),
        scratch_shapes=[pltpu.VMEM((HALO + tm, wd), F32)],
        compiler_params=_cparams(2),
        name="pool_mixer",
    )(*args)


PEEL_BRACKET = 8
UNTESTED_HALVINGS = 9


def _select_threshold(count_ge, max_below, rmin, rmax, n_valid, kq):
    def any_row(pred):
        return jnp.max(jnp.where(pred, 1.0, 0.0))

    def halve(lo, hi, cl, ch):
        mid = lo + 0.5 * (hi - lo)
        stuck = (mid <= lo) | (mid >= hi)
        c = count_ge(mid)
        ge = c >= kq
        return (jnp.where(ge, mid, lo), jnp.where(ge, hi, mid), jnp.where(ge, c, cl), jnp.where(ge, ch, c),
                jnp.where(stuck, 1.0, 0.0))

    def halving_loop(st, bracket_cap):
        def open_rows(cl, ch, stuck):
            return any_row((cl != kq) & (stuck <= 0.0) & (cl - ch > bracket_cap))

        def cond(st):
            return jnp.logical_and(st[0] < MAX_SEARCH_STEPS, st[1] > 0.0)

        def body(st):
            steps, _, lo, hi, cl, ch, _ = st
            lo, hi, cl, ch, stuck = halve(lo, hi, cl, ch)
            return steps + 1, open_rows(cl, ch, stuck), lo, hi, cl, ch, stuck

        st = (st[0], open_rows(st[4], st[5], st[6])) + tuple(st[2:])
        return lax.while_loop(cond, body, st)

    hi0 = rmax + jnp.maximum(jnp.abs(rmax) * 1e-6, 1e-30)
    zero = jnp.zeros_like(rmax)
    st = lax.fori_loop(0, UNTESTED_HALVINGS, lambda _, st: halve(*st[:4]), (rmin, hi0, n_valid, zero, zero))
    st = halving_loop((jnp.int32(0), jnp.float32(0.0)) + tuple(st), float(PEEL_BRACKET))
    _, _, lo, hi, cl, ch, stuck = st

    walk = (cl != kq) & (stuck <= 0.0)

    def walk_cond(st):
        return jnp.logical_and(st[0] < PEEL_BRACKET, st[1] > 0.0)

    def walk_body(st):
        steps, _, top, n_top = st
        active = walk & (n_top < kq)
        top = jnp.where(active, max_below(top), top)
        n_top = jnp.where(active, n_top + 1.0, n_top)
        return steps + 1, any_row(walk & (n_top < kq)), top, n_top

    _, _, top, _ = lax.while_loop(walk_cond, walk_body, (jnp.int32(0), any_row(walk), hi, ch))
    thr = jnp.where(walk, top, lo)
    n_ge = count_ge(thr)

    redo = n_ge != kq
    st = halving_loop((jnp.int32(0), jnp.float32(0.0), lo, hi, cl, ch, jnp.where(redo, stuck, 1.0)), 0.0)
    return jnp.where(redo, st[2], thr), jnp.where(redo, st[4], n_ge)


def _last_tied_index(count_tied_upto, need, n_keys):
    def body(_, st):
        lo, hi = st
        mid = jnp.floor((lo + hi) * 0.5)
        ok = count_tied_upto(mid) >= need
        return jnp.where(ok, lo, mid), jnp.where(ok, mid, hi)

    steps = max(1, math.ceil(math.log2(n_keys + 1)))
    _, hi = lax.fori_loop(0, steps, body, (jnp.full(need.shape, -1.0, F32), jnp.full(need.shape, n_keys - 1.0, F32)))
    return hi


def _attn_prompt_kernel(q_ref, qi_ref, wit_ref, k_ref, vt_ref, ki_ref, o_ref,
                        sc_ref, ext_ref, m_ref, l_ref, acc_ref, *, tk, topk, n_heads):
    i = pl.program_id(1)
    tq = q_ref.shape[0]
    nc = ((i + 1) * tq + tk - 1) // tk
    qpos = i * tq + lax.broadcasted_iota(jnp.int32, (1, tq), 1)
    key_row = lax.broadcasted_iota(jnp.int32, (tk, tq), 0)
    w_rows = wit_ref[...] * (IDX_DIM ** -0.5)

    def idx_chunk(c, carry):
        off = pl.multiple_of(c * tk, tk)
        kic = ki_ref[pl.ds(off, tk), :]
        acc = jnp.zeros((tk, tq), F32)
        for h in range(IDX_HEADS):
            s = lax.dot_general(kic, qi_ref[:, h * IDX_DIM:(h + 1) * IDX_DIM], NT_DIMS,
                                preferred_element_type=F32)
            acc = acc + jnp.maximum(s, 0.0) * w_rows[h:h + 1, :]
        causal = off + key_row <= qpos
        sc = jnp.where(causal, acc, -jnp.inf)
        sc_ref[pl.ds(off, tk), :] = sc
        ext_ref[0] = jnp.minimum(ext_ref[0], jnp.min(jnp.where(causal, acc, jnp.inf).reshape(tk // 8, 8, tq), axis=0))
        ext_ref[1] = jnp.maximum(ext_ref[1], jnp.max(sc.reshape(tk // 8, 8, tq), axis=0))
        return carry

    ext_ref[0] = jnp.full((8, tq), jnp.inf, F32)
    ext_ref[1] = jnp.full((8, tq), -jnp.inf, F32)

    def chunk_loop(n_chunks, chunk_fn):
        base = tq // tk
        per_trip = CHUNK_TRIP_FACTOR * base

        def run(first, count):
            for u in range(count):
                chunk_fn(first + u, 0)

        def body(t, carry):
            run(t * per_trip, per_trip)
            return carry
        lax.fori_loop(0, n_chunks // per_trip, body, 0)
        for r in range(1, CHUNK_TRIP_FACTOR):
            @pl.when(n_chunks % per_trip == r * base)
            def _():
                run((n_chunks // per_trip) * per_trip, r * base)

    chunk_loop(nc, idx_chunk)

    lanes_par = 4

    def reduce_keys(fn, init, red):
        rows = tq

        def body(c, a):
            blk = sc_ref[pl.ds(pl.multiple_of(c * rows, rows), rows), :]
            return fn(a, blk.reshape(lanes_par, rows // (8 * lanes_par), 8, tq))
        a = lax.fori_loop(0, (nc * tk) // rows, body, jnp.full((lanes_par, 8, tq), init, F32))
        return red(red(a, axis=0), axis=0, keepdims=True)

    def count_ge(t):
        return reduce_keys(lambda a, blk: a + jnp.sum(jnp.where(blk >= t, 1.0, 0.0), axis=1), 0.0, jnp.sum)

    def max_below(t):
        return reduce_keys(lambda a, blk: jnp.maximum(a, jnp.max(jnp.where(blk < t, blk, -jnp.inf), axis=1)),
                           -jnp.inf, jnp.max)

    rmin = jnp.min(ext_ref[0], axis=0, keepdims=True)
    rmax = jnp.max(ext_ref[1], axis=0, keepdims=True)
    n_valid = (qpos + 1).astype(F32)
    kq = jnp.minimum(float(topk), n_valid)
    thr, n_ge = _select_threshold(count_ge, max_below, rmin, rmax, n_valid, kq)

    @pl.when(jnp.max(n_ge - kq) > 0.0)
    def _():
        def key_sum(fn):
            def body(c, a):
                off = pl.multiple_of(c * tk, tk)
                hit = fn(sc_ref[pl.ds(off, tk), :], (off + key_row).astype(F32))
                return a + jnp.sum(jnp.where(hit, 1.0, 0.0), axis=0, keepdims=True)
            return lax.fori_loop(0, nc, body, jnp.zeros((1, tq), F32))

        need = kq - key_sum(lambda blk, key: blk > thr)
        last = _last_tied_index(lambda j: key_sum(lambda blk, key: (blk == thr) & (key <= j)), need,
                                sc_ref.shape[0])

        def drop(c, carry):
            off = pl.multiple_of(c * tk, tk)
            blk = sc_ref[pl.ds(off, tk), :]
            beyond = (blk == thr) & ((off + key_row).astype(F32) > last)
            sc_ref[pl.ds(off, tk), :] = jnp.where(beyond, -jnp.inf, blk)
            return carry
        lax.fori_loop(0, nc, drop, 0)

    m_ref[...] = jnp.full(m_ref.shape, -jnp.inf, F32)
    l_ref[...] = jnp.zeros(l_ref.shape, F32)
    acc_ref[...] = jnp.zeros(acc_ref.shape, F32)

    def att_chunk(c, carry):
        off = pl.multiple_of(c * tk, tk)
        mask = sc_ref[pl.ds(off, tk), :] >= thr
        for h in range(n_heads):
            hs = slice(h * HEAD_DIM, (h + 1) * HEAD_DIM)
            s = lax.dot_general(k_ref[pl.ds(off, tk), hs], q_ref[:, hs], NT_DIMS,
                                preferred_element_type=F32)
            s = jnp.where(mask, s, NEG)
            m_prev = m_ref[h]
            m_new = jnp.maximum(m_prev, jnp.max(s, axis=0, keepdims=True))
            alpha = jnp.exp2(m_prev - m_new)
            p = jnp.exp2(s - m_new)
            l_ref[h] = alpha * l_ref[h] + jnp.sum(p, axis=0, keepdims=True)
            m_ref[h] = m_new
            pv = jnp.dot(vt_ref[c, hs, :], p.astype(BF16), preferred_element_type=F32)
            acc_ref[h] = alpha * acc_ref[h] + pv
        return carry

    chunk_loop(nc, att_chunk)

    for h in range(n_heads):
        hs = slice(h * HEAD_DIM, (h + 1) * HEAD_DIM)
        o_ref[:, hs] = (acc_ref[h] / l_ref[h]).T.astype(o_ref.dtype)


def attn_prompt(q, qi, wit, k, vt, ki, *, topk):
    b, s, aw = q.shape
    n_heads = aw // HEAD_DIM
    tq = ATTN_TQ
    tk = vt.shape[3]
    kern = functools.partial(_attn_prompt_kernel, tk=ATTN_TK, topk=topk, n_heads=n_heads)
    return pl.pallas_call(
        kern,
        grid=(b, s // tq),
        in_specs=[
            pl.BlockSpec((None, tq, aw), lambda bi, i: (bi, i, 0)),
            pl.BlockSpec((None, tq, qi.shape[2]), lambda bi, i: (bi, i, 0)),
            pl.BlockSpec((None, IDX_HEADS, tq), lambda bi, i: (bi, 0, i)),
            pl.BlockSpec((None, s, aw), lambda bi, i: (bi, 0, 0)),
            pl.BlockSpec((None, s // tk, aw, tk), lambda bi, i: (bi, 0, 0, 0)),
            pl.BlockSpec((None, s, IDX_DIM), lambda bi, i: (bi, 0, 0)),
        ],
        out_specs=pl.BlockSpec((None, tq, aw), lambda bi, i: (bi, i, 0)),
        out_shape=jax.ShapeDtypeStruct((b, s, aw), BF16),
        scratch_shapes=[
            pltpu.VMEM((s, tq), F32),
            pltpu.VMEM((2, 8, tq), F32),
            pltpu.VMEM((n_heads, 1, tq), F32),
            pltpu.VMEM((n_heads, 1, tq), F32),
            pltpu.VMEM((n_heads, HEAD_DIM, tq), F32),
        ],
        compiler_params=_cparams(2),
        name="attn_prompt",
    )(q, qi, wit, k, vt, ki)


def _idx_sample_kernel(pt_ref, qi_ref, w_ref, *rest, n_pages, group, t_new, topk):
    cik_refs = rest[:group]
    kin_ref, sc_ref, thr_ref = rest[group:]
    b, p = pl.program_id(0), pl.program_id(1)
    db = sc_ref.shape[0]
    rows = qi_ref.shape[0]

    def scores(keys_bf16):
        s = lax.dot_general(qi_ref[...], keys_bf16, NT_DIMS, preferred_element_type=F32)
        r = jnp.maximum(s, 0.0) * w_ref[...]
        return jnp.sum(r.reshape(rows // t_new, t_new, LANES), axis=0)

    for g in range(group):
        sc_ref[b, p * group + g] = scores(cik_refs[g][...].astype(BF16))

    @pl.when(p == 0)
    def _():
        s = scores(kin_ref[...])
        tok = lax.broadcasted_iota(jnp.int32, (t_new, LANES), 0)
        lane = lax.broadcasted_iota(jnp.int32, (t_new, LANES), 1)
        sc_ref[b, n_pages] = jnp.where((lane <= tok) & (lane < t_new), s, -jnp.inf)

    @pl.when((b == db - 1) & (p == pl.num_programs(1) - 1))
    def _():
        def reduce_keys(fn, red):
            return red(red(fn(sc_ref[...]), axis=1), axis=2, keepdims=True)

        def count_ge(t):
            return reduce_keys(lambda x: jnp.where(x >= t[:, None], 1.0, 0.0), jnp.sum)

        def max_below(t):
            return reduce_keys(lambda x: jnp.where(x < t[:, None], x, -jnp.inf), jnp.max)

        rmax = reduce_keys(lambda x: x, jnp.max)
        rmin = reduce_keys(lambda x: jnp.where(x == -jnp.inf, jnp.inf, x), jnp.min)
        tokc = lax.broadcasted_iota(jnp.int32, (db, t_new, 1), 1)
        n_valid = (n_pages * LANES + tokc + 1).astype(F32)
        kq = jnp.minimum(float(topk), n_valid)
        thr, n_ge = _select_threshold(count_ge, max_below, rmin, rmax, n_valid, kq)
        thr_ref[...] = jnp.broadcast_to(thr, thr_ref.shape)

        @pl.when(jnp.max(n_ge - kq) > 0.0)
        def _():
            n_slabs = n_pages + 1
            key = (lax.broadcasted_iota(jnp.int32, (1, n_slabs, 1, LANES), 1) * LANES
                   + lax.broadcasted_iota(jnp.int32, (1, n_slabs, 1, LANES), 3)).astype(F32)
            thr4 = thr[:, None]

            def key_sum(hit):
                return jnp.sum(jnp.sum(jnp.where(hit, 1.0, 0.0), axis=1), axis=2, keepdims=True)

            need = kq - key_sum(sc_ref[...] > thr4)
            last = _last_tied_index(lambda j: key_sum((sc_ref[...] == thr4) & (key <= j[:, None])), need,
                                    n_slabs * LANES)
            sc = sc_ref[...]
            sc_ref[...] = jnp.where((sc == thr4) & (key > last[:, None]), -jnp.inf, sc)


def _page_map(b, p, pt, *, g, group):
    return (0, pt[b, p * group + g], 0, 0)


def idx_sample(page_table, qi_ht, w_ht, cache_idx_k, ki_new_pad, *, t_new, topk):
    db, n_pages = page_table.shape
    group = math.gcd(n_pages, IDX_PAGES_PER_STEP)
    rows = qi_ht.shape[1]
    page = cache_idx_k.shape[2]
    assert n_pages % group == 0
    kern = functools.partial(_idx_sample_kernel, n_pages=n_pages, group=group, t_new=t_new, topk=topk)
    page_specs = [pl.BlockSpec((None, None, page, IDX_DIM), functools.partial(_page_map, g=g, group=group))
                  for g in range(group)]
    grid_spec = pltpu.PrefetchScalarGridSpec(
        num_scalar_prefetch=1,
        grid=(db, n_pages // group),
        in_specs=[
            pl.BlockSpec((None, rows, IDX_DIM), lambda b, p, pt: (b, 0, 0)),
            pl.BlockSpec((None, rows, LANES), lambda b, p, pt: (b, 0, 0)),
            *page_specs,
            pl.BlockSpec((None, page, IDX_DIM), lambda b, p, pt: (b, 0, 0)),
        ],
        out_specs=[
            pl.BlockSpec((db, n_pages + 1, t_new, LANES), lambda b, p, pt: (0, 0, 0, 0)),
            pl.BlockSpec((db, t_new, LANES), lambda b, p, pt: (0, 0, 0)),
        ],
    )
    return pl.pallas_call(
        kern,
        grid_spec=grid_spec,
        out_shape=[jax.ShapeDtypeStruct((db, n_pages + 1, t_new, LANES), F32),
                   jax.ShapeDtypeStruct((db, t_new, LANES), F32)],
        compiler_params=_cparams(2),
        name="idx_sample",
    )(page_table, qi_ht, w_ht, *([cache_idx_k] * group), ki_new_pad)


def _attn_sample_kernel(pt_ref, q_ref, *rest, group, page, n_heads):
    ck_refs = rest[:group]
    cv_refs = rest[group:2 * group]
    kn_ref, vn_ref, sc_ref, scn_ref, thr_ref, o_ref, m_ref, l_ref, acc_ref = rest[2 * group:]
    p = pl.program_id(1)

    @pl.when(p == 0)
    def _():
        m_ref[...] = jnp.full(m_ref.shape, -jnp.inf, F32)
        l_ref[...] = jnp.zeros(l_ref.shape, F32)
        acc_ref[...] = jnp.zeros(acc_ref.shape, F32)

    thr = thr_ref[...]

    def head_rows(ref, h):
        return ref[pl.ds(h, page, stride=n_heads), :].astype(BF16)

    def attend(k_refs, v_refs, sc, sc_thr):
        n = len(k_refs)
        s = jnp.stack([
            lax.dot_general(q_ref[h], jnp.concatenate([head_rows(r, h) for r in k_refs], axis=0), NT_DIMS,
                            preferred_element_type=F32)
            for h in range(n_heads)])
        s = jnp.where(sc[None] >= sc_thr[None], s, NEG)
        m_prev = m_ref[...]
        m_new = jnp.maximum(m_prev, jnp.max(s, axis=2, keepdims=True))
        alpha = jnp.exp2(m_prev - m_new)
        pr = jnp.exp2(s - jnp.concatenate([m_new] * n, axis=2))
        l_ref[...] = alpha * l_ref[...] + jnp.sum(pr, axis=2, keepdims=True)
        m_ref[...] = m_new
        pb = pr.astype(BF16)
        for h in range(n_heads):
            vh = jnp.concatenate([head_rows(r, h) for r in v_refs], axis=0)
            acc_ref[h] = alpha[h] * acc_ref[h] + jnp.dot(pb[h], vh, preferred_element_type=F32)

    attend(ck_refs, cv_refs, jnp.concatenate([sc_ref[g] for g in range(group)], axis=1),
           jnp.concatenate([thr] * group, axis=1))

    @pl.when(p == pl.num_programs(1) - 1)
    def _():
        attend([kn_ref], [vn_ref], scn_ref[...], thr)
        o_ref[...] = acc_ref[...] / l_ref[...]


def attn_sample(page_table, q_ht, cache_k, cache_v, k_new_pad, v_new_pad, sc, thr):
    db, n_pages = page_table.shape
    n_heads, t_new = q_ht.shape[1], q_ht.shape[2]
    prow = cache_k.shape[2]
    page = prow // n_heads
    group = math.gcd(n_pages, KV_PAGES_PER_STEP)
    kern = functools.partial(_attn_sample_kernel, group=group, page=page, n_heads=n_heads)
    cache_specs = [pl.BlockSpec((None, None, prow, HEAD_DIM), functools.partial(_page_map, g=g, group=group))
                   for g in range(group)]
    new_spec = pl.BlockSpec((None, prow, HEAD_DIM), lambda b, p, pt: (b, 0, 0))
    head_spec = pl.BlockSpec((None, n_heads, t_new, HEAD_DIM), lambda b, p, pt: (b, 0, 0, 0))
    grid_spec = pltpu.PrefetchScalarGridSpec(
        num_scalar_prefetch=1,
        grid=(db, n_pages // group),
        in_specs=[
            head_spec,
            *cache_specs, *cache_specs, new_spec, new_spec,
            pl.BlockSpec((None, group, t_new, LANES), lambda b, p, pt: (b, p, 0, 0)),
            pl.BlockSpec((None, None, t_new, LANES), lambda b, p, pt: (b, n_pages, 0, 0)),
            pl.BlockSpec((None, t_new, LANES), lambda b, p, pt: (b, 0, 0)),
        ],
        out_specs=head_spec,
        scratch_shapes=[
            pltpu.VMEM((n_heads, t_new, LANES), F32),
            pltpu.VMEM((n_heads, t_new, LANES), F32),
            pltpu.VMEM((n_heads, t_new, HEAD_DIM), F32),
        ],
    )
    return pl.pallas_call(
        kern,
        grid_spec=grid_spec,
        out_shape=jax.ShapeDtypeStruct((db, n_heads, t_new, HEAD_DIM), F32),
        compiler_params=_cparams(2),
        name="attn_sample",
    )(page_table, q_ht, *([cache_k] * group), *([cache_v] * group), k_new_pad, v_new_pad, sc, sc, thr)


def _outproj_kernel(x_ref, yp_ref, ya_ref, wt_ref, wb_ref, g_ref, x2_ref, h2_ref):
    y = jnp.dot(yp_ref[...], wt_ref[...], preferred_element_type=F32)
    y = y + jnp.dot(ya_ref[...], wb_ref[...], preferred_element_type=F32)
    x2 = x_ref[...] + y
    x2_ref[...] = x2
    h2_ref[...] = _rms(x2, g_ref[...]).astype(h2_ref.dtype)


def outproj(x, yp, ya, w_out, g, *, tm):
    m, d = x.shape
    kp, ka = yp.shape[1], ya.shape[1]
    assert kp == ka and w_out.shape == (kp + ka, d)
    return pl.pallas_call(
        _outproj_kernel,
        grid=(m // tm,),
        in_specs=[
            pl.BlockSpec((tm, d), lambda i: (i, 0)),
            pl.BlockSpec((tm, kp), lambda i: (i, 0)),
            pl.BlockSpec((tm, ka), lambda i: (i, 0)),
            pl.BlockSpec((kp, d), lambda i: (0, 0)),
            pl.BlockSpec((ka, d), lambda i: (1, 0)),
            pl.BlockSpec((1, d), lambda i: (0, 0)),
        ],
        out_specs=[pl.BlockSpec((tm, d), lambda i: (i, 0)), pl.BlockSpec((tm, d), lambda i: (i, 0))],
        out_shape=[jax.ShapeDtypeStruct((m, d), F32), jax.ShapeDtypeStruct((m, d), BF16)],
        compiler_params=_cparams(1),
        name="outproj",
    )(x, yp, ya, w_out, w_out, g.reshape(1, d))


def _ffn_kernel(h2_ref, wg_ref, wu_ref, wd_ref, x2_ref, g_ref, o_ref, acc_ref):
    f = pl.program_id(1)

    @pl.when(f == 0)
    def _():
        acc_ref[...] = jnp.zeros(acc_ref.shape, F32)

    h2 = h2_ref[...]
    gate = jnp.dot(h2, wg_ref[...], preferred_element_type=F32)
    up = jnp.dot(h2, wu_ref[...], preferred_element_type=F32)
    a = (gate * jax.nn.sigmoid(gate) * up).astype(BF16)
    acc_ref[...] += jnp.dot(a, wd_ref[...], preferred_element_type=F32)

    @pl.when(f == pl.num_programs(1) - 1)
    def _():
        o_ref[...] = _rms(x2_ref[...] + acc_ref[...], g_ref[...])


def ffn(h2, wg, wu, wd, x2, g, *, tm, tf):
    m, d = h2.shape
    dff = wg.shape[1]
    return pl.pallas_call(
        _ffn_kernel,
        grid=(m // tm, dff // tf),
        in_specs=[
            pl.BlockSpec((tm, d), lambda i, f: (i, 0)),
            pl.BlockSpec((d, tf), lambda i, f: (0, f)),
            pl.BlockSpec((d, tf), lambda i, f: (0, f)),
            pl.BlockSpec((tf, d), lambda i, f: (f, 0)),
            pl.BlockSpec((tm, d), lambda i, f: (i, 0)),
            pl.BlockSpec((1, d), lambda i, f: (0, 0)),
        ],
        out_specs=pl.BlockSpec((tm, d), lambda i, f: (i, 0)),
        out_shape=jax.ShapeDtypeStruct((m, d), F32),
        scratch_shapes=[pltpu.VMEM((tm, d), F32)],
        compiler_params=_cparams(2),
        name="ffn",
    )(h2, wg, wu, wd, x2, g.reshape(1, d))


def _rope_tables(pos):
    half = HEAD_DIM // 2
    inv = ROPE_THETA ** (-jnp.arange(half, dtype=F32) / half)
    ang = pos.astype(F32)[:, None] * inv[None, :]
    cos, sin = jnp.cos(ang), jnp.sin(ang)
    return jnp.concatenate([cos, cos], axis=-1), jnp.concatenate([-sin, sin], axis=-1)


def _mix_inputs(x2d, pos_rows, g_mix, w_in16, w_tail, *, tm, vt_chunk=None):
    cos, sin = _rope_tables(pos_rows)
    return mixproj(x2d, g_mix, w_in16, w_tail, cos, sin, tm=tm, vt_chunk=vt_chunk)


def kernel(x_prompt, x_sample, cache_k, cache_v, cache_idx_k, state_pool, page_table, g_mix, w_in, w_pool,
           s_pool, w_out, g_ffn, w_gate, w_up, w_down, g_final):
    B, S, D = x_prompt.shape
    DB, T, _ = x_sample.shape
    depth = w_in.shape[0]
    assert depth == 1
    n_pages = page_table.shape[1]
    page = cache_k.shape[2]
    n_heads = cache_k.shape[3]
    aw = n_heads * HEAD_DIM
    pw = w_pool.shape[1] * w_pool.shape[2]
    past = n_pages * page
    l = 0

    wide = pw + 3 * aw + IDX_HEADS * IDX_DIM
    assert (pw, aw, wide) == (PROJ_TN, PROJ_TN, 6 * PROJ_TN) and w_in.shape[2] == wide + IDX_DIM + IDX_HEADS
    w_in16 = w_in[l].astype(BF16)
    w_tail = jnp.pad(w_in16[:, wide:], ((0, 0), (0, LANES - IDX_HEADS)))
    w_pool16 = w_pool[l].astype(BF16)
    w_out16 = w_out[l].astype(BF16)
    wg16, wu16, wd16 = w_gate[l].astype(BF16), w_up[l].astype(BF16), w_down[l].astype(BF16)

    xp2 = x_prompt.reshape(B * S, D)
    pos_p = jnp.arange(S)
    tk = ATTN_TK
    up, q, k32, k16, v32, qi, ki32, ki16, wi, vt = _mix_inputs(xp2, pos_p, g_mix[l], w_in16, w_tail, tm=ROW_TILE,
                                                               vt_chunk=tk)
    up3 = up.reshape(B, S, pw)
    yp = pool_mixer(up3, jnp.zeros((B, HALO, pw), F32), w_pool16, s_pool[l], tm=ROW_TILE, pos0=0)
    wit = wi.reshape(B, S, LANES)[:, :, :IDX_HEADS].transpose(0, 2, 1)
    ya = attn_prompt(q.reshape(B, S, aw), qi.reshape(B, S, -1), wit, k16.reshape(B, S, aw),
                     vt.reshape(B, S // tk, aw, tk), ki16.reshape(B, S, IDX_DIM), topk=min(TOPK_MAX, S // 4))
    x2, h2 = outproj(xp2, yp.reshape(B * S, pw), ya.reshape(B * S, aw), w_out16, g_ffn[l], tm=ROW_TILE)
    y_prompt = ffn(h2, wg16, wu16, wd16, x2, g_final, tm=ROW_TILE, tf=FFN_TILE).reshape(B, S, D)

    k_prompt = k32.reshape(1, B, S, n_heads, HEAD_DIM)
    v_prompt = v32.reshape(1, B, S, n_heads, HEAD_DIM)
    idx_k_prompt = ki32.reshape(1, B, S, IDX_DIM)
    pool_prompt = up3[:, S - POOL_STATE:, :][None]

    M = DB * T
    xs2 = x_sample.reshape(M, D)
    pos_s = jnp.tile(past + jnp.arange(T), DB)
    ups, qs, ks32, _, vs32, qis, kis32, kis16, wis = _mix_inputs(xs2, pos_s, g_mix[l], w_in16, w_tail, tm=M)
    ups3 = ups.reshape(DB, T, pw)
    prefix16 = jnp.pad(state_pool[l], ((0, 0), (HALO - POOL_STATE, 0), (0, 0)))
    yps = pool_mixer(ups3, prefix16, w_pool16, s_pool[l], tm=T, pos0=past)

    qi_ht = qis.reshape(DB, T, IDX_HEADS, IDX_DIM).transpose(0, 2, 1, 3).reshape(DB, IDX_HEADS * T, IDX_DIM)
    w_ht = wis.reshape(DB, T, LANES)[:, :, :IDX_HEADS].transpose(0, 2, 1).reshape(DB, IDX_HEADS * T, 1)
    w_ht = jnp.broadcast_to(w_ht * (IDX_DIM ** -0.5), (DB, IDX_HEADS * T, LANES))
    pad_rows = lambda a: jnp.pad(a.reshape(DB, T, -1), ((0, 0), (0, page - T), (0, 0)))
    sc, thr = idx_sample(page_table, qi_ht, w_ht, cache_idx_k, pad_rows(kis16), t_new=T,
                         topk=min(TOPK_MAX, (past + T) // 4))

    n_phys = cache_k.shape[1]
    key_head_rows = lambda a: pad_rows(a).reshape(DB, page * n_heads, HEAD_DIM)
    q_ht = qs.reshape(DB, T, n_heads, HEAD_DIM).transpose(0, 2, 1, 3)
    o_ht = attn_sample(page_table, q_ht, cache_k.reshape(depth, n_phys, page * n_heads, HEAD_DIM),
                       cache_v.reshape(depth, n_phys, page * n_heads, HEAD_DIM),
                       key_head_rows(ks32), key_head_rows(vs32), sc, thr)
    yas = o_ht.transpose(0, 2, 1, 3).reshape(M, aw).astype(BF16)

    x2s, h2s = outproj(xs2, yps.reshape(M, pw), yas, w_out16, g_ffn[l], tm=M)
    y_sample = ffn(h2s, wg16, wu16, wd16, x2s, g_final, tm=M, tf=FFN_TILE).reshape(DB, T, D)

    k_sample = ks32.reshape(1, DB, T, n_heads, HEAD_DIM)
    v_sample = vs32.reshape(1, DB, T, n_heads, HEAD_DIM)
    idx_k_sample = kis32.reshape(1, DB, T, IDX_DIM)
    pool_sample = jnp.concatenate([state_pool[l][:, T:, :], ups3], axis=1)[None]

    return (y_prompt, y_sample, k_prompt, v_prompt, idx_k_prompt, pool_prompt,
            k_sample, v_sample, idx_k_sample, pool_sample)
```

```python
import functools
import math

import jax
import jax.numpy as jnp
from jax import lax
from jax.experimental import pallas as pl
from jax.experimental.pallas import tpu as pltpu

F32 = jnp.float32
BF16 = jnp.bfloat16

LANES = 128
HEAD_DIM = 128
IDX_DIM = 128
IDX_HEADS = 16
POOL_GROUPS = 4
POOL_WINDOWS = (2, 4, 8, 16)
POOL_STATE = max(POOL_WINDOWS) - 1
HALO = 16
TOPK_MAX = 256
ROW_TILE = 512
FFN_TILE = 512
ATTN_TQ = 256
ATTN_TK = 128
CHUNK_TRIP_FACTOR = 4
IDX_PAGES_PER_STEP = 32
KV_PAGES_PER_STEP = 16
ROPE_THETA = 10000.0
EPS = 1e-6
NEG = -1e30
MAX_SEARCH_STEPS = 1024
LOG2E = 1.4426950408889634
VMEM_LIMIT = 56 * 1024 * 1024

NT_DIMS = (((1,), (1,)), ((), ()))


def _cparams(n_grid):
    return pltpu.CompilerParams(dimension_semantics=("arbitrary",) * n_grid, vmem_limit_bytes=VMEM_LIMIT)


def _rms(x, g):
    return x * lax.rsqrt(jnp.mean(x * x, axis=-1, keepdims=True) + EPS) * g


PROJ_TN = 1024
PROJ_PIECE = 256


def _rope(y, cos, sin):
    return [y[:, g * HEAD_DIM:(g + 1) * HEAD_DIM] * cos
            + pltpu.roll(y[:, g * HEAD_DIM:(g + 1) * HEAD_DIM], HEAD_DIM // 2, axis=1) * sin
            for g in range(y.shape[1] // HEAD_DIM)]


def _mixproj_kernel(x_ref, g_ref, w_ref, wt_ref, cos_ref, sin_ref,
                    up_ref, q_ref, k32_ref, k16_ref, v32_ref, qi_ref, ki32_ref, ki16_ref, wi_ref, *rest):
    maybe_vt_ref, h_ref = rest if len(rest) == 2 else (None, rest[0])
    j = pl.program_id(1)

    def heads_to(pieces, refs, col0=0):
        for g, r in enumerate(pieces):
            for o_ref in refs:
                o_ref[:, col0 + g * HEAD_DIM:col0 + (g + 1) * HEAD_DIM] = r.astype(o_ref.dtype)

    @pl.when(j == 0)
    def _():
        h_ref[...] = _rms(x_ref[...], g_ref[...]).astype(h_ref.dtype)
        t = jnp.dot(h_ref[...], wt_ref[...], preferred_element_type=F32)
        heads_to(_rope(t[:, :IDX_DIM], cos_ref[...], sin_ref[...]), (ki32_ref, ki16_ref))
        wi_ref[...] = t[:, IDX_DIM:] * (IDX_HEADS ** -0.5)

    def col_pieces(fn):
        for c0 in range(0, PROJ_TN, PROJ_PIECE):
            fn(c0, jnp.dot(h_ref[...], w_ref[:, c0:c0 + PROJ_PIECE], preferred_element_type=F32))

    def rope_to(refs, col_base=0, scale=None):
        def fn(c0, y):
            y = y if scale is None else y * scale
            heads_to(_rope(y, cos_ref[...], sin_ref[...]), refs, col0=col_base + c0)
        return fn

    @pl.when(j == 0)
    def _():
        def fn(c0, y):
            up_ref[:, c0:c0 + PROJ_PIECE] = y
        col_pieces(fn)

    @pl.when(j == 1)
    def _():
        col_pieces(rope_to((q_ref,), scale=HEAD_DIM ** -0.5 * LOG2E))

    @pl.when(j == 2)
    def _():
        col_pieces(rope_to((k32_ref, k16_ref)))

    @pl.when(j == 3)
    def _():
        def fn(c0, y):
            v32_ref[:, c0:c0 + PROJ_PIECE] = y
            if maybe_vt_ref is not None:
                n_chunks, _, ck = maybe_vt_ref.shape
                for c in range(n_chunks):
                    for g in range(PROJ_PIECE // HEAD_DIM):
                        hs = slice(g * HEAD_DIM, (g + 1) * HEAD_DIM)
                        maybe_vt_ref[c, c0 + g * HEAD_DIM:c0 + (g + 1) * HEAD_DIM, :] = (
                            y[c * ck:(c + 1) * ck, hs].T.astype(maybe_vt_ref.dtype))
        col_pieces(fn)

    for t_qi in range(2):
        @pl.when(j == 4 + t_qi)
        def _():
            col_pieces(rope_to((qi_ref,), col_base=t_qi * PROJ_TN))


def mixproj(x, g, w_in16, w_tail, cos, sin, *, tm, vt_chunk=None):
    m, d = x.shape
    tn = PROJ_TN
    n_tiles = 6
    row = lambda width: pl.BlockSpec((tm, width), lambda i, j: (i, 0))
    rope_tiles = cos.shape[0] // tm
    rope_spec = pl.BlockSpec((tm, HEAD_DIM), lambda i, j: (i % rope_tiles, 0))
    widths = (tn, tn, tn, tn, tn, 2 * tn, IDX_DIM, IDX_DIM, LANES)
    dtypes = (F32, BF16, F32, BF16, F32, BF16, F32, BF16, F32)
    out_specs = [row(w) for w in widths]
    out_shape = [jax.ShapeDtypeStruct((m, w), dt) for w, dt in zip(widths, dtypes)]
    if vt_chunk is not None:
        out_specs.append(pl.BlockSpec((tm // vt_chunk, tn, vt_chunk), lambda i, j: (i, 0, 0)))
        out_shape.append(jax.ShapeDtypeStruct((m // vt_chunk, tn, vt_chunk), BF16))
    return pl.pallas_call(
        _mixproj_kernel,
        grid=(m // tm, n_tiles),
        in_specs=[
            row(d),
            pl.BlockSpec((1, d), lambda i, j: (0, 0)),
            pl.BlockSpec((d, tn), lambda i, j: (0, j)),
            pl.BlockSpec(w_tail.shape, lambda i, j: (0, 0)),
            rope_spec, rope_spec,
        ],
        out_specs=out_specs,
        out_shape=out_shape,
        scratch_shapes=[pltpu.VMEM((tm, d), BF16)],
        compiler_params=_cparams(2),
        name="mixproj",
    )(x, g.reshape(1, d), w_in16, w_tail, cos, sin)


def _pool_kernel(*refs, tm, pos0, has_prev):
    if has_prev:
        up_ref, prev_ref, prefix_ref, w_ref, s_ref, o_ref, ext_ref = refs
    else:
        up_ref, prefix_ref, w_ref, s_ref, o_ref, ext_ref = refs
    i = pl.program_id(1)
    if has_prev:
        halo = jnp.where(i == 0, prefix_ref[...], prev_ref[...])
    else:
        halo = prefix_ref[...]
    ext_ref[0:HALO, :] = halo
    ext_ref[HALO:HALO + tm, :] = up_ref[...]
    pos = pos0 + i * tm + lax.broadcasted_iota(jnp.int32, (tm, 1), 0)
    gw = up_ref.shape[1] // POOL_GROUPS
    for g, w in enumerate(POOL_WINDOWS):
        sl = slice(g * gw, (g + 1) * gw)
        cur = ext_ref[HALO:HALO + tm, sl]
        s = cur
        for j in range(1, w):
            s = s + ext_ref[HALO - j:HALO - j + tm, sl]
        cnt = jnp.minimum(w, pos + 1).astype(F32)
        d = s / cnt - cur
        y = jnp.dot(d.astype(BF16), w_ref[g], preferred_element_type=F32) * s_ref[:, sl]
        o_ref[:, sl] = y.astype(o_ref.dtype)


def pool_mixer(up, prefix16, w_pool, s_pool, *, tm, pos0):
    b, t, wd = up.shape
    has_prev = t > tm
    gw = wd // POOL_GROUPS
    in_specs = [pl.BlockSpec((None, tm, wd), lambda bi, i: (bi, i, 0))]
    args = [up]
    if has_prev:
        r = tm // HALO
        in_specs.append(pl.BlockSpec((None, HALO, wd), lambda bi, i: (bi, jnp.maximum(i * r - 1, 0), 0)))
        args.append(up)
    in_specs += [
        pl.BlockSpec((None, HALO, wd), lambda bi, i: (bi, 0, 0)),
        pl.BlockSpec((POOL_GROUPS, gw, gw), lambda bi, i: (0, 0, 0)),
        pl.BlockSpec((1, wd), lambda bi, i: (0, 0)),
    ]
    args += [prefix16, w_pool, s_pool.reshape(1, wd)]
    return pl.pallas_call(
        functools.partial(_pool_kernel, tm=tm, pos0=pos0, has_prev=has_prev),
        grid=(b, t // tm),
        in_specs=in_specs,
        out_specs=pl.BlockSpec((None, tm, wd), lambda bi, i: (bi, i, 0)),
        out_shape=jax.ShapeDtypeStruct((b, t, wd), BF16),
        scratch_shapes=[pltpu.VMEM((HALO + tm, wd), F32)],
        compiler_params=_cparams(2),
        name="pool_mixer",
    )(*args)


PEEL_BRACKET = 8
UNTESTED_HALVINGS = 9


def _select_threshold(count_ge, max_below, rmin, rmax, n_valid, kq):
    def any_row(pred):
        return jnp.max(jnp.where(pred, 1.0, 0.0))

    def halve(lo, hi, cl, ch):
        mid = lo + 0.5 * (hi - lo)
        stuck = (mid <= lo) | (mid >= hi)
        c = count_ge(mid)
        ge = c >= kq
        return (jnp.where(ge, mid, lo), jnp.where(ge, hi, mid), jnp.where(ge, c, cl), jnp.where(ge, ch, c),
                jnp.where(stuck, 1.0, 0.0))

    def halving_loop(st, bracket_cap):
        def open_rows(cl, ch, stuck):
            return any_row((cl != kq) & (stuck <= 0.0) & (cl - ch > bracket_cap))

        def cond(st):
            return jnp.logical_and(st[0] < MAX_SEARCH_STEPS, st[1] > 0.0)

        def body(st):
            steps, _, lo, hi, cl, ch, _ = st
            lo, hi, cl, ch, stuck = halve(lo, hi, cl, ch)
            return steps + 1, open_rows(cl, ch, stuck), lo, hi, cl, ch, stuck

        st = (st[0], open_rows(st[4], st[5], st[6])) + tuple(st[2:])
        return lax.while_loop(cond, body, st)

    hi0 = rmax + jnp.maximum(jnp.abs(rmax) * 1e-6, 1e-30)
    zero = jnp.zeros_like(rmax)
    st = lax.fori_loop(0, UNTESTED_HALVINGS, lambda _, st: halve(*st[:4]), (rmin, hi0, n_valid, zero, zero))
    st = halving_loop((jnp.int32(0), jnp.float32(0.0)) + tuple(st), float(PEEL_BRACKET))
    _, _, lo, hi, cl, ch, stuck = st

    walk = (cl != kq) & (stuck <= 0.0)

    def walk_cond(st):
        return jnp.logical_and(st[0] < PEEL_BRACKET, st[1] > 0.0)

    def walk_body(st):
        steps, _, top, n_top = st
        active = walk & (n_top < kq)
        top = jnp.where(active, max_below(top), top)
        n_top = jnp.where(active, n_top + 1.0, n_top)
        return steps + 1, any_row(walk & (n_top < kq)), top, n_top

    _, _, top, _ = lax.while_loop(walk_cond, walk_body, (jnp.int32(0), any_row(walk), hi, ch))
    thr = jnp.where(walk, top, lo)
    n_ge = count_ge(thr)

    redo = n_ge != kq
    st = halving_loop((jnp.int32(0), jnp.float32(0.0), lo, hi, cl, ch, jnp.where(redo, stuck, 1.0)), 0.0)
    return jnp.where(redo, st[2], thr), jnp.where(redo, st[4], n_ge)


def _last_tied_index(count_tied_upto, need, n_keys):
    def body(_, st):
        lo, hi = st
        mid = jnp.floor((lo + hi) * 0.5)
        ok = count_tied_upto(mid) >= need
        return jnp.where(ok, lo, mid), jnp.where(ok, mid, hi)

    steps = max(1, math.ceil(math.log2(n_keys + 1)))
    _, hi = lax.fori_loop(0, steps, body, (jnp.full(need.shape, -1.0, F32), jnp.full(need.shape, n_keys - 1.0, F32)))
    return hi


def _attn_prompt_kernel(q_ref, qi_ref, wit_ref, k_ref, vt_ref, ki_ref, o_ref,
                        sc_ref, ext_ref, m_ref, l_ref, acc_ref, *, tk, topk, n_heads):
    i = pl.program_id(1)
    tq = q_ref.shape[0]
    nc = ((i + 1) * tq + tk - 1) // tk
    qpos = i * tq + lax.broadcasted_iota(jnp.int32, (1, tq), 1)
    key_row = lax.broadcasted_iota(jnp.int32, (tk, tq), 0)
    w_rows = wit_ref[...] * (IDX_DIM ** -0.5)

    def idx_chunk(c, carry):
        off = pl.multiple_of(c * tk, tk)
        kic = ki_ref[pl.ds(off, tk), :]
        acc = jnp.zeros((tk, tq), F32)
        for h in range(IDX_HEADS):
            s = lax.dot_general(kic, qi_ref[:, h * IDX_DIM:(h + 1) * IDX_DIM], NT_DIMS,
                                preferred_element_type=F32)
            acc = acc + jnp.maximum(s, 0.0) * w_rows[h:h + 1, :]
        causal = off + key_row <= qpos
        sc = jnp.where(causal, acc, -jnp.inf)
        sc_ref[pl.ds(off, tk), :] = sc
        ext_ref[0] = jnp.minimum(ext_ref[0], jnp.min(jnp.where(causal, acc, jnp.inf).reshape(tk // 8, 8, tq), axis=0))
        ext_ref[1] = jnp.maximum(ext_ref[1], jnp.max(sc.reshape(tk // 8, 8, tq), axis=0))
        return carry

    ext_ref[0] = jnp.full((8, tq), jnp.inf, F32)
    ext_ref[1] = jnp.full((8, tq), -jnp.inf, F32)

    def chunk_loop(n_chunks, chunk_fn):
        base = tq // tk
        per_trip = CHUNK_TRIP_FACTOR * base

        def run(first, count):
            for u in range(count):
                chunk_fn(first + u, 0)

        def body(t, carry):
            run(t * per_trip, per_trip)
            return carry
        lax.fori_loop(0, n_chunks // per_trip, body, 0)
        for r in range(1, CHUNK_TRIP_FACTOR):
            @pl.when(n_chunks % per_trip == r * base)
            def _():
                run((n_chunks // per_trip) * per_trip, r * base)

    chunk_loop(nc, idx_chunk)

    lanes_par = 4

    def reduce_keys(fn, init, red):
        rows = tq

        def body(c, a):
            blk = sc_ref[pl.ds(pl.multiple_of(c * rows, rows), rows), :]
            return fn(a, blk.reshape(lanes_par, rows // (8 * lanes_par), 8, tq))
        a = lax.fori_loop(0, (nc * tk) // rows, body, jnp.full((lanes_par, 8, tq), init, F32))
        return red(red(a, axis=0), axis=0, keepdims=True)

    def count_ge(t):
        return reduce_keys(lambda a, blk: a + jnp.sum(jnp.where(blk >= t, 1.0, 0.0), axis=1), 0.0, jnp.sum)

    def max_below(t):
        return reduce_keys(lambda a, blk: jnp.maximum(a, jnp.max(jnp.where(blk < t, blk, -jnp.inf), axis=1)),
                           -jnp.inf, jnp.max)

    rmin = jnp.min(ext_ref[0], axis=0, keepdims=True)
    rmax = jnp.max(ext_ref[1], axis=0, keepdims=True)
    n_valid = (qpos + 1).astype(F32)
    kq = jnp.minimum(float(topk), n_valid)
    thr, n_ge = _select_threshold(count_ge, max_below, rmin, rmax, n_valid, kq)

    @pl.when(jnp.max(n_ge - kq) > 0.0)
    def _():
        def key_sum(fn):
            def body(c, a):
                off = pl.multiple_of(c * tk, tk)
                hit = fn(sc_ref[pl.ds(off, tk), :], (off + key_row).astype(F32))
                return a + jnp.sum(jnp.where(hit, 1.0, 0.0), axis=0, keepdims=True)
            return lax.fori_loop(0, nc, body, jnp.zeros((1, tq), F32))

        need = kq - key_sum(lambda blk, key: blk > thr)
        last = _last_tied_index(lambda j: key_sum(lambda blk, key: (blk == thr) & (key <= j)), need,
                                sc_ref.shape[0])

        def drop(c, carry):
            off = pl.multiple_of(c * tk, tk)
            blk = sc_ref[pl.ds(off, tk), :]
            beyond = (blk == thr) & ((off + key_row).astype(F32) > last)
            sc_ref[pl.ds(off, tk), :] = jnp.where(beyond, -jnp.inf, blk)
            return carry
        lax.fori_loop(0, nc, drop, 0)

    m_ref[...] = jnp.full(m_ref.shape, -jnp.inf, F32)
    l_ref[...] = jnp.zeros(l_ref.shape, F32)
    acc_ref[...] = jnp.zeros(acc_ref.shape, F32)

    def att_chunk(c, carry):
        off = pl.multiple_of(c * tk, tk)
        mask = sc_ref[pl.ds(off, tk), :] >= thr
        for h in range(n_heads):
            hs = slice(h * HEAD_DIM, (h + 1) * HEAD_DIM)
            s = lax.dot_general(k_ref[pl.ds(off, tk), hs], q_ref[:, hs], NT_DIMS,
                                preferred_element_type=F32)
            s = jnp.where(mask, s, NEG)
            m_prev = m_ref[h]
            m_new = jnp.maximum(m_prev, jnp.max(s, axis=0, keepdims=True))
            alpha = jnp.exp2(m_prev - m_new)
            p = jnp.exp2(s - m_new)
            l_ref[h] = alpha * l_ref[h] + jnp.sum(p, axis=0, keepdims=True)
            m_ref[h] = m_new
            pv = jnp.dot(vt_ref[c, hs, :], p.astype(BF16), preferred_element_type=F32)
            acc_ref[h] = alpha * acc_ref[h] + pv
        return carry

    chunk_loop(nc, att_chunk)

    for h in range(n_heads):
        hs = slice(h * HEAD_DIM, (h + 1) * HEAD_DIM)
        o_ref[:, hs] = (acc_ref[h] / l_ref[h]).T.astype(o_ref.dtype)


def attn_prompt(q, qi, wit, k, vt, ki, *, topk):
    b, s, aw = q.shape
    n_heads = aw // HEAD_DIM
    tq = ATTN_TQ
    tk = vt.shape[3]
    kern = functools.partial(_attn_prompt_kernel, tk=ATTN_TK, topk=topk, n_heads=n_heads)
    return pl.pallas_call(
        kern,
        grid=(b, s // tq),
        in_specs=[
            pl.BlockSpec((None, tq, aw), lambda bi, i: (bi, i, 0)),
            pl.BlockSpec((None, tq, qi.shape[2]), lambda bi, i: (bi, i, 0)),
            pl.BlockSpec((None, IDX_HEADS, tq), lambda bi, i: (bi, 0, i)),
            pl.BlockSpec((None, s, aw), lambda bi, i: (bi, 0, 0)),
            pl.BlockSpec((None, s // tk, aw, tk), lambda bi, i: (bi, 0, 0, 0)),
            pl.BlockSpec((None, s, IDX_DIM), lambda bi, i: (bi, 0, 0)),
        ],
        out_specs=pl.BlockSpec((None, tq, aw), lambda bi, i: (bi, i, 0)),
        out_shape=jax.ShapeDtypeStruct((b, s, aw), BF16),
        scratch_shapes=[
            pltpu.VMEM((s, tq), F32),
            pltpu.VMEM((2, 8, tq), F32),
            pltpu.VMEM((n_heads, 1, tq), F32),
            pltpu.VMEM((n_heads, 1, tq), F32),
            pltpu.VMEM((n_heads, HEAD_DIM, tq), F32),
        ],
        compiler_params=_cparams(2),
        name="attn_prompt",
    )(q, qi, wit, k, vt, ki)


def _idx_sample_kernel(pt_ref, qi_ref, w_ref, *rest, n_pages, group, t_new, topk):
    cik_refs = rest[:group]
    kin_ref, sc_ref, thr_ref = rest[group:]
    b, p = pl.program_id(0), pl.program_id(1)
    db = sc_ref.shape[0]
    rows = qi_ref.shape[0]

    def scores(keys_bf16):
        s = lax.dot_general(qi_ref[...], keys_bf16, NT_DIMS, preferred_element_type=F32)
        r = jnp.maximum(s, 0.0) * w_ref[...]
        return jnp.sum(r.reshape(rows // t_new, t_new, LANES), axis=0)

    keys = jnp.concatenate([r[...].astype(BF16) for r in cik_refs], axis=0)
    s_all = lax.dot_general(qi_ref[...], keys, NT_DIMS, preferred_element_type=F32)
    for g in range(group):
        r = jnp.maximum(s_all[:, g * LANES:(g + 1) * LANES], 0.0) * w_ref[...]
        sc_ref[b, p * group + g] = jnp.sum(r.reshape(rows // t_new, t_new, LANES), axis=0)

    @pl.when(p == 0)
    def _():
        s = scores(kin_ref[...])
        tok = lax.broadcasted_iota(jnp.int32, (t_new, LANES), 0)
        lane = lax.broadcasted_iota(jnp.int32, (t_new, LANES), 1)
        sc_ref[b, n_pages] = jnp.where((lane <= tok) & (lane < t_new), s, -jnp.inf)

    @pl.when((b == db - 1) & (p == pl.num_programs(1) - 1))
    def _():
        def reduce_keys(fn, red):
            return red(red(fn(sc_ref[...]), axis=1), axis=2, keepdims=True)

        def count_ge(t):
            return reduce_keys(lambda x: jnp.where(x >= t[:, None], 1.0, 0.0), jnp.sum)

        def max_below(t):
            return reduce_keys(lambda x: jnp.where(x < t[:, None], x, -jnp.inf), jnp.max)

        rmax = reduce_keys(lambda x: x, jnp.max)
        rmin = reduce_keys(lambda x: jnp.where(x == -jnp.inf, jnp.inf, x), jnp.min)
        tokc = lax.broadcasted_iota(jnp.int32, (db, t_new, 1), 1)
        n_valid = (n_pages * LANES + tokc + 1).astype(F32)
        kq = jnp.minimum(float(topk), n_valid)
        thr, n_ge = _select_threshold(count_ge, max_below, rmin, rmax, n_valid, kq)
        thr_ref[...] = jnp.broadcast_to(thr, thr_ref.shape)

        @pl.when(jnp.max(n_ge - kq) > 0.0)
        def _():
            n_slabs = n_pages + 1
            key = (lax.broadcasted_iota(jnp.int32, (1, n_slabs, 1, LANES), 1) * LANES
                   + lax.broadcasted_iota(jnp.int32, (1, n_slabs, 1, LANES), 3)).astype(F32)
            thr4 = thr[:, None]

            def key_sum(hit):
                return jnp.sum(jnp.sum(jnp.where(hit, 1.0, 0.0), axis=1), axis=2, keepdims=True)

            need = kq - key_sum(sc_ref[...] > thr4)
            last = _last_tied_index(lambda j: key_sum((sc_ref[...] == thr4) & (key <= j[:, None])), need,
                                    n_slabs * LANES)
            sc = sc_ref[...]
            sc_ref[...] = jnp.where((sc == thr4) & (key > last[:, None]), -jnp.inf, sc)


def _page_map(b, p, pt, *, g, group):
    return (0, pt[b, p * group + g], 0, 0)


def idx_sample(page_table, qi_ht, w_ht, cache_idx_k, ki_new_pad, *, t_new, topk):
    db, n_pages = page_table.shape
    group = math.gcd(n_pages, IDX_PAGES_PER_STEP)
    rows = qi_ht.shape[1]
    page = cache_idx_k.shape[2]
    assert page == LANES
    kern = functools.partial(_idx_sample_kernel, n_pages=n_pages, group=group, t_new=t_new, topk=topk)
    page_specs = [pl.BlockSpec((None, None, page, IDX_DIM), functools.partial(_page_map, g=g, group=group))
                  for g in range(group)]
    grid_spec = pltpu.PrefetchScalarGridSpec(
        num_scalar_prefetch=1,
        grid=(db, n_pages // group),
        in_specs=[
            pl.BlockSpec((None, rows, IDX_DIM), lambda b, p, pt: (b, 0, 0)),
            pl.BlockSpec((None, rows, LANES), lambda b, p, pt: (b, 0, 0)),
            *page_specs,
            pl.BlockSpec((None, page, IDX_DIM), lambda b, p, pt: (b, 0, 0)),
        ],
        out_specs=[
            pl.BlockSpec((db, n_pages + 1, t_new, LANES), lambda b, p, pt: (0, 0, 0, 0)),
            pl.BlockSpec((db, t_new, LANES), lambda b, p, pt: (0, 0, 0)),
        ],
    )
    return pl.pallas_call(
        kern,
        grid_spec=grid_spec,
        out_shape=[jax.ShapeDtypeStruct((db, n_pages + 1, t_new, LANES), F32),
                   jax.ShapeDtypeStruct((db, t_new, LANES), F32)],
        compiler_params=_cparams(2),
        name="idx_sample",
    )(page_table, qi_ht, w_ht, *([cache_idx_k] * group), ki_new_pad)


def _attn_sample_kernel(pt_ref, q_ref, *rest, group, page, n_heads):
    ck_refs = rest[:group]
    cv_refs = rest[group:2 * group]
    kn_ref, vn_ref, sc_ref, scn_ref, thr_ref, o_ref, m_ref, l_ref, acc_ref = rest[2 * group:]
    p = pl.program_id(1)

    @pl.when(p == 0)
    def _():
        m_ref[...] = jnp.full(m_ref.shape, -jnp.inf, F32)
        l_ref[...] = jnp.zeros(l_ref.shape, F32)
        acc_ref[...] = jnp.zeros(acc_ref.shape, F32)

    thr = thr_ref[...]

    def head_rows(ref, h):
        return ref[pl.ds(h, page, stride=n_heads), :].astype(BF16)

    def attend(k_refs, v_refs, sc, sc_thr):
        n = len(k_refs)
        s = jnp.stack([
            lax.dot_general(q_ref[h], jnp.concatenate([head_rows(r, h) for r in k_refs], axis=0), NT_DIMS,
                            preferred_element_type=F32)
            for h in range(n_heads)])
        s = jnp.where(sc[None] >= sc_thr[None], s, NEG)
        m_prev = m_ref[...]
        m_new = jnp.maximum(m_prev, jnp.max(s, axis=2, keepdims=True))
        alpha = jnp.exp2(m_prev - m_new)
        pr = jnp.exp2(s - jnp.concatenate([m_new] * n, axis=2))
        l_ref[...] = alpha * l_ref[...] + jnp.sum(pr, axis=2, keepdims=True)
        m_ref[...] = m_new
        pb = pr.astype(BF16)
        for h in range(n_heads):
            vh = jnp.concatenate([head_rows(r, h) for r in v_refs], axis=0)
            acc_ref[h] = alpha[h] * acc_ref[h] + jnp.dot(pb[h], vh, preferred_element_type=F32)

    attend(ck_refs, cv_refs, jnp.concatenate([sc_ref[g] for g in range(group)], axis=1),
           jnp.concatenate([thr] * group, axis=1))

    @pl.when(p == pl.num_programs(1) - 1)
    def _():
        attend([kn_ref], [vn_ref], scn_ref[...], thr)
        o_ref[...] = acc_ref[...] / l_ref[...]


def attn_sample(page_table, q_ht, cache_k, cache_v, k_new_pad, v_new_pad, sc, thr):
    db, n_pages = page_table.shape
    n_heads, t_new = q_ht.shape[1], q_ht.shape[2]
    prow = cache_k.shape[2]
    page = prow // n_heads
    group = math.gcd(n_pages, KV_PAGES_PER_STEP)
    kern = functools.partial(_attn_sample_kernel, group=group, page=page, n_heads=n_heads)
    cache_specs = [pl.BlockSpec((None, None, prow, HEAD_DIM), functools.partial(_page_map, g=g, group=group))
                   for g in range(group)]
    new_spec = pl.BlockSpec((None, prow, HEAD_DIM), lambda b, p, pt: (b, 0, 0))
    head_spec = pl.BlockSpec((None, n_heads, t_new, HEAD_DIM), lambda b, p, pt: (b, 0, 0, 0))
    grid_spec = pltpu.PrefetchScalarGridSpec(
        num_scalar_prefetch=1,
        grid=(db, n_pages // group),
        in_specs=[
            head_spec,
            *cache_specs, *cache_specs, new_spec, new_spec,
            pl.BlockSpec((None, group, t_new, LANES), lambda b, p, pt: (b, p, 0, 0)),
            pl.BlockSpec((None, None, t_new, LANES), lambda b, p, pt: (b, n_pages, 0, 0)),
            pl.BlockSpec((None, t_new, LANES), lambda b, p, pt: (b, 0, 0)),
        ],
        out_specs=head_spec,
        scratch_shapes=[
            pltpu.VMEM((n_heads, t_new, LANES), F32),
            pltpu.VMEM((n_heads, t_new, LANES), F32),
            pltpu.VMEM((n_heads, t_new, HEAD_DIM), F32),
        ],
    )
    return pl.pallas_call(
        kern,
        grid_spec=grid_spec,
        out_shape=jax.ShapeDtypeStruct((db, n_heads, t_new, HEAD_DIM), F32),
        compiler_params=_cparams(2),
        name="attn_sample",
    )(page_table, q_ht, *([cache_k] * group), *([cache_v] * group), k_new_pad, v_new_pad, sc, sc, thr)


def _outproj_kernel(x_ref, yp_ref, ya_ref, wt_ref, wb_ref, g_ref, x2_ref, h2_ref):
    y = jnp.dot(yp_ref[...], wt_ref[...], preferred_element_type=F32)
    y = y + jnp.dot(ya_ref[...], wb_ref[...], preferred_element_type=F32)
    x2 = x_ref[...] + y
    x2_ref[...] = x2
    h2_ref[...] = _rms(x2, g_ref[...]).astype(h2_ref.dtype)


def outproj(x, yp, ya, w_out, g, *, tm):
    m, d = x.shape
    kp, ka = yp.shape[1], ya.shape[1]
    assert kp == ka and w_out.shape == (kp + ka, d)
    return pl.pallas_call(
        _outproj_kernel,
        grid=(m // tm,),
        in_specs=[
            pl.BlockSpec((tm, d), lambda i: (i, 0)),
            pl.BlockSpec((tm, kp), lambda i: (i, 0)),
            pl.BlockSpec((tm, ka), lambda i: (i, 0)),
            pl.BlockSpec((kp, d), lambda i: (0, 0)),
            pl.BlockSpec((ka, d), lambda i: (1, 0)),
            pl.BlockSpec((1, d), lambda i: (0, 0)),
        ],
        out_specs=[pl.BlockSpec((tm, d), lambda i: (i, 0)), pl.BlockSpec((tm, d), lambda i: (i, 0))],
        out_shape=[jax.ShapeDtypeStruct((m, d), F32), jax.ShapeDtypeStruct((m, d), BF16)],
        compiler_params=_cparams(1),
        name="outproj",
    )(x, yp, ya, w_out, w_out, g.reshape(1, d))


def _ffn_kernel(h2_ref, wg_ref, wu_ref, wd_ref, x2_ref, g_ref, o_ref, acc_ref):
    f = pl.program_id(1)

    @pl.when(f == 0)
    def _():
        acc_ref[...] = jnp.zeros(acc_ref.shape, F32)

    h2 = h2_ref[...]
    gate = jnp.dot(h2, wg_ref[...], preferred_element_type=F32)
    up = jnp.dot(h2, wu_ref[...], preferred_element_type=F32)
    a = (gate * jax.nn.sigmoid(gate) * up).astype(BF16)
    acc_ref[...] += jnp.dot(a, wd_ref[...], preferred_element_type=F32)

    @pl.when(f == pl.num_programs(1) - 1)
    def _():
        o_ref[...] = _rms(x2_ref[...] + acc_ref[...], g_ref[...])


def ffn(h2, wg, wu, wd, x2, g, *, tm, tf):
    m, d = h2.shape
    dff = wg.shape[1]
    return pl.pallas_call(
        _ffn_kernel,
        grid=(m // tm, dff // tf),
        in_specs=[
            pl.BlockSpec((tm, d), lambda i, f: (i, 0)),
            pl.BlockSpec((d, tf), lambda i, f: (0, f)),
            pl.BlockSpec((d, tf), lambda i, f: (0, f)),
            pl.BlockSpec((tf, d), lambda i, f: (f, 0)),
            pl.BlockSpec((tm, d), lambda i, f: (i, 0)),
            pl.BlockSpec((1, d), lambda i, f: (0, 0)),
        ],
        out_specs=pl.BlockSpec((tm, d), lambda i, f: (i, 0)),
        out_shape=jax.ShapeDtypeStruct((m, d), F32),
        scratch_shapes=[pltpu.VMEM((tm, d), F32)],
        compiler_params=_cparams(2),
        name="ffn",
    )(h2, wg, wu, wd, x2, g.reshape(1, d))


def _rope_tables(pos):
    half = HEAD_DIM // 2
    inv = ROPE_THETA ** (-jnp.arange(half, dtype=F32) / half)
    ang = pos.astype(F32)[:, None] * inv[None, :]
    cos, sin = jnp.cos(ang), jnp.sin(ang)
    return jnp.concatenate([cos, cos], axis=-1), jnp.concatenate([-sin, sin], axis=-1)


def _mix_inputs(x2d, pos_rows, g_mix, w_in16, w_tail, *, tm, vt_chunk=None):
    cos, sin = _rope_tables(pos_rows)
    return mixproj(x2d, g_mix, w_in16, w_tail, cos, sin, tm=tm, vt_chunk=vt_chunk)


def kernel(x_prompt, x_sample, cache_k, cache_v, cache_idx_k, state_pool, page_table, g_mix, w_in, w_pool,
           s_pool, w_out, g_ffn, w_gate, w_up, w_down, g_final):
    B, S, D = x_prompt.shape
    DB, T, _ = x_sample.shape
    depth = w_in.shape[0]
    assert depth == 1
    n_pages = page_table.shape[1]
    page = cache_k.shape[2]
    n_heads = cache_k.shape[3]
    aw = n_heads * HEAD_DIM
    pw = w_pool.shape[1] * w_pool.shape[2]
    past = n_pages * page
    l = 0

    wide = pw + 3 * aw + IDX_HEADS * IDX_DIM
    assert (pw, aw, wide) == (PROJ_TN, PROJ_TN, 6 * PROJ_TN) and w_in.shape[2] == wide + IDX_DIM + IDX_HEADS
    w_in16 = w_in[l].astype(BF16)
    w_tail = jnp.pad(w_in16[:, wide:], ((0, 0), (0, LANES - IDX_HEADS)))
    w_pool16 = w_pool[l].astype(BF16)
    w_out16 = w_out[l].astype(BF16)
    wg16, wu16, wd16 = w_gate[l].astype(BF16), w_up[l].astype(BF16), w_down[l].astype(BF16)

    xp2 = x_prompt.reshape(B * S, D)
    pos_p = jnp.arange(S)
    tk = ATTN_TK
    up, q, k32, k16, v32, qi, ki32, ki16, wi, vt = _mix_inputs(xp2, pos_p, g_mix[l], w_in16, w_tail, tm=ROW_TILE,
                                                               vt_chunk=tk)
    up3 = up.reshape(B, S, pw)
    yp = pool_mixer(up3, jnp.zeros((B, HALO, pw), F32), w_pool16, s_pool[l], tm=ROW_TILE, pos0=0)
    wit = wi.reshape(B, S, LANES)[:, :, :IDX_HEADS].transpose(0, 2, 1)
    ya = attn_prompt(q.reshape(B, S, aw), qi.reshape(B, S, -1), wit, k16.reshape(B, S, aw),
                     vt.reshape(B, S // tk, aw, tk), ki16.reshape(B, S, IDX_DIM), topk=min(TOPK_MAX, S // 4))
    x2, h2 = outproj(xp2, yp.reshape(B * S, pw), ya.reshape(B * S, aw), w_out16, g_ffn[l], tm=ROW_TILE)
    y_prompt = ffn(h2, wg16, wu16, wd16, x2, g_final, tm=ROW_TILE, tf=FFN_TILE).reshape(B, S, D)

    k_prompt = k32.reshape(1, B, S, n_heads, HEAD_DIM)
    v_prompt = v32.reshape(1, B, S, n_heads, HEAD_DIM)
    idx_k_prompt = ki32.reshape(1, B, S, IDX_DIM)
    pool_prompt = up3[:, S - POOL_STATE:, :][None]

    M = DB * T
    xs2 = x_sample.reshape(M, D)
    pos_s = jnp.tile(past + jnp.arange(T), DB)
    ups, qs, ks32, _, vs32, qis, kis32, kis16, wis = _mix_inputs(xs2, pos_s, g_mix[l], w_in16, w_tail, tm=M)
    ups3 = ups.reshape(DB, T, pw)
    prefix16 = jnp.pad(state_pool[l], ((0, 0), (HALO - POOL_STATE, 0), (0, 0)))
    yps = pool_mixer(ups3, prefix16, w_pool16, s_pool[l], tm=T, pos0=past)

    qi_ht = qis.reshape(DB, T, IDX_HEADS, IDX_DIM).transpose(0, 2, 1, 3).reshape(DB, IDX_HEADS * T, IDX_DIM)
    w_ht = wis.reshape(DB, T, LANES)[:, :, :IDX_HEADS].transpose(0, 2, 1).reshape(DB, IDX_HEADS * T, 1)
    w_ht = jnp.broadcast_to(w_ht * (IDX_DIM ** -0.5), (DB, IDX_HEADS * T, LANES))
    pad_rows = lambda a: jnp.pad(a.reshape(DB, T, -1), ((0, 0), (0, page - T), (0, 0)))
    sc, thr = idx_sample(page_table, qi_ht, w_ht, cache_idx_k, pad_rows(kis16), t_new=T,
                         topk=min(TOPK_MAX, (past + T) // 4))

    n_phys = cache_k.shape[1]
    key_head_rows = lambda a: pad_rows(a).reshape(DB, page * n_heads, HEAD_DIM)
    q_ht = qs.reshape(DB, T, n_heads, HEAD_DIM).transpose(0, 2, 1, 3)
    o_ht = attn_sample(page_table, q_ht, cache_k.reshape(depth, n_phys, page * n_heads, HEAD_DIM),
                       cache_v.reshape(depth, n_phys, page * n_heads, HEAD_DIM),
                       key_head_rows(ks32), key_head_rows(vs32), sc, thr)
    yas = o_ht.transpose(0, 2, 1, 3).reshape(M, aw).astype(BF16)

    x2s, h2s = outproj(xs2, yps.reshape(M, pw), yas, w_out16, g_ffn[l], tm=M)
    y_sample = ffn(h2s, wg16, wu16, wd16, x2s, g_final, tm=M, tf=FFN_TILE).reshape(DB, T, D)

    k_sample = ks32.reshape(1, DB, T, n_heads, HEAD_DIM)
    v_sample = vs32.reshape(1, DB, T, n_heads, HEAD_DIM)
    idx_k_sample = kis32.reshape(1, DB, T, IDX_DIM)
    pool_sample = jnp.concatenate([state_pool[l][:, T:, :], ups3], axis=1)[None]

    return (y_prompt, y_sample, k_prompt, v_prompt, idx_k_prompt, pool_prompt,
            k_sample, v_sample, idx_k_sample, pool_sample)
```

```python
import functools
import math

import jax
import jax.numpy as jnp
from jax import lax
from jax.experimental import pallas as pl
from jax.experimental.pallas import tpu as pltpu

F32 = jnp.float32
BF16 = jnp.bfloat16

LANES = 128
HEAD_DIM = 128
IDX_DIM = 128
IDX_HEADS = 16
POOL_GROUPS = 4
POOL_WINDOWS = (2, 4, 8, 16)
POOL_STATE = max(POOL_WINDOWS) - 1
HALO = 16
TOPK_MAX = 256
ROW_TILE = 512
FFN_TILE = 512
ATTN_TQ = 256
ATTN_TK = 128
CHUNK_TRIP_FACTOR = 4
IDX_PAGES_PER_STEP = 32
KV_PAGES_PER_STEP = 16
ROPE_THETA = 10000.0
EPS = 1e-6
NEG = -1e30
MAX_SEARCH_STEPS = 1024
LOG2E = 1.4426950408889634
VMEM_LIMIT = 56 * 1024 * 1024

NT_DIMS = (((1,), (1,)), ((), ()))


def _cparams(n_grid):
    return pltpu.CompilerParams(dimension_semantics=("arbitrary",) * n_grid, vmem_limit_bytes=VMEM_LIMIT)


def _rms(x, g):
    return x * lax.rsqrt(jnp.mean(x * x, axis=-1, keepdims=True) + EPS) * g


PROJ_TN = 1024
PROJ_PIECE = 256


def _rope(y, cos, sin):
    return [y[:, g * HEAD_DIM:(g + 1) * HEAD_DIM] * cos
            + pltpu.roll(y[:, g * HEAD_DIM:(g + 1) * HEAD_DIM], HEAD_DIM // 2, axis=1) * sin
            for g in range(y.shape[1] // HEAD_DIM)]


def _mixproj_kernel(x_ref, g_ref, w_ref, wt_ref, cos_ref, sin_ref,
                    up_ref, q_ref, k32_ref, k16_ref, v32_ref, qi_ref, ki32_ref, ki16_ref, wi_ref, *rest):
    maybe_vt_ref, h_ref = rest if len(rest) == 2 else (None, rest[0])
    j = pl.program_id(1)

    def heads_to(pieces, refs, col0=0):
        for g, r in enumerate(pieces):
            for o_ref in refs:
                o_ref[:, col0 + g * HEAD_DIM:col0 + (g + 1) * HEAD_DIM] = r.astype(o_ref.dtype)

    @pl.when(j == 0)
    def _():
        h_ref[...] = _rms(x_ref[...], g_ref[...]).astype(h_ref.dtype)
        t = jnp.dot(h_ref[...], wt_ref[...], preferred_element_type=F32)
        heads_to(_rope(t[:, :IDX_DIM], cos_ref[...], sin_ref[...]), (ki32_ref, ki16_ref))
        wi_ref[...] = t[:, IDX_DIM:] * (IDX_HEADS ** -0.5)

    def col_pieces(fn):
        for c0 in range(0, PROJ_TN, PROJ_PIECE):
            fn(c0, jnp.dot(h_ref[...], w_ref[:, c0:c0 + PROJ_PIECE], preferred_element_type=F32))

    def rope_to(refs, col_base=0, scale=None):
        def fn(c0, y):
            y = y if scale is None else y * scale
            heads_to(_rope(y, cos_ref[...], sin_ref[...]), refs, col0=col_base + c0)
        return fn

    @pl.when(j == 0)
    def _():
        def fn(c0, y):
            up_ref[:, c0:c0 + PROJ_PIECE] = y
        col_pieces(fn)

    @pl.when(j == 1)
    def _():
        col_pieces(rope_to((q_ref,), scale=HEAD_DIM ** -0.5 * LOG2E))

    @pl.when(j == 2)
    def _():
        col_pieces(rope_to((k32_ref, k16_ref)))

    @pl.when(j == 3)
    def _():
        def fn(c0, y):
            v32_ref[:, c0:c0 + PROJ_PIECE] = y
            if maybe_vt_ref is not None:
                n_chunks, _, ck = maybe_vt_ref.shape
                for c in range(n_chunks):
                    for g in range(PROJ_PIECE // HEAD_DIM):
                        hs = slice(g * HEAD_DIM, (g + 1) * HEAD_DIM)
                        maybe_vt_ref[c, c0 + g * HEAD_DIM:c0 + (g + 1) * HEAD_DIM, :] = (
                            y[c * ck:(c + 1) * ck, hs].T.astype(maybe_vt_ref.dtype))
        col_pieces(fn)

    for t_qi in range(2):
        @pl.when(j == 4 + t_qi)
        def _():
            col_pieces(rope_to((qi_ref,), col_base=t_qi * PROJ_TN))


def mixproj(x, g, w_in16, w_tail, cos, sin, *, tm, vt_chunk=None):
    m, d = x.shape
    tn = PROJ_TN
    n_tiles = 6
    row = lambda width: pl.BlockSpec((tm, width), lambda i, j: (i, 0))
    rope_tiles = cos.shape[0] // tm
    rope_spec = pl.BlockSpec((tm, HEAD_DIM), lambda i, j: (i % rope_tiles, 0))
    widths = (tn, tn, tn, tn, tn, 2 * tn, IDX_DIM, IDX_DIM, LANES)
    dtypes = (F32, BF16, F32, BF16, F32, BF16, F32, BF16, F32)
    out_specs = [row(w) for w in widths]
    out_shape = [jax.ShapeDtypeStruct((m, w), dt) for w, dt in zip(widths, dtypes)]
    if vt_chunk is not None:
        out_specs.append(pl.BlockSpec((tm // vt_chunk, tn, vt_chunk), lambda i, j: (i, 0, 0)))
        out_shape.append(jax.ShapeDtypeStruct((m // vt_chunk, tn, vt_chunk), BF16))
    return pl.pallas_call(
        _mixproj_kernel,
        grid=(m // tm, n_tiles),
        in_specs=[
            row(d),
            pl.BlockSpec((1, d), lambda i, j: (0, 0)),
            pl.BlockSpec((d, tn), lambda i, j: (0, j)),
            pl.BlockSpec(w_tail.shape, lambda i, j: (0, 0)),
            rope_spec, rope_spec,
        ],
        out_specs=out_specs,
        out_shape=out_shape,
        scratch_shapes=[pltpu.VMEM((tm, d), BF16)],
        compiler_params=_cparams(2),
        name="mixproj",
    )(x, g.reshape(1, d), w_in16, w_tail, cos, sin)


def _pool_kernel(*refs, tm, pos0, has_prev):
    if has_prev:
        up_ref, prev_ref, prefix_ref, w_ref, s_ref, o_ref, ext_ref = refs
    else:
        up_ref, prefix_ref, w_ref, s_ref, o_ref, ext_ref = refs
    i = pl.program_id(1)
    if has_prev:
        halo = jnp.where(i == 0, prefix_ref[...], prev_ref[...])
    else:
        halo = prefix_ref[...]
    ext_ref[0:HALO, :] = halo
    ext_ref[HALO:HALO + tm, :] = up_ref[...]
    pos = pos0 + i * tm + lax.broadcasted_iota(jnp.int32, (tm, 1), 0)
    gw = up_ref.shape[1] // POOL_GROUPS
    for g, w in enumerate(POOL_WINDOWS):
        sl = slice(g * gw, (g + 1) * gw)
        cur = ext_ref[HALO:HALO + tm, sl]
        s = cur
        for j in range(1, w):
            s = s + ext_ref[HALO - j:HALO - j + tm, sl]
        cnt = jnp.minimum(w, pos + 1).astype(F32)
        d = s / cnt - cur
        y = jnp.dot(d.astype(BF16), w_ref[g], preferred_element_type=F32) * s_ref[:, sl]
        o_ref[:, sl] = y.astype(o_ref.dtype)


def pool_mixer(up, prefix16, w_pool, s_pool, *, tm, pos0):
    b, t, wd = up.shape
    has_prev = t > tm
    gw = wd // POOL_GROUPS
    in_specs = [pl.BlockSpec((None, tm, wd), lambda bi, i: (bi, i, 0))]
    args = [up]
    if has_prev:
        r = tm // HALO
        in_specs.append(pl.BlockSpec((None, HALO, wd), lambda bi, i: (bi, jnp.maximum(i * r - 1, 0), 0)))
        args.append(up)
    in_specs += [
        pl.BlockSpec((None, HALO, wd), lambda bi, i: (bi, 0, 0)),
        pl.BlockSpec((POOL_GROUPS, gw, gw), lambda bi, i: (0, 0, 0)),
        pl.BlockSpec((1, wd), lambda bi, i: (0, 0)),
    ]
    args += [prefix16, w_pool, s_pool.reshape(1, wd)]
    return pl.pallas_call(
        functools.partial(_pool_kernel, tm=tm, pos0=pos0, has_prev=has_prev),
        grid=(b, t // tm),
        in_specs=in_specs,
        out_specs=pl.BlockSpec((None, tm, wd), lambda bi, i: (bi, i, 0)),
        out_shape=jax.ShapeDtypeStruct((b, t, wd), BF16),
        scratch_shapes=[pltpu.VMEM((HALO + tm, wd), F32)],
        compiler_params=_cparams(2),
        name="pool_mixer",
    )(*args)


PEEL_BRACKET = 8
UNTESTED_HALVINGS = 9
UNTESTED_WALK_STEPS = 3


def _select_threshold(count_ge, max_below, rmin, rmax, n_valid, kq):
    def any_row(pred):
        return jnp.max(jnp.where(pred, 1.0, 0.0))

    def halve(lo, hi, cl, ch):
        mid = lo + 0.5 * (hi - lo)
        stuck = (mid <= lo) | (mid >= hi)
        c = count_ge(mid)
        ge = c >= kq
        return (jnp.where(ge, mid, lo), jnp.where(ge, hi, mid), jnp.where(ge, c, cl), jnp.where(ge, ch, c),
                jnp.where(stuck, 1.0, 0.0))

    def halving_loop(st, bracket_cap):
        def open_rows(cl, ch, stuck):
            return any_row((cl != kq) & (stuck <= 0.0) & (cl - ch > bracket_cap))

        def cond(st):
            return jnp.logical_and(st[0] < MAX_SEARCH_STEPS, st[1] > 0.0)

        def body(st):
            steps, _, lo, hi, cl, ch, _ = st
            lo, hi, cl, ch, stuck = halve(lo, hi, cl, ch)
            return steps + 1, open_rows(cl, ch, stuck), lo, hi, cl, ch, stuck

        st = (st[0], open_rows(st[4], st[5], st[6])) + tuple(st[2:])
        return lax.while_loop(cond, body, st)

    hi0 = rmax + jnp.maximum(jnp.abs(rmax) * 1e-6, 1e-30)
    zero = jnp.zeros_like(rmax)
    st = lax.fori_loop(0, UNTESTED_HALVINGS, lambda _, st: halve(*st[:4]), (rmin, hi0, n_valid, zero, zero))
    st = halving_loop((jnp.int32(0), jnp.float32(0.0)) + tuple(st), float(PEEL_BRACKET))
    _, _, lo, hi, cl, ch, stuck = st

    walk = (cl != kq) & (stuck <= 0.0)

    def walk_cond(st):
        return jnp.logical_and(st[0] < PEEL_BRACKET, st[1] > 0.0)

    def walk_step(top, n_top):
        active = walk & (n_top < kq)
        return jnp.where(active, max_below(top), top), jnp.where(active, n_top + 1.0, n_top)

    def walk_body(st):
        steps, _, top, n_top = st
        top, n_top = walk_step(top, n_top)
        return steps + 1, any_row(walk & (n_top < kq)), top, n_top

    top, n_top = lax.fori_loop(0, UNTESTED_WALK_STEPS, lambda _, st: walk_step(*st), (hi, ch))
    _, _, top, _ = lax.while_loop(walk_cond, walk_body,
                                  (jnp.int32(UNTESTED_WALK_STEPS), any_row(walk & (n_top < kq)), top, n_top))
    thr = jnp.where(walk, top, lo)
    n_ge = count_ge(thr)

    redo = n_ge != kq
    st = halving_loop((jnp.int32(0), jnp.float32(0.0), lo, hi, cl, ch, jnp.where(redo, stuck, 1.0)), 0.0)
    return jnp.where(redo, st[2], thr), jnp.where(redo, st[4], n_ge)


def _last_tied_index(count_tied_upto, need, n_keys):
    def body(_, st):
        lo, hi = st
        mid = jnp.floor((lo + hi) * 0.5)
        ok = count_tied_upto(mid) >= need
        return jnp.where(ok, lo, mid), jnp.where(ok, mid, hi)

    steps = max(1, math.ceil(math.log2(n_keys + 1)))
    _, hi = lax.fori_loop(0, steps, body, (jnp.full(need.shape, -1.0, F32), jnp.full(need.shape, n_keys - 1.0, F32)))
    return hi


def _attn_prompt_kernel(q_ref, qi_ref, wit_ref, k_ref, vt_ref, ki_ref, o_ref,
                        sc_ref, ext_ref, m_ref, l_ref, acc_ref, *, tk, topk, n_heads):
    i = pl.program_id(1)
    tq = q_ref.shape[0]
    nc = ((i + 1) * tq + tk - 1) // tk
    qpos = i * tq + lax.broadcasted_iota(jnp.int32, (1, tq), 1)
    key_row = lax.broadcasted_iota(jnp.int32, (tk, tq), 0)
    w_rows = wit_ref[...] * (IDX_DIM ** -0.5)

    def idx_chunk(c, carry):
        off = pl.multiple_of(c * tk, tk)
        kic = ki_ref[pl.ds(off, tk), :]
        acc = jnp.zeros((tk, tq), F32)
        for h in range(IDX_HEADS):
            s = lax.dot_general(kic, qi_ref[:, h * IDX_DIM:(h + 1) * IDX_DIM], NT_DIMS,
                                preferred_element_type=F32)
            acc = acc + jnp.maximum(s, 0.0) * w_rows[h:h + 1, :]
        causal = off + key_row <= qpos
        sc = jnp.where(causal, acc, -jnp.inf)
        sc_ref[pl.ds(off, tk), :] = sc
        ext_ref[0] = jnp.minimum(ext_ref[0], jnp.min(jnp.where(causal, acc, jnp.inf).reshape(tk // 8, 8, tq), axis=0))
        ext_ref[1] = jnp.maximum(ext_ref[1], jnp.max(sc.reshape(tk // 8, 8, tq), axis=0))
        return carry

    ext_ref[0] = jnp.full((8, tq), jnp.inf, F32)
    ext_ref[1] = jnp.full((8, tq), -jnp.inf, F32)

    def chunk_loop(n_chunks, chunk_fn):
        base = tq // tk
        per_trip = CHUNK_TRIP_FACTOR * base

        def run(first, count):
            for u in range(count):
                chunk_fn(first + u, 0)

        def body(t, carry):
            run(t * per_trip, per_trip)
            return carry
        lax.fori_loop(0, n_chunks // per_trip, body, 0)
        for r in range(1, CHUNK_TRIP_FACTOR):
            @pl.when(n_chunks % per_trip == r * base)
            def _():
                run((n_chunks // per_trip) * per_trip, r * base)

    chunk_loop(nc, idx_chunk)

    lanes_par = 4

    def reduce_keys(fn, init, red):
        rows = tq
        n_blocks = (nc * tk) // rows

        def step(first_row, n_rows, a):
            blk = sc_ref[pl.ds(pl.multiple_of(first_row, rows), n_rows), :]
            return fn(a, blk.reshape(lanes_par, n_rows // (8 * lanes_par), 8, tq))

        a = lax.fori_loop(0, n_blocks // 2, lambda c, a: step(c * 2 * rows, 2 * rows, a),
                          jnp.full((lanes_par, 8, tq), init, F32))
        a = lax.fori_loop(0, n_blocks % 2, lambda c, a: step((n_blocks - 1) * rows, rows, a), a)
        return red(red(a, axis=0), axis=0, keepdims=True)

    def count_ge(t):
        return reduce_keys(lambda a, blk: a + jnp.sum(jnp.where(blk >= t, 1.0, 0.0), axis=1), 0.0, jnp.sum)

    def max_below(t):
        return reduce_keys(lambda a, blk: jnp.maximum(a, jnp.max(jnp.where(blk < t, blk, -jnp.inf), axis=1)),
                           -jnp.inf, jnp.max)

    rmin = jnp.min(ext_ref[0], axis=0, keepdims=True)
    rmax = jnp.max(ext_ref[1], axis=0, keepdims=True)
    n_valid = (qpos + 1).astype(F32)
    kq = jnp.minimum(float(topk), n_valid)
    thr, n_ge = _select_threshold(count_ge, max_below, rmin, rmax, n_valid, kq)

    @pl.when(jnp.max(n_ge - kq) > 0.0)
    def _():
        def key_sum(fn):
            def body(c, a):
                off = pl.multiple_of(c * tk, tk)
                hit = fn(sc_ref[pl.ds(off, tk), :], (off + key_row).astype(F32))
                return a + jnp.sum(jnp.where(hit, 1.0, 0.0), axis=0, keepdims=True)
            return lax.fori_loop(0, nc, body, jnp.zeros((1, tq), F32))

        need = kq - key_sum(lambda blk, key: blk > thr)
        last = _last_tied_index(lambda j: key_sum(lambda blk, key: (blk == thr) & (key <= j)), need,
                                sc_ref.shape[0])

        def drop(c, carry):
            off = pl.multiple_of(c * tk, tk)
            blk = sc_ref[pl.ds(off, tk), :]
            beyond = (blk == thr) & ((off + key_row).astype(F32) > last)
            sc_ref[pl.ds(off, tk), :] = jnp.where(beyond, -jnp.inf, blk)
            return carry
        lax.fori_loop(0, nc, drop, 0)

    m_ref[...] = jnp.full(m_ref.shape, -jnp.inf, F32)
    l_ref[...] = jnp.zeros(l_ref.shape, F32)
    acc_ref[...] = jnp.zeros(acc_ref.shape, F32)

    def att_chunk(c, carry):
        off = pl.multiple_of(c * tk, tk)
        mask = sc_ref[pl.ds(off, tk), :] >= thr
        for h in range(n_heads):
            hs = slice(h * HEAD_DIM, (h + 1) * HEAD_DIM)
            s = lax.dot_general(k_ref[pl.ds(off, tk), hs], q_ref[:, hs], NT_DIMS,
                                preferred_element_type=F32)
            s = jnp.where(mask, s, NEG)
            m_prev = m_ref[h]
            m_new = jnp.maximum(m_prev, jnp.max(s, axis=0, keepdims=True))
            alpha = jnp.exp2(m_prev - m_new)
            p = jnp.exp2(s - m_new)
            l_ref[h] = alpha * l_ref[h] + jnp.sum(p, axis=0, keepdims=True)
            m_ref[h] = m_new
            pv = jnp.dot(vt_ref[c, hs, :], p.astype(BF16), preferred_element_type=F32)
            acc_ref[h] = alpha * acc_ref[h] + pv
        return carry

    chunk_loop(nc, att_chunk)

    for h in range(n_heads):
        hs = slice(h * HEAD_DIM, (h + 1) * HEAD_DIM)
        o_ref[:, hs] = (acc_ref[h] / l_ref[h]).T.astype(o_ref.dtype)


def attn_prompt(q, qi, wit, k, vt, ki, *, topk):
    b, s, aw = q.shape
    n_heads = aw // HEAD_DIM
    tq = ATTN_TQ
    tk = vt.shape[3]
    kern = functools.partial(_attn_prompt_kernel, tk=ATTN_TK, topk=topk, n_heads=n_heads)
    return pl.pallas_call(
        kern,
        grid=(b, s // tq),
        in_specs=[
            pl.BlockSpec((None, tq, aw), lambda bi, i: (bi, i, 0)),
            pl.BlockSpec((None, tq, qi.shape[2]), lambda bi, i: (bi, i, 0)),
            pl.BlockSpec((None, IDX_HEADS, tq), lambda bi, i: (bi, 0, i)),
            pl.BlockSpec((None, s, aw), lambda bi, i: (bi, 0, 0)),
            pl.BlockSpec((None, s // tk, aw, tk), lambda bi, i: (bi, 0, 0, 0)),
            pl.BlockSpec((None, s, IDX_DIM), lambda bi, i: (bi, 0, 0)),
        ],
        out_specs=pl.BlockSpec((None, tq, aw), lambda bi, i: (bi, i, 0)),
        out_shape=jax.ShapeDtypeStruct((b, s, aw), BF16),
        scratch_shapes=[
            pltpu.VMEM((s, tq), F32),
            pltpu.VMEM((2, 8, tq), F32),
            pltpu.VMEM((n_heads, 1, tq), F32),
            pltpu.VMEM((n_heads, 1, tq), F32),
            pltpu.VMEM((n_heads, HEAD_DIM, tq), F32),
        ],
        compiler_params=_cparams(2),
        name="attn_prompt",
    )(q, qi, wit, k, vt, ki)


def _idx_sample_kernel(pt_ref, qi_ref, w_ref, *rest, n_pages, group, t_new, topk):
    cik_refs = rest[:group]
    kin_ref, sc_ref, thr_ref = rest[group:]
    b, p = pl.program_id(0), pl.program_id(1)
    db = sc_ref.shape[0]
    rows = qi_ref.shape[0]

    def scores(keys_bf16):
        s = lax.dot_general(qi_ref[...], keys_bf16, NT_DIMS, preferred_element_type=F32)
        r = jnp.maximum(s, 0.0) * w_ref[...]
        return jnp.sum(r.reshape(rows // t_new, t_new, LANES), axis=0)

    keys = jnp.concatenate([r[...].astype(BF16) for r in cik_refs], axis=0)
    s_all = lax.dot_general(qi_ref[...], keys, NT_DIMS, preferred_element_type=F32)
    for g in range(group):
        r = jnp.maximum(s_all[:, g * LANES:(g + 1) * LANES], 0.0) * w_ref[...]
        sc_ref[b, p * group + g] = jnp.sum(r.reshape(rows // t_new, t_new, LANES), axis=0)

    @pl.when(p == 0)
    def _():
        s = scores(kin_ref[...])
        tok = lax.broadcasted_iota(jnp.int32, (t_new, LANES), 0)
        lane = lax.broadcasted_iota(jnp.int32, (t_new, LANES), 1)
        sc_ref[b, n_pages] = jnp.where((lane <= tok) & (lane < t_new), s, -jnp.inf)

    @pl.when((b == db - 1) & (p == pl.num_programs(1) - 1))
    def _():
        def reduce_keys(fn, red):
            return red(red(fn(sc_ref[...]), axis=1), axis=2, keepdims=True)

        def count_ge(t):
            return reduce_keys(lambda x: jnp.where(x >= t[:, None], 1.0, 0.0), jnp.sum)

        def max_below(t):
            return reduce_keys(lambda x: jnp.where(x < t[:, None], x, -jnp.inf), jnp.max)

        rmax = reduce_keys(lambda x: x, jnp.max)
        rmin = reduce_keys(lambda x: jnp.where(x == -jnp.inf, jnp.inf, x), jnp.min)
        tokc = lax.broadcasted_iota(jnp.int32, (db, t_new, 1), 1)
        n_valid = (n_pages * LANES + tokc + 1).astype(F32)
        kq = jnp.minimum(float(topk), n_valid)
        thr, n_ge = _select_threshold(count_ge, max_below, rmin, rmax, n_valid, kq)
        thr_ref[...] = jnp.broadcast_to(thr, thr_ref.shape)

        @pl.when(jnp.max(n_ge - kq) > 0.0)
        def _():
            n_slabs = n_pages + 1
            key = (lax.broadcasted_iota(jnp.int32, (1, n_slabs, 1, LANES), 1) * LANES
                   + lax.broadcasted_iota(jnp.int32, (1, n_slabs, 1, LANES), 3)).astype(F32)
            thr4 = thr[:, None]

            def key_sum(hit):
                return jnp.sum(jnp.sum(jnp.where(hit, 1.0, 0.0), axis=1), axis=2, keepdims=True)

            need = kq - key_sum(sc_ref[...] > thr4)
            last = _last_tied_index(lambda j: key_sum((sc_ref[...] == thr4) & (key <= j[:, None])), need,
                                    n_slabs * LANES)
            sc = sc_ref[...]
            sc_ref[...] = jnp.where((sc == thr4) & (key > last[:, None]), -jnp.inf, sc)


def _page_map(b, p, pt, *, g, group):
    return (0, pt[b, p * group + g], 0, 0)


def idx_sample(page_table, qi_ht, w_ht, cache_idx_k, ki_new_pad, *, t_new, topk):
    db, n_pages = page_table.shape
    group = math.gcd(n_pages, IDX_PAGES_PER_STEP)
    rows = qi_ht.shape[1]
    page = cache_idx_k.shape[2]
    assert page == LANES
    kern = functools.partial(_idx_sample_kernel, n_pages=n_pages, group=group, t_new=t_new, topk=topk)
    page_specs = [pl.BlockSpec((None, None, page, IDX_DIM), functools.partial(_page_map, g=g, group=group))
                  for g in range(group)]
    grid_spec = pltpu.PrefetchScalarGridSpec(
        num_scalar_prefetch=1,
        grid=(db, n_pages // group),
        in_specs=[
            pl.BlockSpec((None, rows, IDX_DIM), lambda b, p, pt: (b, 0, 0)),
            pl.BlockSpec((None, rows, LANES), lambda b, p, pt: (b, 0, 0)),
            *page_specs,
            pl.BlockSpec((None, page, IDX_DIM), lambda b, p, pt: (b, 0, 0)),
        ],
        out_specs=[
            pl.BlockSpec((db, n_pages + 1, t_new, LANES), lambda b, p, pt: (0, 0, 0, 0)),
            pl.BlockSpec((db, t_new, LANES), lambda b, p, pt: (0, 0, 0)),
        ],
    )
    return pl.pallas_call(
        kern,
        grid_spec=grid_spec,
        out_shape=[jax.ShapeDtypeStruct((db, n_pages + 1, t_new, LANES), F32),
                   jax.ShapeDtypeStruct((db, t_new, LANES), F32)],
        compiler_params=_cparams(2),
        name="idx_sample",
    )(page_table, qi_ht, w_ht, *([cache_idx_k] * group), ki_new_pad)


def _attn_sample_kernel(pt_ref, q_ref, *rest, group, page, n_heads):
    ck_refs = rest[:group]
    cv_refs = rest[group:2 * group]
    kn_ref, vn_ref, sc_ref, scn_ref, thr_ref, o_ref, m_ref, l_ref, acc_ref = rest[2 * group:]
    p = pl.program_id(1)

    @pl.when(p == 0)
    def _():
        m_ref[...] = jnp.full(m_ref.shape, -jnp.inf, F32)
        l_ref[...] = jnp.zeros(l_ref.shape, F32)
        acc_ref[...] = jnp.zeros(acc_ref.shape, F32)

    thr = thr_ref[...]

    def head_rows(ref, h):
        return ref[pl.ds(h, page, stride=n_heads), :].astype(BF16)

    def attend(k_refs, v_refs, sc, sc_thr):
        n = len(k_refs)
        s = jnp.stack([
            lax.dot_general(q_ref[h], jnp.concatenate([head_rows(r, h) for r in k_refs], axis=0), NT_DIMS,
                            preferred_element_type=F32)
            for h in range(n_heads)])
        s = jnp.where(sc[None] >= sc_thr[None], s, NEG)
        m_prev = m_ref[...]
        m_new = jnp.maximum(m_prev, jnp.max(s, axis=2, keepdims=True))
        alpha = jnp.exp2(m_prev - m_new)
        pr = jnp.exp2(s - jnp.concatenate([m_new] * n, axis=2))
        l_ref[...] = alpha * l_ref[...] + jnp.sum(pr, axis=2, keepdims=True)
        m_ref[...] = m_new
        pb = pr.astype(BF16)
        for h in range(n_heads):
            vh = jnp.concatenate([head_rows(r, h) for r in v_refs], axis=0)
            acc_ref[h] = alpha[h] * acc_ref[h] + jnp.dot(pb[h], vh, preferred_element_type=F32)

    attend(ck_refs, cv_refs, jnp.concatenate([sc_ref[g] for g in range(group)], axis=1),
           jnp.concatenate([thr] * group, axis=1))

    @pl.when(p == pl.num_programs(1) - 1)
    def _():
        attend([kn_ref], [vn_ref], scn_ref[...], thr)
        o_ref[...] = acc_ref[...] / l_ref[...]


def attn_sample(page_table, q_ht, cache_k, cache_v, k_new_pad, v_new_pad, sc, thr):
    db, n_pages = page_table.shape
    n_heads, t_new = q_ht.shape[1], q_ht.shape[2]
    prow = cache_k.shape[2]
    page = prow // n_heads
    group = math.gcd(n_pages, KV_PAGES_PER_STEP)
    kern = functools.partial(_attn_sample_kernel, group=group, page=page, n_heads=n_heads)
    cache_specs = [pl.BlockSpec((None, None, prow, HEAD_DIM), functools.partial(_page_map, g=g, group=group))
                   for g in range(group)]
    new_spec = pl.BlockSpec((None, prow, HEAD_DIM), lambda b, p, pt: (b, 0, 0))
    head_spec = pl.BlockSpec((None, n_heads, t_new, HEAD_DIM), lambda b, p, pt: (b, 0, 0, 0))
    grid_spec = pltpu.PrefetchScalarGridSpec(
        num_scalar_prefetch=1,
        grid=(db, n_pages // group),
        in_specs=[
            head_spec,
            *cache_specs, *cache_specs, new_spec, new_spec,
            pl.BlockSpec((None, group, t_new, LANES), lambda b, p, pt: (b, p, 0, 0)),
            pl.BlockSpec((None, None, t_new, LANES), lambda b, p, pt: (b, n_pages, 0, 0)),
            pl.BlockSpec((None, t_new, LANES), lambda b, p, pt: (b, 0, 0)),
        ],
        out_specs=head_spec,
        scratch_shapes=[
            pltpu.VMEM((n_heads, t_new, LANES), F32),
            pltpu.VMEM((n_heads, t_new, LANES), F32),
            pltpu.VMEM((n_heads, t_new, HEAD_DIM), F32),
        ],
    )
    return pl.pallas_call(
        kern,
        grid_spec=grid_spec,
        out_shape=jax.ShapeDtypeStruct((db, n_heads, t_new, HEAD_DIM), F32),
        compiler_params=_cparams(2),
        name="attn_sample",
    )(page_table, q_ht, *([cache_k] * group), *([cache_v] * group), k_new_pad, v_new_pad, sc, sc, thr)


def _outproj_kernel(x_ref, yp_ref, ya_ref, wt_ref, wb_ref, g_ref, x2_ref, h2_ref):
    y = jnp.dot(yp_ref[...], wt_ref[...], preferred_element_type=F32)
    y = y + jnp.dot(ya_ref[...], wb_ref[...], preferred_element_type=F32)
    x2 = x_ref[...] + y
    x2_ref[...] = x2
    h2_ref[...] = _rms(x2, g_ref[...]).astype(h2_ref.dtype)


def outproj(x, yp, ya, w_out, g, *, tm):
    m, d = x.shape
    kp, ka = yp.shape[1], ya.shape[1]
    assert kp == ka and w_out.shape == (kp + ka, d)
    return pl.pallas_call(
        _outproj_kernel,
        grid=(m // tm,),
        in_specs=[
            pl.BlockSpec((tm, d), lambda i: (i, 0)),
            pl.BlockSpec((tm, kp), lambda i: (i, 0)),
            pl.BlockSpec((tm, ka), lambda i: (i, 0)),
            pl.BlockSpec((kp, d), lambda i: (0, 0)),
            pl.BlockSpec((ka, d), lambda i: (1, 0)),
            pl.BlockSpec((1, d), lambda i: (0, 0)),
        ],
        out_specs=[pl.BlockSpec((tm, d), lambda i: (i, 0)), pl.BlockSpec((tm, d), lambda i: (i, 0))],
        out_shape=[jax.ShapeDtypeStruct((m, d), F32), jax.ShapeDtypeStruct((m, d), BF16)],
        compiler_params=_cparams(1),
        name="outproj",
    )(x, yp, ya, w_out, w_out, g.reshape(1, d))


def _ffn_kernel(h2_ref, wg_ref, wu_ref, wd_ref, x2_ref, g_ref, o_ref, acc_ref):
    f = pl.program_id(1)

    @pl.when(f == 0)
    def _():
        acc_ref[...] = jnp.zeros(acc_ref.shape, F32)

    h2 = h2_ref[...]
    gate = jnp.dot(h2, wg_ref[...], preferred_element_type=F32)
    up = jnp.dot(h2, wu_ref[...], preferred_element_type=F32)
    a = (gate * jax.nn.sigmoid(gate) * up).astype(BF16)
    acc_ref[...] += jnp.dot(a, wd_ref[...], preferred_element_type=F32)

    @pl.when(f == pl.num_programs(1) - 1)
    def _():
        o_ref[...] = _rms(x2_ref[...] + acc_ref[...], g_ref[...])


def ffn(h2, wg, wu, wd, x2, g, *, tm, tf):
    m, d = h2.shape
    dff = wg.shape[1]
    return pl.pallas_call(
        _ffn_kernel,
        grid=(m // tm, dff // tf),
        in_specs=[
            pl.BlockSpec((tm, d), lambda i, f: (i, 0)),
            pl.BlockSpec((d, tf), lambda i, f: (0, f)),
            pl.BlockSpec((d, tf), lambda i, f: (0, f)),
            pl.BlockSpec((tf, d), lambda i, f: (f, 0)),
            pl.BlockSpec((tm, d), lambda i, f: (i, 0)),
            pl.BlockSpec((1, d), lambda i, f: (0, 0)),
        ],
        out_specs=pl.BlockSpec((tm, d), lambda i, f: (i, 0)),
        out_shape=jax.ShapeDtypeStruct((m, d), F32),
        scratch_shapes=[pltpu.VMEM((tm, d), F32)],
        compiler_params=_cparams(2),
        name="ffn",
    )(h2, wg, wu, wd, x2, g.reshape(1, d))


def _rope_tables(pos):
    half = HEAD_DIM // 2
    inv = ROPE_THETA ** (-jnp.arange(half, dtype=F32) / half)
    ang = pos.astype(F32)[:, None] * inv[None, :]
    cos, sin = jnp.cos(ang), jnp.sin(ang)
    return jnp.concatenate([cos, cos], axis=-1), jnp.concatenate([-sin, sin], axis=-1)


def _mix_inputs(x2d, pos_rows, g_mix, w_in16, w_tail, *, tm, vt_chunk=None):
    cos, sin = _rope_tables(pos_rows)
    return mixproj(x2d, g_mix, w_in16, w_tail, cos, sin, tm=tm, vt_chunk=vt_chunk)


def kernel(x_prompt, x_sample, cache_k, cache_v, cache_idx_k, state_pool, page_table, g_mix, w_in, w_pool,
           s_pool, w_out, g_ffn, w_gate, w_up, w_down, g_final):
    B, S, D = x_prompt.shape
    DB, T, _ = x_sample.shape
    depth = w_in.shape[0]
    assert depth == 1
    n_pages = page_table.shape[1]
    page = cache_k.shape[2]
    n_heads = cache_k.shape[3]
    aw = n_heads * HEAD_DIM
    pw = w_pool.shape[1] * w_pool.shape[2]
    past = n_pages * page
    l = 0

    wide = pw + 3 * aw + IDX_HEADS * IDX_DIM
    assert (pw, aw, wide) == (PROJ_TN, PROJ_TN, 6 * PROJ_TN) and w_in.shape[2] == wide + IDX_DIM + IDX_HEADS
    w_in16 = w_in[l].astype(BF16)
    w_tail = jnp.pad(w_in16[:, wide:], ((0, 0), (0, LANES - IDX_HEADS)))
    w_pool16 = w_pool[l].astype(BF16)
    w_out16 = w_out[l].astype(BF16)
    wg16, wu16, wd16 = w_gate[l].astype(BF16), w_up[l].astype(BF16), w_down[l].astype(BF16)

    xp2 = x_prompt.reshape(B * S, D)
    pos_p = jnp.arange(S)
    tk = ATTN_TK
    up, q, k32, k16, v32, qi, ki32, ki16, wi, vt = _mix_inputs(xp2, pos_p, g_mix[l], w_in16, w_tail, tm=ROW_TILE,
                                                               vt_chunk=tk)
    up3 = up.reshape(B, S, pw)
    yp = pool_mixer(up3, jnp.zeros((B, HALO, pw), F32), w_pool16, s_pool[l], tm=ROW_TILE, pos0=0)
    wit = wi.reshape(B, S, LANES)[:, :, :IDX_HEADS].transpose(0, 2, 1)
    ya = attn_prompt(q.reshape(B, S, aw), qi.reshape(B, S, -1), wit, k16.reshape(B, S, aw),
                     vt.reshape(B, S // tk, aw, tk), ki16.reshape(B, S, IDX_DIM), topk=min(TOPK_MAX, S // 4))
    x2, h2 = outproj(xp2, yp.reshape(B * S, pw), ya.reshape(B * S, aw), w_out16, g_ffn[l], tm=ROW_TILE)
    y_prompt = ffn(h2, wg16, wu16, wd16, x2, g_final, tm=ROW_TILE, tf=FFN_TILE).reshape(B, S, D)

    k_prompt = k32.reshape(1, B, S, n_heads, HEAD_DIM)
    v_prompt = v32.reshape(1, B, S, n_heads, HEAD_DIM)
    idx_k_prompt = ki32.reshape(1, B, S, IDX_DIM)
    pool_prompt = up3[:, S - POOL_STATE:, :][None]

    M = DB * T
    xs2 = x_sample.reshape(M, D)
    pos_s = jnp.tile(past + jnp.arange(T), DB)
    ups, qs, ks32, _, vs32, qis, kis32, kis16, wis = _mix_inputs(xs2, pos_s, g_mix[l], w_in16, w_tail, tm=M)
    ups3 = ups.reshape(DB, T, pw)
    prefix16 = jnp.pad(state_pool[l], ((0, 0), (HALO - POOL_STATE, 0), (0, 0)))
    yps = pool_mixer(ups3, prefix16, w_pool16, s_pool[l], tm=T, pos0=past)

    qi_ht = qis.reshape(DB, T, IDX_HEADS, IDX_DIM).transpose(0, 2, 1, 3).reshape(DB, IDX_HEADS * T, IDX_DIM)
    w_ht = wis.reshape(DB, T, LANES)[:, :, :IDX_HEADS].transpose(0, 2, 1).reshape(DB, IDX_HEADS * T, 1)
    w_ht = jnp.broadcast_to(w_ht * (IDX_DIM ** -0.5), (DB, IDX_HEADS * T, LANES))
    pad_rows = lambda a: jnp.pad(a.reshape(DB, T, -1), ((0, 0), (0, page - T), (0, 0)))
    sc, thr = idx_sample(page_table, qi_ht, w_ht, cache_idx_k, pad_rows(kis16), t_new=T,
                         topk=min(TOPK_MAX, (past + T) // 4))

    n_phys = cache_k.shape[1]
    key_head_rows = lambda a: pad_rows(a).reshape(DB, page * n_heads, HEAD_DIM)
    q_ht = qs.reshape(DB, T, n_heads, HEAD_DIM).transpose(0, 2, 1, 3)
    o_ht = attn_sample(page_table, q_ht, cache_k.reshape(depth, n_phys, page * n_heads, HEAD_DIM),
                       cache_v.reshape(depth, n_phys, page * n_heads, HEAD_DIM),
                       key_head_rows(ks32), key_head_rows(vs32), sc, thr)
    yas = o_ht.transpose(0, 2, 1, 3).reshape(M, aw).astype(BF16)

    x2s, h2s = outproj(xs2, yps.reshape(M, pw), yas, w_out16, g_ffn[l], tm=M)
    y_sample = ffn(h2s, wg16, wu16, wd16, x2s, g_final, tm=M, tf=FFN_TILE).reshape(DB, T, D)

    k_sample = ks32.reshape(1, DB, T, n_heads, HEAD_DIM)
    v_sample = vs32.reshape(1, DB, T, n_heads, HEAD_DIM)
    idx_k_sample = kis32.reshape(1, DB, T, IDX_DIM)
    pool_sample = jnp.concatenate([state_pool[l][:, T:, :], ups3], axis=1)[None]

    return (y_prompt, y_sample, k_prompt, v_prompt, idx_k_prompt, pool_prompt,
            k_sample, v_sample, idx_k_sample, pool_sample)
```
